```python
import math
import jax, jax.numpy as jnp
from jax import lax
import numpy as np

D_MODEL = 1024
BATCH = 8
SEQ = 2048
DEPTH = 2
DEC_BATCH = 128
DEC_SEQ = 8
PAST_LEN = 16384
PAGE_SIZE = 128

S5_WIDTH = D_MODEL // 2
S5_GROUP = 16
S5_GROUPS = S5_WIDTH // S5_GROUP
S5_STATE = 64
CONV_WIDTH = D_MODEL // 2
CONV_K = 3
M_WIDTH = D_MODEL
M_HEADS = 4
M_HEAD_DIM = M_WIDTH // M_HEADS
M_CHUNK = 64

NORM_EPS = 1e-6
NEG = -1e30

IN_SPLITS = (S5_WIDTH, S5_WIDTH,
             CONV_WIDTH, CONV_WIDTH, CONV_WIDTH, CONV_WIDTH,
             M_WIDTH, M_WIDTH, M_WIDTH, M_WIDTH, M_WIDTH,
             M_HEADS, M_HEADS,
             D_MODEL, D_MODEL, D_MODEL)
IN_WIDTH = sum(IN_SPLITS)
IN_OFFSETS = tuple(int(v) for v in np.cumsum(IN_SPLITS)[:-1])

kernel_name = 'hybrid_s5_conv_mlstm_gated_step'


def rmsnorm(x, w):
    xf = x.astype(jnp.float32)
    y = xf * lax.rsqrt(jnp.mean(xf * xf, axis=-1, keepdims=True) + NORM_EPS)
    return (y * w.astype(jnp.float32)).astype(x.dtype)


def s5_mixer(u, s0_re, s0_im, lam_re, lam_im, log_dt, b_re, b_im, c_re, c_im, d_skip, w_glu):
    f32 = jnp.float32
    bsz, T, _ = u.shape
    uf = u.astype(f32)
    ug = uf.reshape(bsz, T, S5_GROUPS, S5_GROUP)
    lam = lax.complex(lam_re.astype(f32), lam_im.astype(f32))
    dt = jnp.exp(log_dt.astype(f32))[:, None]
    lam_dt = lam * dt
    a_bar = jnp.exp(lam_dt)
    b = lax.complex(b_re.astype(f32), b_im.astype(f32))
    b_bar = ((a_bar - 1.0) / lam)[..., None] * b
    bu = lax.complex(jnp.einsum('gpc,btgc->btgp', b_bar.real, ug),
                     jnp.einsum('gpc,btgc->btgp', b_bar.imag, ug))
    a_seq = jnp.broadcast_to(a_bar, bu.shape)

    def combine(e1, e2):
        a1, x1 = e1
        a2, x2 = e2
        return a1 * a2, a2 * x1 + x2

    _, s_zero = lax.associative_scan(combine, (a_seq, bu), axis=1)
    steps = jnp.arange(1, T + 1, dtype=f32)[:, None, None]
    a_pow = jnp.exp(lam_dt[None] * steps)
    s0 = lax.complex(s0_re.astype(f32), s0_im.astype(f32))
    s = s_zero + a_pow[None] * s0[:, None]
    y = (jnp.einsum('gcp,btgp->btgc', c_re.astype(f32), s.real)
         - jnp.einsum('gcp,btgp->btgc', c_im.astype(f32), s.imag))
    y = y.reshape(bsz, T, S5_WIDTH) + d_skip.astype(f32) * uf
    g = jax.nn.gelu(y)
    out = g * jax.nn.sigmoid(g @ w_glu.astype(f32))
    s_last = s[:, -1]
    return out.astype(u.dtype), s_last.real.astype(s0_re.dtype), s_last.imag.astype(s0_im.dtype)


def short_conv_mixer(gate_b, gate_c, xt, buf, conv_w):
    v = gate_c * xt
    T = v.shape[1]
    full = jnp.concatenate([buf.astype(v.dtype), v], axis=1)
    y = sum(conv_w[k] * full[:, k:k + T] for k in range(CONV_K))
    return gate_b * y, full[:, T:].astype(buf.dtype)


def mlstm_mixer(q, k, v, i_pre, f_pre, c0, n0, m0):
    f32 = jnp.float32
    bsz, T = q.shape[:2]
    L = min(M_CHUNK, T)
    n_chunks = -(-T // L)
    pad = n_chunks * L - T

    def prep(a, fill):
        a = a.astype(f32)
        a = jnp.pad(a, [(0, 0), (0, pad)] + [(0, 0)] * (a.ndim - 2), constant_values=fill)
        a = a.reshape((bsz, n_chunks, L) + a.shape[2:])
        return jnp.swapaxes(jnp.moveaxis(a, 1, 0), 2, 3)

    qs, ks_, vs = prep(q, 0.0), prep(k, 0.0), prep(v, 0.0)
    lis = prep(i_pre, NEG)
    lfs = prep(jax.nn.log_sigmoid(i_pre.dtype.type(0) + f_pre.astype(f32)), 0.0)
    causal = jnp.tril(jnp.ones((L, L), dtype=bool))

    def step(carry, inp):
        c, n, m = carry
        qc, kc, vc, li, lf = inp
        b = jnp.cumsum(lf, axis=-1)
        log_d = b[..., :, None] - b[..., None, :] + li[..., None, :]
        log_d = jnp.where(causal, log_d, NEG)
        inter = b + m[..., None]
        m_t = jnp.maximum(inter, jnp.max(log_d, axis=-1))
        sc = jnp.einsum('bhld,bhjd->bhlj', qc, kc) * jnp.exp(log_d - m_t[..., None])
        w_inter = jnp.exp(inter - m_t)
        num = (w_inter[..., None] * jnp.einsum('bhld,bhde->bhle', qc, c)
               + jnp.einsum('bhlj,bhje->bhle', sc, vc))
        den = w_inter * jnp.einsum('bhld,bhd->bhl', qc, n) + jnp.sum(sc, axis=-1)
        h = num / jnp.maximum(jnp.abs(den), jnp.exp(-m_t))[..., None]
        b_last = b[..., -1]
        log_w = b_last[..., None] - b + li
        m_new = jnp.maximum(b_last + m, jnp.max(log_w, axis=-1))
        w = jnp.exp(log_w - m_new[..., None])
        decay = jnp.exp(b_last + m - m_new)
        c_new = decay[..., None, None] * c + jnp.einsum('bhl,bhld,bhle->bhde', w, kc, vc)
        n_new = decay[..., None] * n + jnp.einsum('bhl,bhld->bhd', w, kc)
        return (c_new, n_new, m_new), h

    (c_f, n_f, m_f), hs = lax.scan(step, (c0.astype(f32), n0.astype(f32), m0.astype(f32)),
                                   (qs, ks_, vs, lis, lfs))
    hs = jnp.moveaxis(jnp.swapaxes(hs, 2, 3), 0, 1)
    hs = hs.reshape(bsz, n_chunks * L, M_HEADS, M_HEAD_DIM)[:, :T]
    return hs, c_f.astype(c0.dtype), n_f.astype(n0.dtype), m_f.astype(m0.dtype)


def headwise_norm(h, w):
    mu = jnp.mean(h, axis=-1, keepdims=True)
    var = jnp.mean(jnp.square(h - mu), axis=-1, keepdims=True)
    return (h - mu) * lax.rsqrt(var + NORM_EPS) * w.astype(jnp.float32).reshape(M_HEADS, M_HEAD_DIM)


def hybrid_layer(x, s_re, s_im, conv_buf, c_m, n_m, m_m,
                 norm_w, w_in, i_bias, f_bias, lam_re, lam_im, log_dt, b_re, b_im,
                 c_re, c_im, d_skip, w_glu, w_proj_s, conv_w, w_proj_c,
                 mlstm_norm_w, w_proj_m, w_out):
    bsz, T, _ = x.shape
    h = rmsnorm(x, norm_w)
    proj = h @ w_in
    (s_u, s_z, c_b, c_c, c_x, c_z, m_q, m_k, m_v, m_o, m_z, m_i, m_f,
     g_s, g_c, g_m) = jnp.split(proj, IN_OFFSETS, axis=-1)
    y_s, s_re_new, s_im_new = s5_mixer(s_u, s_re, s_im, lam_re, lam_im, log_dt,
                                       b_re, b_im, c_re, c_im, d_skip, w_glu)
    y_s = (y_s * jax.nn.silu(s_z)) @ w_proj_s
    y_c, conv_new = short_conv_mixer(c_b, c_c, c_x, conv_buf, conv_w)
    y_c = (y_c * jax.nn.silu(c_z)) @ w_proj_c
    heads = lambda a: a.reshape(bsz, T, M_HEADS, M_HEAD_DIM)
    hm, c_new, n_new, m_new = mlstm_mixer(heads(m_q), heads(m_k) * (M_HEAD_DIM ** -0.5), heads(m_v),
                                          m_i + i_bias, m_f + f_bias, c_m, n_m, m_m)
    hm = hm * jax.nn.sigmoid(heads(m_o).astype(jnp.float32))
    hm = headwise_norm(hm, mlstm_norm_w).reshape(bsz, T, M_WIDTH).astype(x.dtype)
    y_m = (hm * jax.nn.silu(m_z)) @ w_proj_m
    merged = jax.nn.sigmoid(g_s) * y_s + jax.nn.sigmoid(g_c) * y_c + jax.nn.sigmoid(g_m) * y_m
    return (x + merged @ w_out, s_re_new, s_im_new, conv_new, c_new, n_new, m_new)


def setup_inputs(seed: int = 0) -> dict:
    key = jax.random.key(seed)
    ks = jax.random.split(key, 32)
    f32 = jnp.float32
    nrm = lambda k, shape, scale: scale * jax.random.normal(k, shape, f32)
    n_idx = jnp.arange(S5_STATE, dtype=f32)
    s5_shape = (DEPTH, S5_GROUPS, S5_STATE)
    return {
        'x_prompt': nrm(ks[0], (BATCH, SEQ, D_MODEL), 1.0),
        'x_sample': nrm(ks[1], (DEC_BATCH, DEC_SEQ, D_MODEL), 1.0),
        'state_ssm_re': nrm(ks[2], (DEPTH, DEC_BATCH, S5_GROUPS, S5_STATE), 0.1),
        'state_ssm_im': nrm(ks[3], (DEPTH, DEC_BATCH, S5_GROUPS, S5_STATE), 0.1),
        'state_conv': nrm(ks[4], (DEPTH, DEC_BATCH, CONV_K - 1, CONV_WIDTH), 1.0),
        'state_mlstm_c': nrm(ks[5], (DEPTH, DEC_BATCH, M_HEADS, M_HEAD_DIM, M_HEAD_DIM), 0.05),
        'state_mlstm_n': nrm(ks[6], (DEPTH, DEC_BATCH, M_HEADS, M_HEAD_DIM), 0.3),
        'state_mlstm_m': nrm(ks[7], (DEPTH, DEC_BATCH, M_HEADS), 1.0),
        'norm_w': 1.0 + nrm(ks[8], (DEPTH, D_MODEL), 0.02),
        'w_in': nrm(ks[9], (DEPTH, D_MODEL, IN_WIDTH), D_MODEL ** -0.5),
        'i_bias': nrm(ks[10], (DEPTH, M_HEADS), 0.1),
        'f_bias': jnp.linspace(3.0, 6.0, M_HEADS, dtype=f32) + nrm(ks[11], (DEPTH, M_HEADS), 0.1),
        'lam_re': -0.5 + nrm(ks[12], s5_shape, 0.01),
        'lam_im': math.pi * n_idx + nrm(ks[13], s5_shape, 0.01),
        'log_dt': jax.random.uniform(ks[14], (DEPTH, S5_GROUPS), f32, math.log(1e-3), math.log(1e-1)),
        'b_re': nrm(ks[15], (DEPTH, S5_GROUPS, S5_STATE, S5_GROUP), (2 * S5_GROUP) ** -0.5),
        'b_im': nrm(ks[16], (DEPTH, S5_GROUPS, S5_STATE, S5_GROUP), (2 * S5_GROUP) ** -0.5),
        'c_re': nrm(ks[17], (DEPTH, S5_GROUPS, S5_GROUP, S5_STATE), S5_STATE ** -0.5),
        'c_im': nrm(ks[18], (DEPTH, S5_GROUPS, S5_GROUP, S5_STATE), S5_STATE ** -0.5),
        'd_skip': nrm(ks[19], (DEPTH, S5_WIDTH), 1.0),
        'w_glu': nrm(ks[20], (DEPTH, S5_WIDTH, S5_WIDTH), S5_WIDTH ** -0.5),
        'w_proj_s': nrm(ks[21], (DEPTH, S5_WIDTH, D_MODEL), S5_WIDTH ** -0.5),
        'conv_w': nrm(ks[22], (DEPTH, CONV_K, CONV_WIDTH), CONV_K ** -0.5),
        'w_proj_c': nrm(ks[23], (DEPTH, CONV_WIDTH, D_MODEL), CONV_WIDTH ** -0.5),
        'mlstm_norm_w': 1.0 + nrm(ks[24], (DEPTH, M_WIDTH), 0.02),
        'w_proj_m': nrm(ks[25], (DEPTH, M_WIDTH, D_MODEL), M_WIDTH ** -0.5),
        'w_out': nrm(ks[26], (DEPTH, D_MODEL, D_MODEL), D_MODEL ** -0.5),
        'final_norm_w': 1.0 + nrm(ks[27], (D_MODEL,), 0.02),
    }


def reference(x_prompt, x_sample, state_ssm_re, state_ssm_im, state_conv, state_mlstm_c,
              state_mlstm_n, state_mlstm_m, norm_w, w_in, i_bias, f_bias, lam_re, lam_im,
              log_dt, b_re, b_im, c_re, c_im, d_skip, w_glu, w_proj_s, conv_w, w_proj_c,
              mlstm_norm_w, w_proj_m, w_out, final_norm_w):
    f32 = jnp.float32
    bp = x_prompt.shape[0]
    yp, ys = x_prompt, x_sample
    p_states = [[] for _ in range(6)]
    s_states = [[] for _ in range(6)]
    for l in range(DEPTH):
        wl = (norm_w[l], w_in[l], i_bias[l], f_bias[l], lam_re[l], lam_im[l], log_dt[l],
              b_re[l], b_im[l], c_re[l], c_im[l], d_skip[l], w_glu[l], w_proj_s[l],
              conv_w[l], w_proj_c[l], mlstm_norm_w[l], w_proj_m[l], w_out[l])
        z_ssm = jnp.zeros((bp, S5_GROUPS, S5_STATE), f32)
        z_conv = jnp.zeros((bp, CONV_K - 1, CONV_WIDTH), x_prompt.dtype)
        z_c = jnp.zeros((bp, M_HEADS, M_HEAD_DIM, M_HEAD_DIM), f32)
        z_n = jnp.zeros((bp, M_HEADS, M_HEAD_DIM), f32)
        z_m = jnp.zeros((bp, M_HEADS), f32)
        yp, *sp = hybrid_layer(yp, z_ssm, z_ssm, z_conv, z_c, z_n, z_m, *wl)
        ys, *ss = hybrid_layer(ys, state_ssm_re[l], state_ssm_im[l], state_conv[l],
                               state_mlstm_c[l], state_mlstm_n[l], state_mlstm_m[l], *wl)
        for i in range(6):
            p_states[i].append(sp[i])
            s_states[i].append(ss[i])
    y_prompt = rmsnorm(yp, final_norm_w)
    y_sample = rmsnorm(ys, final_norm_w)
    ssm_re_p, ssm_re_s = jnp.stack(p_states[0]), jnp.stack(s_states[0])
    ssm_im_p, ssm_im_s = jnp.stack(p_states[1]), jnp.stack(s_states[1])
    conv_p, conv_s = jnp.stack(p_states[2]), jnp.stack(s_states[2])
    mc_p, mc_s = jnp.stack(p_states[3]), jnp.stack(s_states[3])
    mn_p, mn_s = jnp.stack(p_states[4]), jnp.stack(s_states[4])
    mm_p, mm_s = jnp.stack(p_states[5]), jnp.stack(s_states[5])
    return (y_prompt, y_sample, ssm_re_p, ssm_re_s, ssm_im_p, ssm_im_s, conv_p, conv_s,
            mc_p, mc_s, mn_p, mn_s, mm_p, mm_s)
```

```python
import functools
import math

import jax
import jax.numpy as jnp
from jax import lax
from jax.experimental import pallas as pl
from jax.experimental.pallas import tpu as pltpu

D = 1024
S5W = 512
S5G = 32
S5P = 64
S5C = 16
S5N = S5G * S5P
CW = 512
CK = 3
NH = 4
DH = 256
CHUNK = 64
EPS = 1e-6
NEG = -1e30
K_SCALE = DH ** -0.5

OFF_S = 0
OFF_M = 3072
OFF_IF = 8192
OFF_G = 8200
IN_W = 11272

F32 = jnp.float32
BF16 = jnp.bfloat16
VMEM_LIMIT = 58 * 1024 * 1024
ROW_CHUNK = 128


def _sigmoid(x):
    return 1.0 / (1.0 + jnp.exp(-x))


def _silu(x):
    return x * _sigmoid(x)


def _gelu_tanh(x):
    return 0.5 * x * (1.0 + jnp.tanh(math.sqrt(2.0 / math.pi) * (x + 0.044715 * (x * x * x))))


def _log_sigmoid(x):
    return jnp.minimum(x, 0.0) - jnp.log1p(jnp.exp(-jnp.abs(x)))


def _rmsnorm(x, w):
    return x * lax.rsqrt(jnp.mean(x * x, axis=-1, keepdims=True) + EPS) * w


def _dot(a, b):
    return jnp.dot(a, b, preferred_element_type=F32)


def _dot_nt(a, b):
    return lax.dot_general(a, b, (((1,), (1,)), ((), ())), preferred_element_type=F32)


def _dot_tn(a, b):
    return lax.dot_general(a, b, (((0,), (0,)), ((), ())), preferred_element_type=F32)


def _pitch(n):
    p = n + 8
    return p if (p // 8) % 2 == 1 else p + 8


def _s5_disc_kernel(lre_ref, lim_ref, ldt_ref, bre_ref, bim_ref, are_ref, aim_ref, bbre_ref, bbim_ref):
    lr = lre_ref[...]
    li = lim_ref[...]
    dt = jnp.exp(ldt_ref[...])
    ea = jnp.exp(lr * dt)
    ar = ea * jnp.cos(li * dt)
    ai = ea * jnp.sin(li * dt)
    are_ref[...] = ar
    aim_ref[...] = ai
    nr = ar - 1.0
    inv = 1.0 / (lr * lr + li * li)
    cr = (nr * lr + ai * li) * inv
    ci = (ai * lr - nr * li) * inv
    half = S5N // 2
    for h in range(2):
        crh = cr[:, h * half:(h + 1) * half]
        cih = ci[:, h * half:(h + 1) * half]
        br = bre_ref[h]
        bi = bim_ref[h]
        bbre_ref[h] = (crh * br - cih * bi).astype(BF16)
        bbim_ref[h] = (crh * bi + cih * br).astype(BF16)


def _s5_discretise(lam_re, lam_im, log_dt, b_re, b_im):
    depth = lam_re.shape[0]
    eye = jnp.eye(16, dtype=F32)

    def pack_b(b):
        b = b.reshape(depth, 2, 16, S5P, S5C)
        return jnp.einsum('lhgpc,gk->lhgckp', b, eye).reshape(depth, 2, 256, 1024)

    row = lambda a: a.reshape(depth, 1, S5N)
    ldt = jnp.broadcast_to(log_dt[:, :, None], (depth, S5G, S5P))
    vec = pl.BlockSpec((None, 1, S5N), lambda l: (l, 0, 0))
    mat = pl.BlockSpec((None, 2, 256, 1024), lambda l: (l, 0, 0, 0))
    return pl.pallas_call(
        _s5_disc_kernel,
        grid=(depth,),
        in_specs=[vec, vec, vec, mat, mat],
        out_specs=[vec, vec, mat, mat],
        out_shape=[jax.ShapeDtypeStruct((depth, 1, S5N), F32), jax.ShapeDtypeStruct((depth, 1, S5N), F32),
                   jax.ShapeDtypeStruct((depth, 2, 256, 1024), BF16),
                   jax.ShapeDtypeStruct((depth, 2, 256, 1024), BF16)],
        name="s5_discretise",
    )(row(lam_re), row(lam_im), row(ldt), pack_b(b_re), pack_b(b_im))


def _s5conv_kernel(has_state, bt, tt, *refs):
    refs = list(refs)
    (x_ref, nw_ref, ws_ref, wg_ref, are_ref, aim_ref, bbre_ref, bbim_ref, cre_ref, cim_ref,
     dsk_ref, wglu_ref, wps_ref, cw_ref, wpc_ref) = refs[:15]
    refs = refs[15:]
    if has_state:
        s0re_ref, s0im_ref, cv0_ref = refs[:3]
        refs = refs[3:]
    msc_ref, sre_ref, sim_ref, cvo_ref = refs[:4]
    h_scr, ps_scr, uslab, utb, bu_scr, yslab, y_scr, vhalo = refs[4:]

    r = bt * tt
    pt = _pitch(tt)
    pb = _pitch(bt)
    ti = pl.program_id(1)

    @pl.when(ti == 0)
    def _init():
        if has_state:
            sre_ref[...] = s0re_ref[...]
            sim_ref[...] = s0im_ref[...]
            vhalo[:, 6:8, :] = cv0_ref[...]
        else:
            sre_ref[...] = jnp.zeros_like(sre_ref)
            sim_ref[...] = jnp.zeros_like(sim_ref)
            vhalo[:, 6:8, :] = jnp.zeros((bt, 2, CW), F32)

    x = x_ref[...].reshape(r, D)
    h = _rmsnorm(x, nw_ref[...]).astype(BF16)
    h_scr[...] = h
    ps_scr[...] = _dot(h, ws_ref[...])

    def put_u(b, c):
        src = pl.multiple_of(b * tt, 8)
        dst = pl.multiple_of(b * pt, 8)
        for j in range(4):
            uslab[j, pl.ds(dst, tt), :] = ps_scr[pl.ds(src, tt), j * 128:(j + 1) * 128]
        return c
    lax.fori_loop(0, bt, put_u, 0)

    def get_u(t, c):
        for bs in range(bt // 8):
            dst = pl.multiple_of(t * bt + bs * 8, 8)
            for j in range(4):
                utb[pl.ds(dst, 8), j * 128:(j + 1) * 128] = uslab[j, pl.ds(bs * 8 * pt + t, 8, stride=pt), :]
        return c
    lax.fori_loop(0, tt, get_u, 0)

    for hf in range(2):
        uh = utb[:, hf * 256:(hf + 1) * 256].astype(BF16)
        bu_scr[:, hf * 1024:(hf + 1) * 1024] = _dot(uh, bbre_ref[hf])
        bu_scr[:, S5N + hf * 1024:S5N + (hf + 1) * 1024] = _dot(uh, bbim_ref[hf])

    for q in range(4):
        c_re = slice(q * 512, (q + 1) * 512)
        c_im = slice(S5N + q * 512, S5N + (q + 1) * 512)
        ar = jnp.broadcast_to(are_ref[:, c_re], (8, 512))
        ai = jnp.broadcast_to(aim_ref[:, c_re], (8, 512))

        def scan_rows(bs, c, c_re=c_re, c_im=c_im, ar=ar, ai=ai):
            r0 = pl.multiple_of(bs * 8, 8)

            def step(t, carry):
                sr, si = carry
                row = pl.multiple_of(t * bt + r0, 8)
                nr = ar * sr - ai * si + bu_scr[pl.ds(row, 8), c_re]
                ni = ar * si + ai * sr + bu_scr[pl.ds(row, 8), c_im]
                bu_scr[pl.ds(row, 8), c_re] = nr
                bu_scr[pl.ds(row, 8), c_im] = ni
                return nr, ni

            sr, si = lax.fori_loop(0, tt, step, (sre_ref[pl.ds(r0, 8), c_re], sim_ref[pl.ds(r0, 8), c_re]))
            sre_ref[pl.ds(r0, 8), c_re] = sr
            sim_ref[pl.ds(r0, 8), c_re] = si
            return c
        lax.fori_loop(0, bt // 8, scan_rows, 0)

    for m in range(2):
        sre = bu_scr[:, m * 1024:(m + 1) * 1024].astype(BF16)
        sim = bu_scr[:, S5N + m * 1024:S5N + (m + 1) * 1024].astype(BF16)
        ytb = _dot(sre, cre_ref[m]) - _dot(sim, cim_ref[m])
        for t in range(tt):
            for jj in range(2):
                yslab[2 * m + jj, t * pb:t * pb + bt, :] = ytb[t * bt:(t + 1) * bt, jj * 128:(jj + 1) * 128]

    def get_y(b, c):
        for ts in range(tt // 8):
            dst = pl.multiple_of(b * tt + ts * 8, 8)
            for j in range(4):
                y_scr[pl.ds(dst, 8), j * 128:(j + 1) * 128] = yslab[j, pl.ds(ts * 8 * pb + b, 8, stride=pb), :]
        return c
    lax.fori_loop(0, bt, get_y, 0)

    v = ps_scr[:, 1536:2048] * ps_scr[:, 2048:2560]
    vhalo[:, 8:8 + tt, :] = v.reshape(bt, tt, CW)
    cw = cw_ref[...]
    yc = (cw[0:1, :] * vhalo[:, 6:6 + tt, :] + cw[1:2, :] * vhalo[:, 7:7 + tt, :]
          + cw[2:3, :] * vhalo[:, 8:8 + tt, :])
    ps_scr[:, 2048:2560] = yc.reshape(r, CW)
    new_halo = vhalo[:, tt + 6:tt + 8, :]
    vhalo[:, 6:8, :] = new_halo
    cvo_ref[...] = new_halo

    rc = min(ROW_CHUNK, r)
    nb = rc // tt if rc >= tt else 0
    dsk = dsk_ref[...]
    for ci in range(r // rc):
        rows = slice(ci * rc, (ci + 1) * rc)
        u = ps_scr[rows, 0:512]
        y = y_scr[rows, :] + dsk * u
        gl = _gelu_tanh(y)
        glu = gl * _sigmoid(_dot(gl.astype(BF16), wglu_ref[...]))
        ts_ = glu * _silu(ps_scr[rows, 512:1024])
        ys = _dot(ts_.astype(BF16), wps_ref[...])
        tc_ = ps_scr[rows, 1024:1536] * ps_scr[rows, 2048:2560] * _silu(ps_scr[rows, 2560:3072])
        ycv = _dot(tc_.astype(BF16), wpc_ref[...])
        g = _dot(h_scr[rows, :], wg_ref[...])
        out = _sigmoid(g[:, :D]) * ys + _sigmoid(g[:, D:]) * ycv
        if nb:
            msc_ref[ci * nb:(ci + 1) * nb] = out.reshape(nb, tt, D)
        else:
            per = tt // rc
            msc_ref[ci // per, (ci % per) * rc:(ci % per + 1) * rc, :] = out


def _s5conv_call(layer, x, weights, state, bt, tt):
    b, t, _ = x.shape
    r = bt * tt
    has_state = state is not None
    wspec = lambda shape: pl.BlockSpec((None,) + shape, lambda bi, ti: (layer,) + (0,) * len(shape),
                                       pipeline_mode=pl.Buffered(1))
    tile = pl.BlockSpec((bt, tt, D), lambda bi, ti: (bi, ti, 0))
    in_specs = [tile, wspec((1, D)), wspec((D, 3072)), wspec((D, 2048)), wspec((1, S5N)), wspec((1, S5N)),
                wspec((2, 256, 1024)), wspec((2, 256, 1024)), wspec((2, 1024, 256)), wspec((2, 1024, 256)),
                wspec((1, S5W)), wspec((S5W, S5W)), wspec((S5W, D)), wspec((CK, CW)), wspec((CW, D))]
    args = [x] + list(weights)
    if has_state:
        in_specs += [pl.BlockSpec((None, bt, S5N), lambda bi, ti: (layer, bi, 0)),
                     pl.BlockSpec((None, bt, S5N), lambda bi, ti: (layer, bi, 0)),
                     pl.BlockSpec((None, bt, CK - 1, CW), lambda bi, ti: (layer, bi, 0, 0))]
        args += list(state)
    out_specs = [tile,
                 pl.BlockSpec((bt, S5N), lambda bi, ti: (bi, 0)),
                 pl.BlockSpec((bt, S5N), lambda bi, ti: (bi, 0)),
                 pl.BlockSpec((bt, CK - 1, CW), lambda bi, ti: (bi, 0, 0))]
    out_shape = [jax.ShapeDtypeStruct((b, t, D), F32), jax.ShapeDtypeStruct((b, S5N), F32),
                 jax.ShapeDtypeStruct((b, S5N), F32), jax.ShapeDtypeStruct((b, CK - 1, CW), F32)]
    scratch = [pltpu.VMEM((r, D), BF16),
               pltpu.VMEM((r, 3072), F32),
               pltpu.VMEM((4, bt * _pitch(tt), 128), F32),
               pltpu.VMEM((r, S5W), F32),
               pltpu.VMEM((r, 2 * S5N), F32),
               pltpu.VMEM((4, tt * _pitch(bt), 128), F32),
               pltpu.VMEM((r, S5W), F32),
               pltpu.VMEM((bt, tt + 8, CW), F32)]
    return pl.pallas_call(
        functools.partial(_s5conv_kernel, has_state, bt, tt),
        grid=(b // bt, t // tt),
        in_specs=in_specs, out_specs=out_specs, out_shape=out_shape, scratch_shapes=scratch,
        compiler_params=pltpu.CompilerParams(dimension_semantics=("arbitrary", "arbitrary"),
                                             vmem_limit_bytes=VMEM_LIMIT),
        name="s5conv_state" if has_state else "s5conv",
    )(*args)


def _seg_scan(x, pos, seg, op, fill):
    s = 1
    while s < seg:
        x = op(x, jnp.where(pos >= s, pltpu.roll(x, s, 1), fill))
        s *= 2
    return x


def _seg_bcast_last(x, pos, seg):
    n = x.shape[1]
    s = 1
    while s < seg:
        x = jnp.where(pos + s <= seg - 1, pltpu.roll(x, n - s, 1), x)
        s *= 2
    return x


def _mlstm_kernel(has_state, final, bt, tt, *refs):
    refs = list(refs)
    (x_ref, msc_ref, nw_ref, wm_ref, wif_ref, bias_ref, wgm_ref, hnw_ref, wpm_ref, wout_ref, fnw_ref) = refs[:11]
    refs = refs[11:]
    if has_state:
        c0_ref, n0_ref, m0_ref = refs[:3]
        refs = refs[3:]
    y_ref, c_ref, n_ref, m_ref = refs[:4]
    h_scr, p_scr, hm_scr = refs[4:]

    r = bt * tt
    rp = max(r, 128)
    ti = pl.program_id(1)

    @pl.when(ti == 0)
    def _init():
        if has_state:
            c_ref[...] = c0_ref[...]
            n_ref[...] = n0_ref[...]
            m_ref[...] = m0_ref[...]
        else:
            c_ref[...] = jnp.zeros_like(c_ref)
            n_ref[...] = jnp.zeros_like(n_ref)
            m_ref[...] = jnp.zeros_like(m_ref)

    x = x_ref[...].reshape(r, D)
    h = _rmsnorm(x, nw_ref[...]).astype(BF16)
    h_scr[...] = h
    p_scr[...] = _dot(h, wm_ref[...])

    hp = h if rp == r else jnp.concatenate([h, jnp.zeros((rp - r, D), BF16)], axis=0)
    gt = _dot_nt(wif_ref[...], hp) + bias_ref[...]
    lane = lax.broadcasted_iota(jnp.int32, (8, rp), 1)
    pos = jnp.bitwise_and(lane, tt - 1)
    li = gt[0:8, :]
    lf = _log_sigmoid(gt[8:16, :])
    bcum = _seg_scan(lf, pos, tt, jnp.add, 0.0)
    g = li - bcum
    mprev = m_ref[...]
    mrun = jnp.maximum(mprev, _seg_scan(g, pos, tt, jnp.maximum, NEG))
    mlast = _seg_bcast_last(mrun, pos, tt)
    winter = jnp.exp(mprev - mrun)
    efloor = jnp.exp(-(bcum + mrun))
    wrow = jnp.exp(g - mlast)
    decay = jnp.exp(mprev - mlast)
    m_ref[...] = _seg_bcast_last(bcum, pos, tt) + mlast
    zt = jnp.concatenate([mrun, winter, efloor, wrow], axis=0).T

    row_i = lax.broadcasted_iota(jnp.int32, (tt, tt), 0)
    col_i = lax.broadcasted_iota(jnp.int32, (tt, tt), 1)
    causal = row_i >= col_i

    for b in range(bt):
        rows = slice(b * tt, (b + 1) * tt)
        for hd in range(NH):
            cs = slice(hd * DH, (hd + 1) * DH)
            qf = p_scr[rows, cs]
            kf = p_scr[rows, D + hd * DH:D + (hd + 1) * DH] * K_SCALE
            vb = p_scr[rows, 2 * D + hd * DH:2 * D + (hd + 1) * DH].astype(BF16)
            qb = qf.astype(BF16)
            kb = kf.astype(BF16)
            mcol = zt[rows, hd:hd + 1]
            wi = zt[rows, 8 + hd:9 + hd]
            ef = zt[rows, 16 + hd:17 + hd]
            wc = zt[rows, 24 + hd:25 + hd]
            grow = g[hd:hd + 1, b * tt:(b + 1) * tt]
            dm = jnp.exp(jnp.where(causal, grow - mcol, NEG))
            sc = _dot_nt(qb, kb) * dm
            cf = c_ref[b, hd]
            nrow = n_ref[b, hd:hd + 1, :]
            num = wi * _dot(qb, cf.astype(BF16)) + _dot(sc.astype(BF16), vb)
            den = wi * jnp.sum(qf * nrow, axis=-1, keepdims=True) + jnp.sum(sc, axis=-1, keepdims=True)
            hm_scr[rows, cs] = num * (1.0 / jnp.maximum(jnp.abs(den), ef))
            dec = decay[hd:hd + 1, b * tt:b * tt + 1]
            wk = kf * wc
            c_ref[b, hd] = dec * cf + _dot_tn(wk.astype(BF16), vb)
            n_ref[b, hd:hd + 1, :] = dec * nrow + jnp.sum(wk, axis=0, keepdims=True)

    rc = min(ROW_CHUNK, r)
    hnw = hnw_ref[...]
    for ci in range(r // rc):
        rows = slice(ci * rc, (ci + 1) * rc)
        hmv = hm_scr[rows, :] * _sigmoid(p_scr[rows, 3 * D:4 * D])
        parts = []
        for hd in range(NH):
            hh = hmv[:, hd * DH:(hd + 1) * DH]
            mu = jnp.mean(hh, axis=-1, keepdims=True)
            dv = hh - mu
            var = jnp.mean(dv * dv, axis=-1, keepdims=True)
            parts.append(dv * lax.rsqrt(var + EPS) * hnw[:, hd * DH:(hd + 1) * DH])
        hn = jnp.concatenate(parts, axis=1)
        tm = hn * _silu(p_scr[rows, 4 * D:5 * D])
        ym = _dot(tm.astype(BF16), wpm_ref[...])
        gm = _dot(h_scr[rows, :], wgm_ref[...])
        nb = rc // tt if rc >= tt else 0
        if nb:
            msc = msc_ref[ci * nb:(ci + 1) * nb].reshape(rc, D)
            xr = x_ref[ci * nb:(ci + 1) * nb].reshape(rc, D)
        else:
            per = tt // rc
            msc = msc_ref[ci // per, (ci % per) * rc:(ci % per + 1) * rc, :]
            xr = x_ref[ci // per, (ci % per) * rc:(ci % per + 1) * rc, :]
        merged = msc + _sigmoid(gm) * ym
        y = xr + _dot(merged.astype(BF16), wout_ref[...])
        if final:
            y = _rmsnorm(y, fnw_ref[...])
        if nb:
            y_ref[ci * nb:(ci + 1) * nb] = y.reshape(nb, tt, D)
        else:
            y_ref[ci // per, (ci % per) * rc:(ci % per + 1) * rc, :] = y


def _mlstm_call(layer, final, x, msc, weights, state, bt, tt):
    b, t, _ = x.shape
    r = bt * tt
    rp = max(r, 128)
    nbt = b // bt
    has_state = state is not None
    wspec = lambda shape: pl.BlockSpec((None,) + shape, lambda bi, ti: (layer,) + (0,) * len(shape),
                                       pipeline_mode=pl.Buffered(1))
    tile = pl.BlockSpec((bt, tt, D), lambda bi, ti: (bi, ti, 0))
    in_specs = [tile, tile, wspec((1, D)), wspec((D, 5 * D)), wspec((16, D)), wspec((16, 1)), wspec((D, D)),
                wspec((1, D)), wspec((D, D)), wspec((D, D)),
                pl.BlockSpec((1, D), lambda bi, ti: (0, 0), pipeline_mode=pl.Buffered(1))]
    args = [x, msc] + list(weights)
    if has_state:
        in_specs += [pl.BlockSpec((None, bt, NH, DH, DH), lambda bi, ti: (layer, bi, 0, 0, 0)),
                     pl.BlockSpec((None, bt, NH, DH), lambda bi, ti: (layer, bi, 0, 0)),
                     pl.BlockSpec((None, None, 8, rp), lambda bi, ti: (layer, bi, 0, 0))]
        args += list(state)
    out_specs = [tile,
                 pl.BlockSpec((bt, NH, DH, DH), lambda bi, ti: (bi, 0, 0, 0)),
                 pl.BlockSpec((bt, NH, DH), lambda bi, ti: (bi, 0, 0)),
                 pl.BlockSpec((None, 8, rp), lambda bi, ti: (bi, 0, 0))]
    out_shape = [jax.ShapeDtypeStruct((b, t, D), F32), jax.ShapeDtypeStruct((b, NH, DH, DH), F32),
                 jax.ShapeDtypeStruct((b, NH, DH), F32), jax.ShapeDtypeStruct((nbt, 8, rp), F32)]
    scratch = [pltpu.VMEM((r, D), BF16),
               pltpu.VMEM((r, 5 * D), F32),
               pltpu.VMEM((r, D), F32)]
    return pl.pallas_call(
        functools.partial(_mlstm_kernel, has_state, final, bt, tt),
        grid=(nbt, t // tt),
        in_specs=in_specs, out_specs=out_specs, out_shape=out_shape, scratch_shapes=scratch,
        compiler_params=pltpu.CompilerParams(dimension_semantics=("arbitrary", "arbitrary"),
                                             vmem_limit_bytes=VMEM_LIMIT),
        name="mlstm_state" if has_state else "mlstm",
    )(*args)


def _expand_m(m, bt, tt):
    depth, b, _ = m.shape
    r = bt * tt
    rp = max(r, 128)
    v = jnp.repeat(m.reshape(depth, b // bt, bt, NH).transpose(0, 1, 3, 2), tt, axis=-1)
    return jnp.pad(v, ((0, 0), (0, 0), (0, 8 - NH), (0, rp - r)))


def _collapse_m(mrows, bt, tt):
    nbt = mrows.shape[0]
    return mrows[:, :NH, 0:bt * tt:tt].transpose(0, 2, 1).reshape(nbt * bt, NH)


def kernel(x_prompt, x_sample, state_ssm_re, state_ssm_im, state_conv, state_mlstm_c, state_mlstm_n, state_mlstm_m, norm_w, w_in, i_bias, f_bias, lam_re, lam_im, log_dt, b_re, b_im, c_re, c_im, d_skip, w_glu, w_proj_s, conv_w, w_proj_c, mlstm_norm_w, w_proj_m, w_out, final_norm_w):
    depth = norm_w.shape[0]
    bp, tp, _ = x_prompt.shape
    bs, ts, _ = x_sample.shape
    assert tp % CHUNK == 0 and ts <= CHUNK and ts % 8 == 0 and bp % 8 == 0 and bs % 64 == 0

    a_re, a_im, bb_re, bb_im = _s5_discretise(lam_re, lam_im, log_dt, b_re, b_im)
    eye = jnp.eye(16, dtype=F32)

    def pack_c(c):
        c = c.reshape(depth, 2, 16, S5C, S5P)
        return jnp.einsum('lmgcp,gk->lmgpkc', c, eye).reshape(depth, 2, 1024, 256).astype(BF16)

    row = lambda a: a.reshape(depth, 1, a.shape[-1])
    nw = row(norm_w)
    w_s = w_in[:, :, OFF_S:OFF_M].astype(BF16)
    w_m = w_in[:, :, OFF_M:OFF_IF].astype(BF16)
    gate_rows = lambda a: jnp.pad(a.reshape(depth, 2, NH, -1), ((0, 0), (0, 0), (0, 8 - NH), (0, 0))).reshape(depth, 16, -1)
    w_if = gate_rows(jnp.swapaxes(w_in[:, :, OFF_IF:OFF_G], 1, 2)).astype(BF16)
    w_gsc = w_in[:, :, OFF_G:OFF_G + 2 * D].astype(BF16)
    w_gm = w_in[:, :, OFF_G + 2 * D:IN_W].astype(BF16)
    bias = gate_rows(jnp.concatenate([i_bias, f_bias], axis=-1)[:, :, None])
    wa = [nw, w_s, w_gsc, a_re, a_im, bb_re, bb_im, pack_c(c_re), pack_c(c_im), row(d_skip),
          w_glu.astype(BF16), w_proj_s.astype(BF16), conv_w, w_proj_c.astype(BF16)]
    wb = [nw, w_m, w_if, bias, w_gm, row(mlstm_norm_w), w_proj_m.astype(BF16), w_out.astype(BF16),
          final_norm_w.reshape(1, D)]

    sb_a, sb_b = 64, 4
    pb_a, pb_b = 8, 4
    st_a = (state_ssm_re.reshape(depth, bs, S5N), state_ssm_im.reshape(depth, bs, S5N), state_conv)
    st_b = (state_mlstm_c, state_mlstm_n, _expand_m(state_mlstm_m, sb_b, ts))

    yp, ys = x_prompt, x_sample
    outs_p = [[] for _ in range(6)]
    outs_s = [[] for _ in range(6)]
    for l in range(depth):
        final = l == depth - 1
        msc, sre, sim, cv = _s5conv_call(l, yp, wa, None, pb_a, CHUNK)
        yp, cc, nn, mm = _mlstm_call(l, final, yp, msc, wb, None, pb_b, CHUNK)
        for i, v in enumerate((sre.reshape(bp, S5G, S5P), sim.reshape(bp, S5G, S5P), cv, cc, nn,
                               _collapse_m(mm, pb_b, CHUNK))):
            outs_p[i].append(v)
        msc, sre, sim, cv = _s5conv_call(l, ys, wa, st_a, sb_a, ts)
        ys, cc, nn, mm = _mlstm_call(l, final, ys, msc, wb, st_b, sb_b, ts)
        for i, v in enumerate((sre.reshape(bs, S5G, S5P), sim.reshape(bs, S5G, S5P), cv, cc, nn,
                               _collapse_m(mm, sb_b, ts))):
            outs_s[i].append(v)
    sp = [jnp.stack(o) for o in outs_p]
    ss = [jnp.stack(o) for o in outs_s]
    return (yp, ys, sp[0], ss[0], sp[1], ss[1], sp[2], ss[2], sp[3], ss[3], sp[4], ss[4], sp[5], ss[5])
```

```python
import functools
import math

import jax
import jax.numpy as jnp
from jax import lax
from jax.experimental import pallas as pl
from jax.experimental.pallas import tpu as pltpu

D = 1024
S5W = 512
S5G = 32
S5P = 64
S5C = 16
S5N = S5G * S5P
CW = 512
CK = 3
NH = 4
DH = 256
PROMPT_CHUNK = 256
S5_TILE_T = 64
EPS = 1e-6
NEG = -1e30
K_SCALE = DH ** -0.5

OFF_S = 0
OFF_M = 3072
OFF_IF = 8192
OFF_G = 8200
IN_W = 11272

F32 = jnp.float32
BF16 = jnp.bfloat16
VMEM_LIMIT = 58 * 1024 * 1024
ROW_CHUNK = 128
PROJ_ROWS = 256


def _sigmoid(x):
    return 1.0 / (1.0 + jnp.exp(-x))


def _silu(x):
    return x * _sigmoid(x)


def _gelu_tanh(x):
    return 0.5 * x * (1.0 + jnp.tanh(math.sqrt(2.0 / math.pi) * (x + 0.044715 * (x * x * x))))


def _log_sigmoid(x):
    return jnp.minimum(x, 0.0) - jnp.log1p(jnp.exp(-jnp.abs(x)))


def _rmsnorm(x, w):
    return x * lax.rsqrt(jnp.mean(x * x, axis=-1, keepdims=True) + EPS) * w


def _dot(a, b):
    return jnp.dot(a, b, preferred_element_type=F32)


def _dot_nt(a, b):
    return lax.dot_general(a, b, (((1,), (1,)), ((), ())), preferred_element_type=F32)


def _dot_tn(a, b):
    return lax.dot_general(a, b, (((0,), (0,)), ((), ())), preferred_element_type=F32)


def _pitch(n):
    p = n + 8
    return p if (p // 8) % 2 == 1 else p + 8


def _s5_disc_kernel(lre_ref, lim_ref, ldt_ref, bre_ref, bim_ref, are_ref, aim_ref, bbre_ref, bbim_ref):
    lr = lre_ref[...]
    li = lim_ref[...]
    dt = jnp.exp(ldt_ref[...])
    ea = jnp.exp(lr * dt)
    ar = ea * jnp.cos(li * dt)
    ai = ea * jnp.sin(li * dt)
    are_ref[...] = ar
    aim_ref[...] = ai
    nr = ar - 1.0
    inv = 1.0 / (lr * lr + li * li)
    cr = (nr * lr + ai * li) * inv
    ci = (ai * lr - nr * li) * inv
    half = S5N // 2
    for h in range(2):
        crh = cr[:, h * half:(h + 1) * half]
        cih = ci[:, h * half:(h + 1) * half]
        br = bre_ref[h]
        bi = bim_ref[h]
        bbre_ref[h] = (crh * br - cih * bi).astype(BF16)
        bbim_ref[h] = (crh * bi + cih * br).astype(BF16)


def _s5_discretise(lam_re, lam_im, log_dt, b_re, b_im):
    depth = lam_re.shape[0]
    eye = jnp.eye(16, dtype=F32)

    def pack_b(b):
        b = b.reshape(depth, 2, 16, S5P, S5C)
        return jnp.einsum('lhgpc,gk->lhgckp', b, eye).reshape(depth, 2, 256, 1024)

    row = lambda a: a.reshape(depth, 1, S5N)
    ldt = jnp.broadcast_to(log_dt[:, :, None], (depth, S5G, S5P))
    vec = pl.BlockSpec((None, 1, S5N), lambda l: (l, 0, 0))
    mat = pl.BlockSpec((None, 2, 256, 1024), lambda l: (l, 0, 0, 0))
    return pl.pallas_call(
        _s5_disc_kernel,
        grid=(depth,),
        in_specs=[vec, vec, vec, mat, mat],
        out_specs=[vec, vec, mat, mat],
        out_shape=[jax.ShapeDtypeStruct((depth, 1, S5N), F32), jax.ShapeDtypeStruct((depth, 1, S5N), F32),
                   jax.ShapeDtypeStruct((depth, 2, 256, 1024), BF16),
                   jax.ShapeDtypeStruct((depth, 2, 256, 1024), BF16)],
        name="s5_discretise",
    )(row(lam_re), row(lam_im), row(ldt), pack_b(b_re), pack_b(b_im))


def _s5conv_kernel(has_state, bt, tt, *refs):
    refs = list(refs)
    (x_ref, nw_ref, ws_ref, wg_ref, are_ref, aim_ref, bbre_ref, bbim_ref, cre_ref, cim_ref,
     dsk_ref, wglu_ref, wps_ref, cw_ref, wpc_ref) = refs[:15]
    refs = refs[15:]
    if has_state:
        s0re_ref, s0im_ref, cv0_ref = refs[:3]
        refs = refs[3:]
    msc_ref, sre_ref, sim_ref, cvo_ref = refs[:4]
    h_scr, ps_scr, uslab, utb, bu_scr, yslab, y_scr, vhalo = refs[4:]

    r = bt * tt
    pt = _pitch(tt)
    pb = _pitch(bt)
    ti = pl.program_id(1)

    @pl.when(ti == 0)
    def _init():
        if has_state:
            sre_ref[...] = s0re_ref[...]
            sim_ref[...] = s0im_ref[...]
            vhalo[:, 6:8, :] = cv0_ref[...]
        else:
            sre_ref[...] = jnp.zeros_like(sre_ref)
            sim_ref[...] = jnp.zeros_like(sim_ref)
            vhalo[:, 6:8, :] = jnp.zeros((bt, 2, CW), F32)

    x = x_ref[...].reshape(r, D)
    h = _rmsnorm(x, nw_ref[...]).astype(BF16)
    h_scr[...] = h
    ps_scr[...] = _dot(h, ws_ref[...])


    for b in range(bt):
        for j in range(4):
            uslab[j, b * pt:b * pt + tt, :] = ps_scr[b * tt:(b + 1) * tt, j * 128:(j + 1) * 128]
    for t in range(tt):
        for bs in range(bt // 8):
            dst = t * bt + bs * 8
            for j in range(4):
                utb[dst:dst + 8, j * 128:(j + 1) * 128] = uslab[j, pl.ds(bs * 8 * pt + t, 8, stride=pt), :]

    for hf in range(2):
        uh = utb[:, hf * 256:(hf + 1) * 256].astype(BF16)
        bu_scr[:, hf * 1024:(hf + 1) * 1024] = _dot(uh, bbre_ref[hf])
        bu_scr[:, S5N + hf * 1024:S5N + (hf + 1) * 1024] = _dot(uh, bbim_ref[hf])

    v = ps_scr[:, 1536:2048] * ps_scr[:, 2048:2560]
    vhalo[:, 8:8 + tt, :] = v.reshape(bt, tt, CW)
    cw = cw_ref[...]
    yc = (cw[0:1, :] * vhalo[:, 6:6 + tt, :] + cw[1:2, :] * vhalo[:, 7:7 + tt, :]
          + cw[2:3, :] * vhalo[:, 8:8 + tt, :])
    ps_scr[:, 2048:2560] = yc.reshape(r, CW)
    new_halo = vhalo[:, tt + 6:tt + 8, :]
    vhalo[:, 6:8, :] = new_halo
    cvo_ref[...] = new_halo

    rc = min(ROW_CHUNK, r)
    for ci in range(r // rc):
        rows = slice(ci * rc, (ci + 1) * rc)
        tc_ = ps_scr[rows, 1024:1536] * ps_scr[rows, 2048:2560] * _silu(ps_scr[rows, 2560:3072])
        ycv = _dot(tc_.astype(BF16), wpc_ref[...])
        g = _dot(h_scr[rows, :], wg_ref[...])
        ps_scr[rows, 1024:2048] = _sigmoid(g[:, D:]) * ycv
        ps_scr[rows, 2048:3072] = _sigmoid(g[:, :D])

    for q in range(4):
        c_re = slice(q * 512, (q + 1) * 512)
        c_im = slice(S5N + q * 512, S5N + (q + 1) * 512)
        ar = jnp.broadcast_to(are_ref[:, c_re], (8, 512))
        ai = jnp.broadcast_to(aim_ref[:, c_re], (8, 512))
        for bs in range(bt // 8):
            r0 = bs * 8
            sr = sre_ref[r0:r0 + 8, c_re]
            si = sim_ref[r0:r0 + 8, c_re]
            for t in range(tt):
                row = t * bt + r0
                nr = ar * sr - ai * si + bu_scr[row:row + 8, c_re]
                ni = ar * si + ai * sr + bu_scr[row:row + 8, c_im]
                bu_scr[row:row + 8, c_re] = nr
                bu_scr[row:row + 8, c_im] = ni
                sr, si = nr, ni
            sre_ref[r0:r0 + 8, c_re] = sr
            sim_ref[r0:r0 + 8, c_re] = si

    for m in range(2):
        sre = bu_scr[:, m * 1024:(m + 1) * 1024].astype(BF16)
        sim = bu_scr[:, S5N + m * 1024:S5N + (m + 1) * 1024].astype(BF16)
        ytb = _dot(sre, cre_ref[m]) - _dot(sim, cim_ref[m])
        for t in range(tt):
            for jj in range(2):
                yslab[2 * m + jj, t * pb:t * pb + bt, :] = ytb[t * bt:(t + 1) * bt, jj * 128:(jj + 1) * 128]
    for b in range(bt):
        for ts in range(tt // 8):
            dst = b * tt + ts * 8
            for j in range(4):
                y_scr[dst:dst + 8, j * 128:(j + 1) * 128] = yslab[j, pl.ds(ts * 8 * pb + b, 8, stride=pb), :]

    nb = rc // tt if rc >= tt else 0
    dsk = dsk_ref[...]
    for ci in range(r // rc):
        rows = slice(ci * rc, (ci + 1) * rc)
        y = y_scr[rows, :] + dsk * ps_scr[rows, 0:512]
        gl = _gelu_tanh(y)
        glu = gl * _sigmoid(_dot(gl.astype(BF16), wglu_ref[...]))
        ts_ = glu * _silu(ps_scr[rows, 512:1024])
        ys = _dot(ts_.astype(BF16), wps_ref[...])
        out = ps_scr[rows, 2048:3072] * ys + ps_scr[rows, 1024:2048]
        if nb:
            msc_ref[ci * nb:(ci + 1) * nb] = out.reshape(nb, tt, D)
        else:
            per = tt // rc
            msc_ref[ci // per, (ci % per) * rc:(ci % per + 1) * rc, :] = out


def _s5conv_call(layer, x, weights, state, bt, tt):
    b, t, _ = x.shape
    r = bt * tt
    has_state = state is not None
    wspec = lambda shape: pl.BlockSpec((None,) + shape, lambda bi, ti: (layer,) + (0,) * len(shape),
                                       pipeline_mode=pl.Buffered(1))
    tile = pl.BlockSpec((bt, tt, D), lambda bi, ti: (bi, ti, 0))
    in_specs = [tile, wspec((1, D)), wspec((D, 3072)), wspec((D, 2048)), wspec((1, S5N)), wspec((1, S5N)),
                wspec((2, 256, 1024)), wspec((2, 256, 1024)), wspec((2, 1024, 256)), wspec((2, 1024, 256)),
                wspec((1, S5W)), wspec((S5W, S5W)), wspec((S5W, D)), wspec((CK, CW)), wspec((CW, D))]
    args = [x] + list(weights)
    if has_state:
        in_specs += [pl.BlockSpec((None, bt, S5N), lambda bi, ti: (layer, bi, 0)),
                     pl.BlockSpec((None, bt, S5N), lambda bi, ti: (layer, bi, 0)),
                     pl.BlockSpec((None, bt, CK - 1, CW), lambda bi, ti: (layer, bi, 0, 0))]
        args += list(state)
    out_specs = [tile,
                 pl.BlockSpec((bt, S5N), lambda bi, ti: (bi, 0)),
                 pl.BlockSpec((bt, S5N), lambda bi, ti: (bi, 0)),
                 pl.BlockSpec((bt, CK - 1, CW), lambda bi, ti: (bi, 0, 0))]
    out_shape = [jax.ShapeDtypeStruct((b, t, D), F32), jax.ShapeDtypeStruct((b, S5N), F32),
                 jax.ShapeDtypeStruct((b, S5N), F32), jax.ShapeDtypeStruct((b, CK - 1, CW), F32)]
    scratch = [pltpu.VMEM((r, D), BF16),
               pltpu.VMEM((r, 3072), F32),
               pltpu.VMEM((4, bt * _pitch(tt), 128), F32),
               pltpu.VMEM((r, S5W), F32),
               pltpu.VMEM((r, 2 * S5N), F32),
               pltpu.VMEM((4, tt * _pitch(bt), 128), F32),
               pltpu.VMEM((r, S5W), F32),
               pltpu.VMEM((bt, tt + 8, CW), F32)]
    return pl.pallas_call(
        functools.partial(_s5conv_kernel, has_state, bt, tt),
        grid=(b // bt, t // tt),
        in_specs=in_specs, out_specs=out_specs, out_shape=out_shape, scratch_shapes=scratch,
        compiler_params=pltpu.CompilerParams(dimension_semantics=("arbitrary", "arbitrary"),
                                             vmem_limit_bytes=VMEM_LIMIT),
        name="s5conv_state" if has_state else "s5conv",
    )(*args)


def _seg_scan(x, pos, seg, op, fill):
    s = 1
    while s < seg:
        x = op(x, jnp.where(pos >= s, pltpu.roll(x, s, 1), fill))
        s *= 2
    return x


def _seg_bcast_last(x, pos, seg):
    n = x.shape[1]
    s = 1
    while s < seg:
        x = jnp.where(pos + s <= seg - 1, pltpu.roll(x, n - s, 1), x)
        s *= 2
    return x


def _mlstm_kernel(has_state, has_alias, final, single_tile, bt, tt, nsub, *refs):
    refs = list(refs)
    (x_ref, msc_ref, nw_ref, wm_ref, wif_ref, bias_ref, wgm_ref, hnw_ref, wpm_ref, wout_ref, fnw_ref) = refs[:11]
    refs = refs[11:]
    if has_state:
        c0_ref, n0_ref, m0_ref = refs[:3]
        refs = refs[3:]
    if has_alias:
        refs = refs[1:]
    y_ref, c_ref, n_ref, m_ref = refs[:4]
    h_scr, p_scr, hm_scr = refs[4:]

    r = bt * tt
    ro = r * nsub
    rp = max(r, 128)
    step = pl.program_id(1)

    def init_state():
        if has_state:
            c_ref[...] = c0_ref[...]
            n_ref[...] = n0_ref[...]
            m_ref[...] = m0_ref[...]
        else:
            c_ref[...] = jnp.zeros_like(c_ref)
            n_ref[...] = jnp.zeros_like(n_ref)
            m_ref[...] = jnp.zeros_like(m_ref)

    def normalise():
        h_scr[...] = _rmsnorm(x_ref[...].reshape(ro, D), nw_ref[...]).astype(BF16)

    def project():
        p_scr[...] = _dot(h_scr[...], wm_ref[...])

    if single_tile and has_state:
        c_prev, n_prev, m_prev = c0_ref, n0_ref, m0_ref
    else:
        pl.when(step == 0)(init_state) if not single_tile else init_state()
        c_prev, n_prev, m_prev = c_ref, n_ref, m_ref

    if nsub == 1:
        base = 0
        normalise()
        h = h_scr[...]
    else:
        base = pl.multiple_of(step * r, r)
        pl.when(step == 0)(normalise)
        h = h_scr[pl.ds(base, r), :]

    hp = h if rp == r else jnp.concatenate([h, jnp.zeros((rp - r, D), BF16)], axis=0)
    gt = _dot_nt(wif_ref[...], hp) + bias_ref[...]
    if nsub == 1:
        project()
    else:
        pl.when(step == 0)(project)
    lane = lax.broadcasted_iota(jnp.int32, (8, rp), 1)
    pos = jnp.bitwise_and(lane, tt - 1)
    li = gt[0:8, :]
    lf = _log_sigmoid(gt[8:16, :])
    bcum = _seg_scan(lf, pos, tt, jnp.add, 0.0)
    g = li - bcum
    mprev = m_prev[...]
    mrun = jnp.maximum(mprev, _seg_scan(g, pos, tt, jnp.maximum, NEG))
    mlast = _seg_bcast_last(mrun, pos, tt)
    winter = jnp.exp(mprev - mrun)
    efloor = jnp.exp(-(bcum + mrun))
    wrow = jnp.exp(g - mlast)
    decay = jnp.exp(mprev - mlast)
    m_ref[...] = _seg_bcast_last(bcum, pos, tt) + mlast
    zt = jnp.concatenate([mrun, winter, efloor, wrow], axis=0).T

    row_i = lax.broadcasted_iota(jnp.int32, (tt, tt), 0)
    col_i = lax.broadcasted_iota(jnp.int32, (tt, tt), 1)
    causal = row_i >= col_i

    for b in range(bt):
        rows = slice(b * tt, (b + 1) * tt)
        prow = rows if nsub == 1 else pl.ds(base + b * tt, tt)
        for hd in range(NH):
            cs = slice(hd * DH, (hd + 1) * DH)
            qf = p_scr[prow, cs]
            kf = p_scr[prow, D + hd * DH:D + (hd + 1) * DH] * K_SCALE
            vb = p_scr[prow, 2 * D + hd * DH:2 * D + (hd + 1) * DH].astype(BF16)
            qb = qf.astype(BF16)
            kb = kf.astype(BF16)
            mcol = zt[rows, hd:hd + 1]
            wi = zt[rows, 8 + hd:9 + hd]
            ef = zt[rows, 16 + hd:17 + hd]
            wc = zt[rows, 24 + hd:25 + hd]
            grow = g[hd:hd + 1, b * tt:(b + 1) * tt]
            dm = jnp.exp(jnp.where(causal, grow - mcol, NEG))
            sc = _dot_nt(qb, kb) * dm
            cf = c_prev[b, hd]
            nrow = n_prev[b, hd:hd + 1, :]
            num = wi * _dot(qb, cf.astype(BF16)) + _dot(sc.astype(BF16), vb)
            den = wi * jnp.sum(qf * nrow, axis=-1, keepdims=True) + jnp.sum(sc, axis=-1, keepdims=True)
            hm_scr[prow, cs] = num * (1.0 / jnp.maximum(jnp.abs(den), ef))
            dec = decay[hd:hd + 1, b * tt:b * tt + 1]
            wk = kf * wc
            c_ref[b, hd] = dec * cf + _dot_tn(wk.astype(BF16), vb)
            n_ref[b, hd:hd + 1, :] = dec * nrow + jnp.sum(wk, axis=0, keepdims=True)

    def epilogue():
        rc = min(ROW_CHUNK, ro)
        hnw = hnw_ref[...]
        for ci in range(ro // rc):
            rows = slice(ci * rc, (ci + 1) * rc)
            hmv = hm_scr[rows, :] * _sigmoid(p_scr[rows, 3 * D:4 * D])
            parts = []
            for hd in range(NH):
                hh = hmv[:, hd * DH:(hd + 1) * DH]
                mu = jnp.mean(hh, axis=-1, keepdims=True)
                dv = hh - mu
                var = jnp.mean(dv * dv, axis=-1, keepdims=True)
                parts.append(dv * lax.rsqrt(var + EPS) * hnw[:, hd * DH:(hd + 1) * DH])
            hn = jnp.concatenate(parts, axis=1)
            tm = hn * _silu(p_scr[rows, 4 * D:5 * D])
            ym = _dot(tm.astype(BF16), wpm_ref[...])
            gm = _dot(h_scr[rows, :], wgm_ref[...])
            nb = rc // tt if rc >= tt else 0
            if nb:
                msc = msc_ref[ci * nb:(ci + 1) * nb].reshape(rc, D)
                xr = x_ref[ci * nb:(ci + 1) * nb].reshape(rc, D)
            else:
                per = tt // rc
                msc = msc_ref[ci // per, (ci % per) * rc:(ci % per + 1) * rc, :]
                xr = x_ref[ci // per, (ci % per) * rc:(ci % per + 1) * rc, :]
            merged = msc + _sigmoid(gm) * ym
            y = xr + _dot(merged.astype(BF16), wout_ref[...])
            if final:
                y = _rmsnorm(y, fnw_ref[...])
            if nb:
                y_ref[ci * nb:(ci + 1) * nb] = y.reshape(nb, tt, D)
            else:
                y_ref[ci // per, (ci % per) * rc:(ci % per + 1) * rc, :] = y

    if nsub == 1:
        epilogue()
    else:
        pl.when(step == nsub - 1)(epilogue)


def _mlstm_call(layer, depth, x, msc, weights, state, c_stack, bt, tt):
    b, t, _ = x.shape
    r = bt * tt
    rp = max(r, 128)
    nbt = b // bt
    has_state = state is not None
    has_alias = c_stack is not None
    final = layer == depth - 1
    nsub = max(1, min(nbt, PROJ_ROWS // r)) if t == tt else 1
    assert nbt % nsub == 0
    ro = r * nsub
    if nsub == 1:
        grid = (nbt, t // tt)
        tile_idx = lambda bi, si: (bi, si, 0)
        sub_idx = lambda bi, si: bi
    else:
        grid = (nbt // nsub, nsub)
        tile_idx = lambda bi, si: (bi, 0, 0)
        sub_idx = lambda bi, si: bi * nsub + si
    wspec = lambda shape: pl.BlockSpec((None,) + shape, lambda bi, si: (layer,) + (0,) * len(shape),
                                       pipeline_mode=pl.Buffered(1))
    tile = pl.BlockSpec((bt * nsub, tt, D), tile_idx)
    in_specs = [tile, tile, wspec((1, D)), wspec((D, 5 * D)), wspec((16, D)), wspec((16, 1)), wspec((D, D)),
                wspec((1, D)), wspec((D, D)), wspec((D, D)),
                pl.BlockSpec((1, D), lambda bi, si: (0, 0), pipeline_mode=pl.Buffered(1))]
    args = [x, msc] + list(weights)
    if has_state:
        in_specs += [pl.BlockSpec((None, bt, NH, DH, DH), lambda bi, si: (layer, sub_idx(bi, si), 0, 0, 0)),
                     pl.BlockSpec((None, bt, NH, DH), lambda bi, si: (layer, sub_idx(bi, si), 0, 0)),
                     pl.BlockSpec((None, None, 8, rp), lambda bi, si: (layer, sub_idx(bi, si), 0, 0))]
        args += list(state)
    aliases = {}
    if has_alias:
        aliases = {len(args): 1}
        in_specs.append(pl.BlockSpec(memory_space=pl.ANY))
        args.append(c_stack)
    out_specs = [tile,
                 pl.BlockSpec((None, bt, NH, DH, DH), lambda bi, si: (layer, sub_idx(bi, si), 0, 0, 0)),
                 pl.BlockSpec((bt, NH, DH), lambda bi, si: (sub_idx(bi, si), 0, 0)),
                 pl.BlockSpec((None, 8, rp), lambda bi, si: (sub_idx(bi, si), 0, 0))]
    out_shape = [jax.ShapeDtypeStruct((b, t, D), F32), jax.ShapeDtypeStruct((depth, b, NH, DH, DH), F32),
                 jax.ShapeDtypeStruct((b, NH, DH), F32), jax.ShapeDtypeStruct((nbt, 8, rp), F32)]
    scratch = [pltpu.VMEM((ro, D), BF16),
               pltpu.VMEM((ro, 5 * D), F32),
               pltpu.VMEM((ro, D), F32)]
    return pl.pallas_call(
        functools.partial(_mlstm_kernel, has_state, has_alias, final, t == tt, bt, tt, nsub),
        grid=grid,
        in_specs=in_specs, out_specs=out_specs, out_shape=out_shape, scratch_shapes=scratch,
        input_output_aliases=aliases,
        compiler_params=pltpu.CompilerParams(dimension_semantics=("arbitrary", "arbitrary"),
                                             vmem_limit_bytes=VMEM_LIMIT),
        name="mlstm_state" if has_state else "mlstm",
    )(*args)


def _expand_m(m, bt, tt):
    depth, b, _ = m.shape
    r = bt * tt
    rp = max(r, 128)
    v = jnp.repeat(m.reshape(depth, b // bt, bt, NH).transpose(0, 1, 3, 2), tt, axis=-1)
    return jnp.pad(v, ((0, 0), (0, 0), (0, 8 - NH), (0, rp - r)))


def _collapse_m(mrows, bt, tt):
    nbt = mrows.shape[0]
    return mrows[:, :NH, 0:bt * tt:tt].transpose(0, 2, 1).reshape(nbt * bt, NH)


def kernel(x_prompt, x_sample, state_ssm_re, state_ssm_im, state_conv, state_mlstm_c, state_mlstm_n, state_mlstm_m, norm_w, w_in, i_bias, f_bias, lam_re, lam_im, log_dt, b_re, b_im, c_re, c_im, d_skip, w_glu, w_proj_s, conv_w, w_proj_c, mlstm_norm_w, w_proj_m, w_out, final_norm_w):
    depth = norm_w.shape[0]
    bp, tp, _ = x_prompt.shape
    bs, ts, _ = x_sample.shape
    assert tp % PROMPT_CHUNK == 0 and ts % 8 == 0 and ts & (ts - 1) == 0 and bp % 8 == 0 and bs % 64 == 0

    a_re, a_im, bb_re, bb_im = _s5_discretise(lam_re, lam_im, log_dt, b_re, b_im)
    eye = jnp.eye(16, dtype=F32)

    def pack_c(c):
        c = c.reshape(depth, 2, 16, S5C, S5P)
        return jnp.einsum('lmgcp,gk->lmgpkc', c, eye).reshape(depth, 2, 1024, 256).astype(BF16)

    row = lambda a: a.reshape(depth, 1, a.shape[-1])
    nw = row(norm_w)
    w_s = w_in[:, :, OFF_S:OFF_M].astype(BF16)
    w_m = w_in[:, :, OFF_M:OFF_IF].astype(BF16)
    gate_rows = lambda a: jnp.pad(a.reshape(depth, 2, NH, -1), ((0, 0), (0, 0), (0, 8 - NH), (0, 0))).reshape(depth, 16, -1)
    w_if = gate_rows(jnp.swapaxes(w_in[:, :, OFF_IF:OFF_G], 1, 2)).astype(BF16)
    w_gsc = w_in[:, :, OFF_G:OFF_G + 2 * D].astype(BF16)
    w_gm = w_in[:, :, OFF_G + 2 * D:IN_W].astype(BF16)
    bias = gate_rows(jnp.concatenate([i_bias, f_bias], axis=-1)[:, :, None])
    wa = [nw, w_s, w_gsc, a_re, a_im, bb_re, bb_im, pack_c(c_re), pack_c(c_im), row(d_skip),
          w_glu.astype(BF16), w_proj_s.astype(BF16), conv_w, w_proj_c.astype(BF16)]
    wb = [nw, w_m, w_if, bias, w_gm, row(mlstm_norm_w), w_proj_m.astype(BF16), w_out.astype(BF16),
          final_norm_w.reshape(1, D)]

    sb_a, sb_b = 64, 4
    pb_a, pb_b = 8, 2
    st_a = (state_ssm_re.reshape(depth, bs, S5N), state_ssm_im.reshape(depth, bs, S5N), state_conv)
    st_b = (state_mlstm_c, state_mlstm_n, _expand_m(state_mlstm_m, sb_b, ts))

    yp, ys = x_prompt, x_sample
    cp = cs = None
    outs_p = [[] for _ in range(5)]
    outs_s = [[] for _ in range(5)]
    for l in range(depth):
        msc, sre, sim, cv = _s5conv_call(l, yp, wa, None, pb_a, S5_TILE_T)
        yp, cp, nn, mm = _mlstm_call(l, depth, yp, msc, wb, None, cp, pb_b, PROMPT_CHUNK)
        for i, v in enumerate((sre.reshape(bp, S5G, S5P), sim.reshape(bp, S5G, S5P), cv, nn,
                               _collapse_m(mm, pb_b, PROMPT_CHUNK))):
            outs_p[i].append(v)
        msc, sre, sim, cv = _s5conv_call(l, ys, wa, st_a, sb_a, ts)
        ys, cs, nn, mm = _mlstm_call(l, depth, ys, msc, wb, st_b, cs, sb_b, ts)
        for i, v in enumerate((sre.reshape(bs, S5G, S5P), sim.reshape(bs, S5G, S5P), cv, nn,
                               _collapse_m(mm, sb_b, ts))):
            outs_s[i].append(v)
    sp = [jnp.stack(o) for o in outs_p]
    ss = [jnp.stack(o) for o in outs_s]
    return (yp, ys, sp[0], ss[0], sp[1], ss[1], sp[2], ss[2], cp, cs, sp[3], ss[3], sp[4], ss[4])
```

```python
import functools
import math

import jax
import jax.numpy as jnp
from jax import lax
from jax.experimental import pallas as pl
from jax.experimental.pallas import tpu as pltpu

D = 1024
S5W = 512
S5G = 32
S5P = 64
S5C = 16
S5N = S5G * S5P
CW = 512
CK = 3
NH = 4
DH = 256
PROMPT_CHUNK = 256
S5_TILE_T = 64
EPS = 1e-6
NEG = -1e30
K_SCALE = DH ** -0.5

OFF_S = 0
OFF_M = 3072
OFF_IF = 8192
OFF_G = 8200
IN_W = 11272

F32 = jnp.float32
BF16 = jnp.bfloat16
VMEM_LIMIT = 58 * 1024 * 1024
ROW_CHUNK = 256
PROJ_ROWS = 256


def _sigmoid(x):
    return 1.0 / (1.0 + jnp.exp(-x))


def _silu(x):
    return x * _sigmoid(x)


def _gelu_tanh(x):
    return 0.5 * x * (1.0 + jnp.tanh(math.sqrt(2.0 / math.pi) * (x + 0.044715 * (x * x * x))))


def _log_sigmoid(x):
    return jnp.minimum(x, 0.0) - jnp.log1p(jnp.exp(-jnp.abs(x)))


def _rmsnorm(x, w):
    return x * lax.rsqrt(jnp.mean(x * x, axis=-1, keepdims=True) + EPS) * w


def _dot(a, b):
    return jnp.dot(a, b, preferred_element_type=F32)


def _dot_nt(a, b):
    return lax.dot_general(a, b, (((1,), (1,)), ((), ())), preferred_element_type=F32)


def _dot_tn(a, b):
    return lax.dot_general(a, b, (((0,), (0,)), ((), ())), preferred_element_type=F32)


def _pitch(n):
    p = n + 8
    return p if (p // 8) % 2 == 1 else p + 8


def _s5_disc_kernel(lre_ref, lim_ref, ldt_ref, bre_ref, bim_ref, are_ref, aim_ref, bbre_ref, bbim_ref):
    lr = lre_ref[...]
    li = lim_ref[...]
    dt = jnp.exp(ldt_ref[...])
    ea = jnp.exp(lr * dt)
    ar = ea * jnp.cos(li * dt)
    ai = ea * jnp.sin(li * dt)
    are_ref[...] = ar
    aim_ref[...] = ai
    nr = ar - 1.0
    inv = 1.0 / (lr * lr + li * li)
    cr = (nr * lr + ai * li) * inv
    ci = (ai * lr - nr * li) * inv
    half = S5N // 2
    for h in range(2):
        crh = cr[:, h * half:(h + 1) * half]
        cih = ci[:, h * half:(h + 1) * half]
        br = bre_ref[h]
        bi = bim_ref[h]
        bbre_ref[h] = (crh * br - cih * bi).astype(BF16)
        bbim_ref[h] = (crh * bi + cih * br).astype(BF16)


def _s5_discretise(lam_re, lam_im, log_dt, b_re, b_im):
    depth = lam_re.shape[0]
    eye = jnp.eye(16, dtype=F32)

    def pack_b(b):
        b = b.reshape(depth, 2, 16, S5P, S5C)
        return jnp.einsum('lhgpc,gk->lhgckp', b, eye).reshape(depth, 2, 256, 1024)

    row = lambda a: a.reshape(depth, 1, S5N)
    ldt = jnp.broadcast_to(log_dt[:, :, None], (depth, S5G, S5P))
    vec = pl.BlockSpec((None, 1, S5N), lambda l: (l, 0, 0))
    mat = pl.BlockSpec((None, 2, 256, 1024), lambda l: (l, 0, 0, 0))
    return pl.pallas_call(
        _s5_disc_kernel,
        grid=(depth,),
        in_specs=[vec, vec, vec, mat, mat],
        out_specs=[vec, vec, mat, mat],
        out_shape=[jax.ShapeDtypeStruct((depth, 1, S5N), F32), jax.ShapeDtypeStruct((depth, 1, S5N), F32),
                   jax.ShapeDtypeStruct((depth, 2, 256, 1024), BF16),
                   jax.ShapeDtypeStruct((depth, 2, 256, 1024), BF16)],
        name="s5_discretise",
    )(row(lam_re), row(lam_im), row(ldt), pack_b(b_re), pack_b(b_im))


def _s5conv_kernel(has_state, bt, tt, *refs):
    refs = list(refs)
    (x_ref, nw_ref, ws_ref, wg_ref, are_ref, aim_ref, bbre_ref, bbim_ref, cre_ref, cim_ref,
     dsk_ref, wglu_ref, wps_ref, cw_ref, wpc_ref) = refs[:15]
    refs = refs[15:]
    if has_state:
        s0re_ref, s0im_ref, cv0_ref = refs[:3]
        refs = refs[3:]
    msc_ref, sre_ref, sim_ref, cvo_ref = refs[:4]
    h_scr, ps_scr, g_scr, uslab, utb, bu_scr, yslab, y_scr, vhalo = refs[4:]

    r = bt * tt
    pt = _pitch(tt)
    pb = _pitch(bt)
    rc = min(ROW_CHUNK, r)
    nchunk = r // rc
    ti = pl.program_id(1)

    def merge_gates(ci):
        rows = slice(ci * rc, (ci + 1) * rc)
        g_scr[rows, :] = _sigmoid(_dot(h_scr[rows, :], wg_ref[...]))

    @pl.when(ti == 0)
    def _init():
        if has_state:
            sre_ref[...] = s0re_ref[...]
            sim_ref[...] = s0im_ref[...]
            vhalo[:, 6:8, :] = cv0_ref[...]
        else:
            sre_ref[...] = jnp.zeros_like(sre_ref)
            sim_ref[...] = jnp.zeros_like(sim_ref)
            vhalo[:, 6:8, :] = jnp.zeros((bt, 2, CW), F32)

    x = x_ref[...].reshape(r, D)
    h = _rmsnorm(x, nw_ref[...]).astype(BF16)
    h_scr[...] = h
    ps_scr[...] = _dot(h, ws_ref[...])


    for b in range(bt):
        for j in range(4):
            uslab[j, b * pt:b * pt + tt, :] = ps_scr[b * tt:(b + 1) * tt, j * 128:(j + 1) * 128]
    for t in range(tt):
        for bs in range(bt // 8):
            dst = t * bt + bs * 8
            for j in range(4):
                utb[dst:dst + 8, j * 128:(j + 1) * 128] = uslab[j, pl.ds(bs * 8 * pt + t, 8, stride=pt), :]
    for ci in range(nchunk // 2):
        merge_gates(ci)

    for hf in range(2):
        uh = utb[:, hf * 256:(hf + 1) * 256].astype(BF16)
        bu_scr[:, hf * 1024:(hf + 1) * 1024] = _dot(uh, bbre_ref[hf])
        bu_scr[:, S5N + hf * 1024:S5N + (hf + 1) * 1024] = _dot(uh, bbim_ref[hf])
    for ci in range(nchunk // 2, nchunk):
        merge_gates(ci)

    v = ps_scr[:, 1536:2048] * ps_scr[:, 2048:2560]
    vhalo[:, 8:8 + tt, :] = v.reshape(bt, tt, CW)
    cw = cw_ref[...]
    yc = (cw[0:1, :] * vhalo[:, 6:6 + tt, :] + cw[1:2, :] * vhalo[:, 7:7 + tt, :]
          + cw[2:3, :] * vhalo[:, 8:8 + tt, :])
    ps_scr[:, 2048:2560] = yc.reshape(r, CW)
    new_halo = vhalo[:, tt + 6:tt + 8, :]
    vhalo[:, 6:8, :] = new_halo
    cvo_ref[...] = new_halo

    for ci in range(nchunk):
        rows = slice(ci * rc, (ci + 1) * rc)
        tc_ = ps_scr[rows, 1024:1536] * ps_scr[rows, 2048:2560] * _silu(ps_scr[rows, 2560:3072])
        ps_scr[rows, 1024:2048] = g_scr[rows, D:] * _dot(tc_.astype(BF16), wpc_ref[...])

    for q in range(4):
        c_re = slice(q * 512, (q + 1) * 512)
        c_im = slice(S5N + q * 512, S5N + (q + 1) * 512)
        ar = jnp.broadcast_to(are_ref[:, c_re], (8, 512))
        ai = jnp.broadcast_to(aim_ref[:, c_re], (8, 512))
        for bs in range(bt // 8):
            r0 = bs * 8
            sr = sre_ref[r0:r0 + 8, c_re]
            si = sim_ref[r0:r0 + 8, c_re]
            for t in range(tt):
                row = t * bt + r0
                nr = ar * sr - ai * si + bu_scr[row:row + 8, c_re]
                ni = ar * si + ai * sr + bu_scr[row:row + 8, c_im]
                bu_scr[row:row + 8, c_re] = nr
                bu_scr[row:row + 8, c_im] = ni
                sr, si = nr, ni
            sre_ref[r0:r0 + 8, c_re] = sr
            sim_ref[r0:r0 + 8, c_re] = si

    for m in range(2):
        sre = bu_scr[:, m * 1024:(m + 1) * 1024].astype(BF16)
        sim = bu_scr[:, S5N + m * 1024:S5N + (m + 1) * 1024].astype(BF16)
        ytb = _dot(sre, cre_ref[m]) - _dot(sim, cim_ref[m])
        for t in range(tt):
            for jj in range(2):
                yslab[2 * m + jj, t * pb:t * pb + bt, :] = ytb[t * bt:(t + 1) * bt, jj * 128:(jj + 1) * 128]
    for b in range(bt):
        for ts in range(tt // 8):
            dst = b * tt + ts * 8
            for j in range(4):
                y_scr[dst:dst + 8, j * 128:(j + 1) * 128] = yslab[j, pl.ds(ts * 8 * pb + b, 8, stride=pb), :]

    nb = rc // tt if rc >= tt else 0
    dsk = dsk_ref[...]
    for ci in range(nchunk):
        rows = slice(ci * rc, (ci + 1) * rc)
        y = y_scr[rows, :] + dsk * ps_scr[rows, 0:512]
        gl = _gelu_tanh(y)
        glu = gl * _sigmoid(_dot(gl.astype(BF16), wglu_ref[...]))
        ts_ = glu * _silu(ps_scr[rows, 512:1024])
        ys = _dot(ts_.astype(BF16), wps_ref[...])
        out = g_scr[rows, :D] * ys + ps_scr[rows, 1024:2048]
        if nb:
            msc_ref[ci * nb:(ci + 1) * nb] = out.reshape(nb, tt, D)
        else:
            per = tt // rc
            msc_ref[ci // per, (ci % per) * rc:(ci % per + 1) * rc, :] = out


def _s5conv_call(layer, x, weights, state, bt, tt):
    b, t, _ = x.shape
    r = bt * tt
    has_state = state is not None
    wspec = lambda shape: pl.BlockSpec((None,) + shape, lambda bi, ti: (layer,) + (0,) * len(shape),
                                       pipeline_mode=pl.Buffered(1))
    tile = pl.BlockSpec((bt, tt, D), lambda bi, ti: (bi, ti, 0))
    in_specs = [tile, wspec((1, D)), wspec((D, 3072)), wspec((D, 2048)), wspec((1, S5N)), wspec((1, S5N)),
                wspec((2, 256, 1024)), wspec((2, 256, 1024)), wspec((2, 1024, 256)), wspec((2, 1024, 256)),
                wspec((1, S5W)), wspec((S5W, S5W)), wspec((S5W, D)), wspec((CK, CW)), wspec((CW, D))]
    args = [x] + list(weights)
    if has_state:
        in_specs += [pl.BlockSpec((None, bt, S5N), lambda bi, ti: (layer, bi, 0)),
                     pl.BlockSpec((None, bt, S5N), lambda bi, ti: (layer, bi, 0)),
                     pl.BlockSpec((None, bt, CK - 1, CW), lambda bi, ti: (layer, bi, 0, 0))]
        args += list(state)
    out_specs = [tile,
                 pl.BlockSpec((bt, S5N), lambda bi, ti: (bi, 0)),
                 pl.BlockSpec((bt, S5N), lambda bi, ti: (bi, 0)),
                 pl.BlockSpec((bt, CK - 1, CW), lambda bi, ti: (bi, 0, 0))]
    out_shape = [jax.ShapeDtypeStruct((b, t, D), F32), jax.ShapeDtypeStruct((b, S5N), F32),
                 jax.ShapeDtypeStruct((b, S5N), F32), jax.ShapeDtypeStruct((b, CK - 1, CW), F32)]
    scratch = [pltpu.VMEM((r, D), BF16),
               pltpu.VMEM((r, 3072), F32),
               pltpu.VMEM((r, 2 * D), F32),
               pltpu.VMEM((4, bt * _pitch(tt), 128), F32),
               pltpu.VMEM((r, S5W), F32),
               pltpu.VMEM((r, 2 * S5N), F32),
               pltpu.VMEM((4, tt * _pitch(bt), 128), F32),
               pltpu.VMEM((r, S5W), F32),
               pltpu.VMEM((bt, tt + 8, CW), F32)]
    return pl.pallas_call(
        functools.partial(_s5conv_kernel, has_state, bt, tt),
        grid=(b // bt, t // tt),
        in_specs=in_specs, out_specs=out_specs, out_shape=out_shape, scratch_shapes=scratch,
        compiler_params=pltpu.CompilerParams(dimension_semantics=("arbitrary", "arbitrary"),
                                             vmem_limit_bytes=VMEM_LIMIT),
        name="s5conv_state" if has_state else "s5conv",
    )(*args)


def _seg_scan(x, pos, seg, op, fill):
    s = 1
    while s < seg:
        x = op(x, jnp.where(pos >= s, pltpu.roll(x, s, 1), fill))
        s *= 2
    return x


def _seg_bcast_last(x, pos, seg):
    n = x.shape[1]
    s = 1
    while s < seg:
        x = jnp.where(pos + s <= seg - 1, pltpu.roll(x, n - s, 1), x)
        s *= 2
    return x


def _mlstm_kernel(has_state, has_alias, final, single_tile, bt, tt, nsub, *refs):
    refs = list(refs)
    x_ref, msc_ref, nw_ref = refs[:3]
    wm_refs = refs[3:8]
    wif_ref, bias_ref, wgm_ref, hnw_ref, wpm_ref, wout_ref, fnw_ref = refs[8:15]
    refs = refs[15:]
    if has_state:
        c0_ref, n0_ref, m0_ref = refs[:3]
        refs = refs[3:]
    if has_alias:
        refs = refs[1:]
    y_ref, c_ref, n_ref, m_ref = refs[:4]
    h_scr, p_scr, hm_scr, sg_scr = refs[4:]

    r = bt * tt
    ro = r * nsub
    rp = max(r, 128)
    step = pl.program_id(1)

    def init_state():
        if has_state:
            c_ref[...] = c0_ref[...]
            n_ref[...] = n0_ref[...]
            m_ref[...] = m0_ref[...]
        else:
            c_ref[...] = jnp.zeros_like(c_ref)
            n_ref[...] = jnp.zeros_like(n_ref)
            m_ref[...] = jnp.zeros_like(m_ref)

    def normalise():
        h_scr[...] = _rmsnorm(x_ref[...].reshape(ro, D), nw_ref[...]).astype(BF16)

    def project():
        hh = h_scr[...]
        for k, w_ref in enumerate(wm_refs):
            p_scr[:, k * D:(k + 1) * D] = _dot(hh, w_ref[...])
        if nsub > 1:
            for piece in range(NH):
                merge_gate(piece)

    def merge_gate(piece):
        cols = slice(piece * DH, (piece + 1) * DH)
        sg_scr[:, cols] = _sigmoid(_dot(h_scr[...], wgm_ref[:, cols]))

    if single_tile and has_state:
        c_prev, n_prev, m_prev = c0_ref, n0_ref, m0_ref
    else:
        pl.when(step == 0)(init_state) if not single_tile else init_state()
        c_prev, n_prev, m_prev = c_ref, n_ref, m_ref

    if nsub == 1:
        base = 0
        normalise()
        h = h_scr[...]
    else:
        base = pl.multiple_of(step * r, r)
        pl.when(step == 0)(normalise)
        h = h_scr[pl.ds(base, r), :]

    hp = h if rp == r else jnp.concatenate([h, jnp.zeros((rp - r, D), BF16)], axis=0)
    gt = _dot_nt(wif_ref[...], hp) + bias_ref[...]
    if nsub == 1:
        project()
    else:
        pl.when(step == 0)(project)
    lane = lax.broadcasted_iota(jnp.int32, (8, rp), 1)
    pos = jnp.bitwise_and(lane, tt - 1)
    li = gt[0:8, :]
    lf = _log_sigmoid(gt[8:16, :])
    bcum = _seg_scan(lf, pos, tt, jnp.add, 0.0)
    g = li - bcum
    mprev = m_prev[...]
    mrun = jnp.maximum(mprev, _seg_scan(g, pos, tt, jnp.maximum, NEG))
    mlast = _seg_bcast_last(mrun, pos, tt)
    winter = jnp.exp(mprev - mrun)
    efloor = jnp.exp(-(bcum + mrun))
    wrow = jnp.exp(g - mlast)
    decay = jnp.exp(mprev - mlast)
    m_ref[...] = _seg_bcast_last(bcum, pos, tt) + mlast
    zt = jnp.concatenate([mrun, winter, efloor, wrow], axis=0).T

    row_i = lax.broadcasted_iota(jnp.int32, (tt, tt), 0)
    col_i = lax.broadcasted_iota(jnp.int32, (tt, tt), 1)
    causal = row_i >= col_i

    for b in range(bt):
        rows = slice(b * tt, (b + 1) * tt)
        prow = rows if nsub == 1 else pl.ds(base + b * tt, tt)
        for hd in range(NH):
            cs = slice(hd * DH, (hd + 1) * DH)
            qf = p_scr[prow, cs]
            kf = p_scr[prow, D + hd * DH:D + (hd + 1) * DH] * K_SCALE
            vb = p_scr[prow, 2 * D + hd * DH:2 * D + (hd + 1) * DH].astype(BF16)
            qb = qf.astype(BF16)
            kb = kf.astype(BF16)
            mcol = zt[rows, hd:hd + 1]
            wi = zt[rows, 8 + hd:9 + hd]
            ef = zt[rows, 16 + hd:17 + hd]
            wc = zt[rows, 24 + hd:25 + hd]
            grow = g[hd:hd + 1, b * tt:(b + 1) * tt]
            dm = jnp.exp(jnp.where(causal, grow - mcol, NEG))
            sc = _dot_nt(qb, kb) * dm
            cf = c_prev[b, hd]
            nrow = n_prev[b, hd:hd + 1, :]
            num = wi * _dot(qb, cf.astype(BF16)) + _dot(sc.astype(BF16), vb)
            den = wi * jnp.sum(qf * nrow, axis=-1, keepdims=True) + jnp.sum(sc, axis=-1, keepdims=True)
            hm_scr[prow, cs] = num * (1.0 / jnp.maximum(jnp.abs(den), ef))
            dec = decay[hd:hd + 1, b * tt:b * tt + 1]
            wk = kf * wc
            c_ref[b, hd] = dec * cf + _dot_tn(wk.astype(BF16), vb)
            n_ref[b, hd:hd + 1, :] = dec * nrow + jnp.sum(wk, axis=0, keepdims=True)
            done = b * NH + hd + 1
            if nsub == 1 and (done * NH) % (bt * NH) == 0:
                merge_gate(done * NH // (bt * NH) - 1)

    def epilogue():
        rc = min(ROW_CHUNK, ro)
        hnw = hnw_ref[...]
        for ci in range(ro // rc):
            rows = slice(ci * rc, (ci + 1) * rc)
            hmv = hm_scr[rows, :] * _sigmoid(p_scr[rows, 3 * D:4 * D])
            parts = []
            for hd in range(NH):
                hh = hmv[:, hd * DH:(hd + 1) * DH]
                mu = jnp.mean(hh, axis=-1, keepdims=True)
                dv = hh - mu
                var = jnp.mean(dv * dv, axis=-1, keepdims=True)
                parts.append(dv * lax.rsqrt(var + EPS) * hnw[:, hd * DH:(hd + 1) * DH])
            hn = jnp.concatenate(parts, axis=1)
            tm = hn * _silu(p_scr[rows, 4 * D:5 * D])
            ym = _dot(tm.astype(BF16), wpm_ref[...])
            nb = rc // tt if rc >= tt else 0
            if nb:
                msc = msc_ref[ci * nb:(ci + 1) * nb].reshape(rc, D)
                xr = x_ref[ci * nb:(ci + 1) * nb].reshape(rc, D)
            else:
                per = tt // rc
                msc = msc_ref[ci // per, (ci % per) * rc:(ci % per + 1) * rc, :]
                xr = x_ref[ci // per, (ci % per) * rc:(ci % per + 1) * rc, :]
            merged = msc + sg_scr[rows, :] * ym
            y = xr + _dot(merged.astype(BF16), wout_ref[...])
            if final:
                y = _rmsnorm(y, fnw_ref[...])
            if nb:
                y_ref[ci * nb:(ci + 1) * nb] = y.reshape(nb, tt, D)
            else:
                y_ref[ci // per, (ci % per) * rc:(ci % per + 1) * rc, :] = y

    if nsub == 1:
        epilogue()
    else:
        pl.when(step == nsub - 1)(epilogue)


def _mlstm_call(layer, depth, x, msc, weights, state, c_stack, bt, tt):
    b, t, _ = x.shape
    r = bt * tt
    rp = max(r, 128)
    nbt = b // bt
    has_state = state is not None
    has_alias = c_stack is not None
    final = layer == depth - 1
    nsub = max(1, min(nbt, PROJ_ROWS // r)) if t == tt else 1
    assert nbt % nsub == 0
    ro = r * nsub
    if nsub == 1:
        grid = (nbt, t // tt)
        tile_idx = lambda bi, si: (bi, si, 0)
        sub_idx = lambda bi, si: bi
    else:
        grid = (nbt // nsub, nsub)
        tile_idx = lambda bi, si: (bi, 0, 0)
        sub_idx = lambda bi, si: bi * nsub + si
    wspec = lambda shape: pl.BlockSpec((None,) + shape, lambda bi, si: (layer,) + (0,) * len(shape),
                                       pipeline_mode=pl.Buffered(1))
    tile = pl.BlockSpec((bt * nsub, tt, D), tile_idx)
    wcol = lambda k: pl.BlockSpec((None, D, D), lambda bi, si: (layer, 0, k), pipeline_mode=pl.Buffered(1))
    nw, w_main, w_if, bias, w_gates, hnw, wpm, wout, fnw = weights
    in_specs = ([tile, tile, wspec((1, D))] + [wcol(OFF_M // D + k) for k in range(5)]
                + [wspec((16, D)), wspec((16, 1)), wcol(2), wspec((1, D)), wspec((D, D)), wspec((D, D)),
                   pl.BlockSpec((1, D), lambda bi, si: (0, 0), pipeline_mode=pl.Buffered(1))])
    args = [x, msc, nw] + [w_main] * 5 + [w_if, bias, w_gates, hnw, wpm, wout, fnw]
    if has_state:
        in_specs += [pl.BlockSpec((None, bt, NH, DH, DH), lambda bi, si: (layer, sub_idx(bi, si), 0, 0, 0)),
                     pl.BlockSpec((None, bt, NH, DH), lambda bi, si: (layer, sub_idx(bi, si), 0, 0)),
                     pl.BlockSpec((None, None, 8, rp), lambda bi, si: (layer, sub_idx(bi, si), 0, 0))]
        args += list(state)
    aliases = {}
    if has_alias:
        aliases = {len(args): 1}
        in_specs.append(pl.BlockSpec(memory_space=pl.ANY))
        args.append(c_stack)
    out_specs = [tile,
                 pl.BlockSpec((None, bt, NH, DH, DH), lambda bi, si: (layer, sub_idx(bi, si), 0, 0, 0)),
                 pl.BlockSpec((bt, NH, DH), lambda bi, si: (sub_idx(bi, si), 0, 0)),
                 pl.BlockSpec((None, 8, rp), lambda bi, si: (sub_idx(bi, si), 0, 0))]
    out_shape = [jax.ShapeDtypeStruct((b, t, D), F32), jax.ShapeDtypeStruct((depth, b, NH, DH, DH), F32),
                 jax.ShapeDtypeStruct((b, NH, DH), F32), jax.ShapeDtypeStruct((nbt, 8, rp), F32)]
    scratch = [pltpu.VMEM((ro, D), BF16),
               pltpu.VMEM((ro, 5 * D), F32),
               pltpu.VMEM((ro, D), F32),
               pltpu.VMEM((ro, D), F32)]
    return pl.pallas_call(
        functools.partial(_mlstm_kernel, has_state, has_alias, final, t == tt, bt, tt, nsub),
        grid=grid,
        in_specs=in_specs, out_specs=out_specs, out_shape=out_shape, scratch_shapes=scratch,
        input_output_aliases=aliases,
        compiler_params=pltpu.CompilerParams(dimension_semantics=("arbitrary", "arbitrary"),
                                             vmem_limit_bytes=VMEM_LIMIT),
        name="mlstm_state" if has_state else "mlstm",
    )(*args)


def _expand_m(m, bt, tt):
    depth, b, _ = m.shape
    r = bt * tt
    rp = max(r, 128)
    v = jnp.repeat(m.reshape(depth, b // bt, bt, NH).transpose(0, 1, 3, 2), tt, axis=-1)
    return jnp.pad(v, ((0, 0), (0, 0), (0, 8 - NH), (0, rp - r)))


def _collapse_m(mrows, bt, tt):
    nbt = mrows.shape[0]
    return mrows[:, :NH, 0:bt * tt:tt].transpose(0, 2, 1).reshape(nbt * bt, NH)


def kernel(x_prompt, x_sample, state_ssm_re, state_ssm_im, state_conv, state_mlstm_c, state_mlstm_n, state_mlstm_m, norm_w, w_in, i_bias, f_bias, lam_re, lam_im, log_dt, b_re, b_im, c_re, c_im, d_skip, w_glu, w_proj_s, conv_w, w_proj_c, mlstm_norm_w, w_proj_m, w_out, final_norm_w):
    depth = norm_w.shape[0]
    bp, tp, _ = x_prompt.shape
    bs, ts, _ = x_sample.shape
    assert tp % PROMPT_CHUNK == 0 and ts % 8 == 0 and ts & (ts - 1) == 0 and bp % 8 == 0 and bs % 64 == 0

    a_re, a_im, bb_re, bb_im = _s5_discretise(lam_re, lam_im, log_dt, b_re, b_im)
    eye = jnp.eye(16, dtype=F32)

    def pack_c(c):
        c = c.reshape(depth, 2, 16, S5C, S5P)
        return jnp.einsum('lmgcp,gk->lmgpkc', c, eye).reshape(depth, 2, 1024, 256).astype(BF16)

    row = lambda a: a.reshape(depth, 1, a.shape[-1])
    nw = row(norm_w)
    w_main = w_in[:, :, :OFF_IF].astype(BF16)
    w_gates = w_in[:, :, OFF_G:IN_W].astype(BF16)
    gate_rows = lambda a: jnp.pad(a.reshape(depth, 2, NH, -1), ((0, 0), (0, 0), (0, 8 - NH), (0, 0))).reshape(depth, 16, -1)
    w_if = gate_rows(jnp.swapaxes(w_in[:, :, OFF_IF:OFF_G], 1, 2)).astype(BF16)
    bias = gate_rows(jnp.concatenate([i_bias, f_bias], axis=-1)[:, :, None])
    wa = [nw, w_main, w_gates, a_re, a_im, bb_re, bb_im, pack_c(c_re), pack_c(c_im), row(d_skip),
          w_glu.astype(BF16), w_proj_s.astype(BF16), conv_w, w_proj_c.astype(BF16)]
    wb = [nw, w_main, w_if, bias, w_gates, row(mlstm_norm_w), w_proj_m.astype(BF16), w_out.astype(BF16),
          final_norm_w.reshape(1, D)]

    sb_a, sb_b = 64, 4
    pb_a, pb_b = 8, 2
    st_a = (state_ssm_re.reshape(depth, bs, S5N), state_ssm_im.reshape(depth, bs, S5N), state_conv)
    st_b = (state_mlstm_c, state_mlstm_n, _expand_m(state_mlstm_m, sb_b, ts))

    yp, ys = x_prompt, x_sample
    cp = cs = None
    outs_p = [[] for _ in range(5)]
    outs_s = [[] for _ in range(5)]
    for l in range(depth):
        msc, sre, sim, cv = _s5conv_call(l, yp, wa, None, pb_a, S5_TILE_T)
        yp, cp, nn, mm = _mlstm_call(l, depth, yp, msc, wb, None, cp, pb_b, PROMPT_CHUNK)
        for i, v in enumerate((sre.reshape(bp, S5G, S5P), sim.reshape(bp, S5G, S5P), cv, nn,
                               _collapse_m(mm, pb_b, PROMPT_CHUNK))):
            outs_p[i].append(v)
        msc, sre, sim, cv = _s5conv_call(l, ys, wa, st_a, sb_a, ts)
        ys, cs, nn, mm = _mlstm_call(l, depth, ys, msc, wb, st_b, cs, sb_b, ts)
        for i, v in enumerate((sre.reshape(bs, S5G, S5P), sim.reshape(bs, S5G, S5P), cv, nn,
                               _collapse_m(mm, sb_b, ts))):
            outs_s[i].append(v)
    sp = [jnp.stack(o) for o in outs_p]
    ss = [jnp.stack(o) for o in outs_s]
    return (yp, ys, sp[0], ss[0], sp[1], ss[1], sp[2], ss[2], cp, cs, sp[3], ss[3], sp[4], ss[4])
```

```python
import functools
import math

import jax
import jax.numpy as jnp
from jax import lax
from jax.experimental import pallas as pl
from jax.experimental.pallas import tpu as pltpu

D = 1024
S5W = 512
S5G = 32
S5P = 64
S5C = 16
S5N = S5G * S5P
CW = 512
CK = 3
NH = 4
DH = 256
PROMPT_CHUNK = 256
S5_TILE_T = 64
SHORT_SEQ = 32
EPS = 1e-6
NEG = -1e30
K_SCALE = DH ** -0.5

OFF_S = 0
OFF_M = 3072
OFF_IF = 8192
OFF_G = 8200
IN_W = 11272

F32 = jnp.float32
BF16 = jnp.bfloat16
VMEM_LIMIT = 58 * 1024 * 1024
ROW_CHUNK = 256
PROJ_ROWS = 256


def _sigmoid(x):
    return 1.0 / (1.0 + jnp.exp(-x))


def _silu(x):
    return x * _sigmoid(x)


def _gelu_tanh(x):
    return 0.5 * x * (1.0 + jnp.tanh(math.sqrt(2.0 / math.pi) * (x + 0.044715 * (x * x * x))))


def _log_sigmoid(x):
    return jnp.minimum(x, 0.0) - jnp.log1p(jnp.exp(-jnp.abs(x)))


def _rmsnorm(x, w):
    return x * lax.rsqrt(jnp.mean(x * x, axis=-1, keepdims=True) + EPS) * w


def _dot(a, b):
    return jnp.dot(a, b, preferred_element_type=F32)


def _dot_nt(a, b):
    return lax.dot_general(a, b, (((1,), (1,)), ((), ())), preferred_element_type=F32)


def _dot_tn(a, b):
    return lax.dot_general(a, b, (((0,), (0,)), ((), ())), preferred_element_type=F32)


def _pitch(n):
    p = n + 8
    return p if (p // 8) % 2 == 1 else p + 8


def _s5_disc_kernel(lre_ref, lim_ref, ldt_ref, bre_ref, bim_ref, cre_ref, cim_ref,
                    are_ref, aim_ref, bbre_ref, bbim_ref, cpre_ref, cpim_ref):
    lr = lre_ref[...]
    li = lim_ref[...]
    dt = jnp.exp(ldt_ref[...])
    ea = jnp.exp(lr * dt)
    ar = ea * jnp.cos(li * dt)
    ai = ea * jnp.sin(li * dt)
    are_ref[...] = ar
    aim_ref[...] = ai
    nr = ar - 1.0
    inv = 1.0 / (lr * lr + li * li)
    cr = (nr * lr + ai * li) * inv
    ci = (ai * lr - nr * li) * inv
    half = S5N // 2
    shr = lax.shift_right_logical
    bdiag = (shr(lax.broadcasted_iota(jnp.int32, (256, half), 0), 4)
             == shr(lax.broadcasted_iota(jnp.int32, (256, half), 1), 6))
    cdiag = (shr(lax.broadcasted_iota(jnp.int32, (half, 256), 0), 6)
             == shr(lax.broadcasted_iota(jnp.int32, (half, 256), 1), 4))
    for h in range(2):
        crh = cr[:, h * half:(h + 1) * half]
        cih = ci[:, h * half:(h + 1) * half]
        br = bre_ref[h]
        bi = bim_ref[h]
        bbre_ref[h] = jnp.where(bdiag, crh * br - cih * bi, 0.0).astype(BF16)
        bbim_ref[h] = jnp.where(bdiag, crh * bi + cih * br, 0.0).astype(BF16)
        cpre_ref[h] = jnp.where(cdiag, cre_ref[h], 0.0).astype(BF16)
        cpim_ref[h] = jnp.where(cdiag, cim_ref[h], 0.0).astype(BF16)


def _s5_discretise(lam_re, lam_im, log_dt, b_re, b_im, c_re, c_im):
    depth = lam_re.shape[0]

    def tile_b(b):
        b = jnp.swapaxes(b.reshape(depth, 2, 16, S5P, S5C), 3, 4).reshape(depth, 2, 256, 1, S5P)
        return jnp.broadcast_to(b, (depth, 2, 256, 16, S5P)).reshape(depth, 2, 256, 1024)

    def tile_c(c):
        c = jnp.swapaxes(c.reshape(depth, 2, 16, S5C, S5P), 3, 4).reshape(depth, 2, 1024, 1, S5C)
        return jnp.broadcast_to(c, (depth, 2, 1024, 16, S5C)).reshape(depth, 2, 1024, 256)

    row = lambda a: a.reshape(depth, 1, S5N)
    ldt = jnp.broadcast_to(log_dt[:, :, None], (depth, S5G, S5P))
    vec = pl.BlockSpec((None, 1, S5N), lambda l: (l, 0, 0))
    bmat = pl.BlockSpec((None, 2, 256, 1024), lambda l: (l, 0, 0, 0))
    cmat = pl.BlockSpec((None, 2, 1024, 256), lambda l: (l, 0, 0, 0))
    return pl.pallas_call(
        _s5_disc_kernel,
        grid=(depth,),
        in_specs=[vec, vec, vec, bmat, bmat, cmat, cmat],
        out_specs=[vec, vec, bmat, bmat, cmat, cmat],
        out_shape=[jax.ShapeDtypeStruct((depth, 1, S5N), F32), jax.ShapeDtypeStruct((depth, 1, S5N), F32),
                   jax.ShapeDtypeStruct((depth, 2, 256, 1024), BF16),
                   jax.ShapeDtypeStruct((depth, 2, 256, 1024), BF16),
                   jax.ShapeDtypeStruct((depth, 2, 1024, 256), BF16),
                   jax.ShapeDtypeStruct((depth, 2, 1024, 256), BF16)],
        name="s5_discretise",
    )(row(lam_re), row(lam_im), row(ldt), tile_b(b_re), tile_b(b_im), tile_c(c_re), tile_c(c_im))


def _cast_kernel(w_ref, o_ref):
    o_ref[...] = w_ref[...].astype(BF16)


def _cast_main_columns(w_in, n_blocks):
    depth = w_in.shape[0]
    blk = pl.BlockSpec((None, D, D), lambda l, k: (l, 0, k))
    return pl.pallas_call(
        _cast_kernel,
        grid=(depth, n_blocks),
        in_specs=[blk], out_specs=blk,
        out_shape=jax.ShapeDtypeStruct((depth, D, n_blocks * D), BF16),
        name="cast_weights",
    )(w_in)


def _s5conv_kernel(has_state, bt, tt, *refs):
    refs = list(refs)
    (x_ref, nw_ref, ws_ref, wg_ref, are_ref, aim_ref, bbre_ref, bbim_ref, cre_ref, cim_ref,
     dsk_ref, wglu_ref, wps_ref, cw_ref, wpc_ref) = refs[:15]
    refs = refs[15:]
    if has_state:
        s0re_ref, s0im_ref, cv0_ref = refs[:3]
        refs = refs[3:]
    msc_ref, sre_ref, sim_ref, cvo_ref = refs[:4]
    h_scr, ps_scr, g_scr, uslab, utb, bu_scr, yslab, y_scr, vhalo = refs[4:]

    r = bt * tt
    pt = _pitch(tt)
    pb = _pitch(bt)
    rc = min(ROW_CHUNK, r)
    nchunk = r // rc
    ti = pl.program_id(1)

    def merge_gates(ci):
        rows = slice(ci * rc, (ci + 1) * rc)
        g_scr[rows, :] = _sigmoid(_dot(h_scr[rows, :], wg_ref[...]))

    @pl.when(ti == 0)
    def _init():
        if has_state:
            sre_ref[...] = s0re_ref[...]
            sim_ref[...] = s0im_ref[...]
            vhalo[:, 6:8, :] = cv0_ref[...]
        else:
            sre_ref[...] = jnp.zeros_like(sre_ref)
            sim_ref[...] = jnp.zeros_like(sim_ref)
            vhalo[:, 6:8, :] = jnp.zeros((bt, 2, CW), F32)

    x = x_ref[...].reshape(r, D)
    h = _rmsnorm(x, nw_ref[...]).astype(BF16)
    h_scr[...] = h
    ps_scr[...] = _dot(h, ws_ref[...])


    for b in range(bt):
        for j in range(4):
            uslab[j, b * pt:b * pt + tt, :] = ps_scr[b * tt:(b + 1) * tt, j * 128:(j + 1) * 128]
    for t in range(tt):
        for bs in range(bt // 8):
            dst = t * bt + bs * 8
            for j in range(4):
                utb[dst:dst + 8, j * 128:(j + 1) * 128] = uslab[j, pl.ds(bs * 8 * pt + t, 8, stride=pt), :]
    for ci in range(nchunk // 2):
        merge_gates(ci)

    for hf in range(2):
        uh = utb[:, hf * 256:(hf + 1) * 256].astype(BF16)
        bu_scr[:, hf * 1024:(hf + 1) * 1024] = _dot(uh, bbre_ref[hf])
        bu_scr[:, S5N + hf * 1024:S5N + (hf + 1) * 1024] = _dot(uh, bbim_ref[hf])
    for ci in range(nchunk // 2, nchunk):
        merge_gates(ci)

    v = ps_scr[:, 1536:2048] * ps_scr[:, 2048:2560]
    vhalo[:, 8:8 + tt, :] = v.reshape(bt, tt, CW)
    cw = cw_ref[...]
    yc = (cw[0:1, :] * vhalo[:, 6:6 + tt, :] + cw[1:2, :] * vhalo[:, 7:7 + tt, :]
          + cw[2:3, :] * vhalo[:, 8:8 + tt, :])
    ps_scr[:, 2048:2560] = yc.reshape(r, CW)
    new_halo = vhalo[:, tt + 6:tt + 8, :]
    vhalo[:, 6:8, :] = new_halo
    cvo_ref[...] = new_halo

    for ci in range(nchunk):
        rows = slice(ci * rc, (ci + 1) * rc)
        tc_ = ps_scr[rows, 1024:1536] * ps_scr[rows, 2048:2560] * _silu(ps_scr[rows, 2560:3072])
        ps_scr[rows, 1024:2048] = g_scr[rows, D:] * _dot(tc_.astype(BF16), wpc_ref[...])

    for q in range(4):
        c_re = slice(q * 512, (q + 1) * 512)
        c_im = slice(S5N + q * 512, S5N + (q + 1) * 512)
        ar = jnp.broadcast_to(are_ref[:, c_re], (8, 512))
        ai = jnp.broadcast_to(aim_ref[:, c_re], (8, 512))
        for bs in range(bt // 8):
            r0 = bs * 8
            sr = sre_ref[r0:r0 + 8, c_re]
            si = sim_ref[r0:r0 + 8, c_re]
            for t in range(tt):
                row = t * bt + r0
                nr = ar * sr - ai * si + bu_scr[row:row + 8, c_re]
                ni = ar * si + ai * sr + bu_scr[row:row + 8, c_im]
                bu_scr[row:row + 8, c_re] = nr
                bu_scr[row:row + 8, c_im] = ni
                sr, si = nr, ni
            sre_ref[r0:r0 + 8, c_re] = sr
            sim_ref[r0:r0 + 8, c_re] = si

    for m in range(2):
        sre = bu_scr[:, m * 1024:(m + 1) * 1024].astype(BF16)
        sim = bu_scr[:, S5N + m * 1024:S5N + (m + 1) * 1024].astype(BF16)
        ytb = _dot(sre, cre_ref[m]) - _dot(sim, cim_ref[m])
        for t in range(tt):
            for jj in range(2):
                yslab[2 * m + jj, t * pb:t * pb + bt, :] = ytb[t * bt:(t + 1) * bt, jj * 128:(jj + 1) * 128]
    for b in range(bt):
        for ts in range(tt // 8):
            dst = b * tt + ts * 8
            for j in range(4):
                y_scr[dst:dst + 8, j * 128:(j + 1) * 128] = yslab[j, pl.ds(ts * 8 * pb + b, 8, stride=pb), :]

    nb = rc // tt if rc >= tt else 0
    dsk = dsk_ref[...]
    for ci in range(nchunk):
        rows = slice(ci * rc, (ci + 1) * rc)
        y = y_scr[rows, :] + dsk * ps_scr[rows, 0:512]
        gl = _gelu_tanh(y)
        glu = gl * _sigmoid(_dot(gl.astype(BF16), wglu_ref[...]))
        ts_ = glu * _silu(ps_scr[rows, 512:1024])
        ys = _dot(ts_.astype(BF16), wps_ref[...])
        out = g_scr[rows, :D] * ys + ps_scr[rows, 1024:2048]
        if nb:
            msc_ref[ci * nb:(ci + 1) * nb] = out.reshape(nb, tt, D)
        else:
            per = tt // rc
            msc_ref[ci // per, (ci % per) * rc:(ci % per + 1) * rc, :] = out


def _s5conv_call(layer, x, weights, state, bt, tt):
    b, t, _ = x.shape
    r = bt * tt
    has_state = state is not None
    wspec = lambda shape: pl.BlockSpec((None,) + shape, lambda bi, ti: (layer,) + (0,) * len(shape),
                                       pipeline_mode=pl.Buffered(1))
    tile = pl.BlockSpec((bt, tt, D), lambda bi, ti: (bi, ti, 0))
    in_specs = [tile, wspec((1, D)), wspec((D, 3072)), wspec((D, 2048)), wspec((1, S5N)), wspec((1, S5N)),
                wspec((2, 256, 1024)), wspec((2, 256, 1024)), wspec((2, 1024, 256)), wspec((2, 1024, 256)),
                wspec((1, S5W)), wspec((S5W, S5W)), wspec((S5W, D)), wspec((CK, CW)), wspec((CW, D))]
    args = [x] + list(weights)
    if has_state:
        in_specs += [pl.BlockSpec((None, bt, S5N), lambda bi, ti: (layer, bi, 0)),
                     pl.BlockSpec((None, bt, S5N), lambda bi, ti: (layer, bi, 0)),
                     pl.BlockSpec((None, bt, CK - 1, CW), lambda bi, ti: (layer, bi, 0, 0))]
        args += list(state)
    out_specs = [tile,
                 pl.BlockSpec((bt, S5N), lambda bi, ti: (bi, 0)),
                 pl.BlockSpec((bt, S5N), lambda bi, ti: (bi, 0)),
                 pl.BlockSpec((bt, CK - 1, CW), lambda bi, ti: (bi, 0, 0))]
    out_shape = [jax.ShapeDtypeStruct((b, t, D), F32), jax.ShapeDtypeStruct((b, S5N), F32),
                 jax.ShapeDtypeStruct((b, S5N), F32), jax.ShapeDtypeStruct((b, CK - 1, CW), F32)]
    scratch = [pltpu.VMEM((r, D), BF16),
               pltpu.VMEM((r, 3072), F32),
               pltpu.VMEM((r, 2 * D), F32),
               pltpu.VMEM((4, bt * _pitch(tt), 128), F32),
               pltpu.VMEM((r, S5W), F32),
               pltpu.VMEM((r, 2 * S5N), F32),
               pltpu.VMEM((4, tt * _pitch(bt), 128), F32),
               pltpu.VMEM((r, S5W), F32),
               pltpu.VMEM((bt, tt + 8, CW), F32)]
    return pl.pallas_call(
        functools.partial(_s5conv_kernel, has_state, bt, tt),
        grid=(b // bt, t // tt),
        in_specs=in_specs, out_specs=out_specs, out_shape=out_shape, scratch_shapes=scratch,
        compiler_params=pltpu.CompilerParams(dimension_semantics=("arbitrary", "arbitrary"),
                                             vmem_limit_bytes=VMEM_LIMIT),
        name="s5conv_state" if has_state else "s5conv",
    )(*args)


def _seg_scan(x, pos, seg, op, fill):
    s = 1
    while s < seg:
        x = op(x, jnp.where(pos >= s, pltpu.roll(x, s, 1), fill))
        s *= 2
    return x


def _seg_bcast_last(x, pos, seg):
    n = x.shape[1]
    s = 1
    while s < seg:
        x = jnp.where(pos + s <= seg - 1, pltpu.roll(x, n - s, 1), x)
        s *= 2
    return x


def _mlstm_kernel(has_state, has_alias, final, single_tile, bt, tt, nsub, *refs):
    refs = list(refs)
    x_ref, msc_ref, nw_ref = refs[:3]
    wm_refs = refs[3:8]
    wif_ref, bias_ref, wgm_ref, hnw_ref, wpm_ref, wout_ref, fnw_ref = refs[8:15]
    refs = refs[15:]
    if has_state:
        c0_ref, n0_ref, m0_ref = refs[:3]
        refs = refs[3:]
    if has_alias:
        refs = refs[1:]
    y_ref, c_ref, n_ref, m_ref = refs[:4]
    h_scr, p_scr, hm_scr, sg_scr = refs[4:]

    r = bt * tt
    ro = r * nsub
    rp = max(r, 128)
    step = pl.program_id(1)

    def init_state():
        if has_state:
            c_ref[...] = c0_ref[...]
            n_ref[...] = n0_ref[...]
            m_ref[...] = m0_ref[...]
        else:
            c_ref[...] = jnp.zeros_like(c_ref)
            n_ref[...] = jnp.zeros_like(n_ref)
            m_ref[...] = jnp.zeros_like(m_ref)

    def normalise():
        h_scr[...] = _rmsnorm(x_ref[...].reshape(ro, D), nw_ref[...]).astype(BF16)

    def project():
        hh = h_scr[...]
        for k, w_ref in enumerate(wm_refs):
            p_scr[:, k * D:(k + 1) * D] = _dot(hh, w_ref[...])
        if nsub > 1:
            for piece in range(NH):
                merge_gate(piece)

    def merge_gate(piece):
        cols = slice(piece * DH, (piece + 1) * DH)
        sg_scr[:, cols] = _sigmoid(_dot(h_scr[...], wgm_ref[:, cols]))

    if single_tile and has_state:
        c_prev, n_prev, m_prev = c0_ref, n0_ref, m0_ref
    else:
        pl.when(step == 0)(init_state) if not single_tile else init_state()
        c_prev, n_prev, m_prev = c_ref, n_ref, m_ref

    if nsub == 1:
        base = 0
        normalise()
        h = h_scr[...]
    else:
        base = pl.multiple_of(step * r, r)
        pl.when(step == 0)(normalise)
        h = h_scr[pl.ds(base, r), :]

    hp = h if rp == r else jnp.concatenate([h, jnp.zeros((rp - r, D), BF16)], axis=0)
    gt = _dot_nt(wif_ref[...], hp) + bias_ref[...]
    if nsub == 1:
        project()
    else:
        pl.when(step == 0)(project)
    lane = lax.broadcasted_iota(jnp.int32, (8, rp), 1)
    pos = jnp.bitwise_and(lane, tt - 1)
    li = gt[0:8, :]
    lf = _log_sigmoid(gt[8:16, :])
    bcum = _seg_scan(lf, pos, tt, jnp.add, 0.0)
    g = li - bcum
    mprev = m_prev[...]
    mrun = jnp.maximum(mprev, _seg_scan(g, pos, tt, jnp.maximum, NEG))
    mlast = _seg_bcast_last(mrun, pos, tt)
    winter = jnp.exp(mprev - mrun)
    efloor = jnp.exp(-(bcum + mrun))
    wrow = jnp.exp(g - mlast)
    decay = jnp.exp(mprev - mlast)
    m_ref[...] = _seg_bcast_last(bcum, pos, tt) + mlast
    zt = jnp.concatenate([mrun, winter, efloor, wrow], axis=0).T

    if tt <= SHORT_SEQ:
        nst = NH * r
        prow = slice(0, r) if nsub == 1 else pl.ds(base, r)
        stack = lambda off: jnp.concatenate(
            [p_scr[prow, off + hd * DH:off + (hd + 1) * DH] for hd in range(NH)], axis=0)
        qs = stack(0)
        ks = stack(D) * K_SCALE
        vs = stack(2 * D)
        col = lambda c0: jnp.concatenate([zt[0:r, c0 + hd:c0 + hd + 1] for hd in range(NH)], axis=0)
        mcol, wi, ef, wc = col(0), col(8), col(16), col(24)
        grow = jnp.concatenate([g[hd:hd + 1, 0:r] for hd in range(NH)], axis=1)
        row_i = lax.broadcasted_iota(jnp.int32, (nst, nst), 0)
        col_i = lax.broadcasted_iota(jnp.int32, (nst, nst), 1)
        shift = tt.bit_length() - 1
        same = lax.shift_right_logical(row_i, shift) == lax.shift_right_logical(col_i, shift)
        arg = jnp.where(col_i <= row_i, grow - mcol, NEG)
        dm = jnp.exp(jnp.where(same, arg, NEG))
        sc = _dot_nt(qs.astype(BF16), ks.astype(BF16)) * dm
        intra = _dot(sc.astype(BF16), vs.astype(BF16))
        pairs = [(hd, b) for hd in range(NH) for b in range(bt)]
        blk = lambda hd, b: slice(hd * r + b * tt, hd * r + (b + 1) * tt)
        inter = jnp.concatenate([_dot(qs[blk(hd, b)].astype(BF16), c_prev[b, hd].astype(BF16))
                                 for hd, b in pairs], axis=0)
        nfull = jnp.concatenate([jnp.broadcast_to(n_prev[b, hd:hd + 1, :], (tt, DH)) for hd, b in pairs], axis=0)
        num = wi * inter + intra
        den = wi * jnp.sum(qs * nfull, axis=-1, keepdims=True) + jnp.sum(sc, axis=-1, keepdims=True)
        hh = num * (1.0 / jnp.maximum(jnp.abs(den), ef))
        for hd in range(NH):
            hm_scr[prow, hd * DH:(hd + 1) * DH] = hh[hd * r:(hd + 1) * r]
        wk = ks * wc
        for hd, b in pairs:
            dec = decay[hd:hd + 1, b * tt:b * tt + 1]
            upd = _dot_tn(wk[blk(hd, b)].astype(BF16), vs[blk(hd, b)].astype(BF16))
            c_ref[b, hd] = dec * c_prev[b, hd] + upd
            n_ref[b, hd:hd + 1, :] = (dec * n_prev[b, hd:hd + 1, :]
                                      + jnp.sum(wk[blk(hd, b)], axis=0, keepdims=True))
        if nsub == 1:
            for piece in range(NH):
                merge_gate(piece)

    row_i = lax.broadcasted_iota(jnp.int32, (tt, tt), 0)
    col_i = lax.broadcasted_iota(jnp.int32, (tt, tt), 1)
    causal = row_i >= col_i

    for b in range(bt if tt > SHORT_SEQ else 0):
        rows = slice(b * tt, (b + 1) * tt)
        prow = rows if nsub == 1 else pl.ds(base + b * tt, tt)
        for hd in range(NH):
            cs = slice(hd * DH, (hd + 1) * DH)
            qf = p_scr[prow, cs]
            kf = p_scr[prow, D + hd * DH:D + (hd + 1) * DH] * K_SCALE
            vb = p_scr[prow, 2 * D + hd * DH:2 * D + (hd + 1) * DH].astype(BF16)
            qb = qf.astype(BF16)
            kb = kf.astype(BF16)
            mcol = zt[rows, hd:hd + 1]
            wi = zt[rows, 8 + hd:9 + hd]
            ef = zt[rows, 16 + hd:17 + hd]
            wc = zt[rows, 24 + hd:25 + hd]
            grow = g[hd:hd + 1, b * tt:(b + 1) * tt]
            dm = jnp.exp(jnp.where(causal, grow - mcol, NEG))
            sc = _dot_nt(qb, kb) * dm
            cf = c_prev[b, hd]
            nrow = n_prev[b, hd:hd + 1, :]
            num = wi * _dot(qb, cf.astype(BF16)) + _dot(sc.astype(BF16), vb)
            den = wi * jnp.sum(qf * nrow, axis=-1, keepdims=True) + jnp.sum(sc, axis=-1, keepdims=True)
            hm_scr[prow, cs] = num * (1.0 / jnp.maximum(jnp.abs(den), ef))
            dec = decay[hd:hd + 1, b * tt:b * tt + 1]
            wk = kf * wc
            c_ref[b, hd] = dec * cf + _dot_tn(wk.astype(BF16), vb)
            n_ref[b, hd:hd + 1, :] = dec * nrow + jnp.sum(wk, axis=0, keepdims=True)
            done = b * NH + hd + 1
            if nsub == 1 and (done * NH) % (bt * NH) == 0:
                merge_gate(done * NH // (bt * NH) - 1)

    def epilogue():
        rc = min(ROW_CHUNK, ro)
        hnw = hnw_ref[...]
        for ci in range(ro // rc):
            rows = slice(ci * rc, (ci + 1) * rc)
            hmv = hm_scr[rows, :] * _sigmoid(p_scr[rows, 3 * D:4 * D])
            parts = []
            for hd in range(NH):
                hh = hmv[:, hd * DH:(hd + 1) * DH]
                mu = jnp.mean(hh, axis=-1, keepdims=True)
                dv = hh - mu
                var = jnp.mean(dv * dv, axis=-1, keepdims=True)
                parts.append(dv * lax.rsqrt(var + EPS) * hnw[:, hd * DH:(hd + 1) * DH])
            hn = jnp.concatenate(parts, axis=1)
            tm = hn * _silu(p_scr[rows, 4 * D:5 * D])
            ym = _dot(tm.astype(BF16), wpm_ref[...])
            nb = rc // tt if rc >= tt else 0
            if nb:
                msc = msc_ref[ci * nb:(ci + 1) * nb].reshape(rc, D)
                xr = x_ref[ci * nb:(ci + 1) * nb].reshape(rc, D)
            else:
                per = tt // rc
                msc = msc_ref[ci // per, (ci % per) * rc:(ci % per + 1) * rc, :]
                xr = x_ref[ci // per, (ci % per) * rc:(ci % per + 1) * rc, :]
            merged = msc + sg_scr[rows, :] * ym
            y = xr + _dot(merged.astype(BF16), wout_ref[...])
            if final:
                y = _rmsnorm(y, fnw_ref[...])
            if nb:
                y_ref[ci * nb:(ci + 1) * nb] = y.reshape(nb, tt, D)
            else:
                y_ref[ci // per, (ci % per) * rc:(ci % per + 1) * rc, :] = y

    if nsub == 1:
        epilogue()
    else:
        pl.when(step == nsub - 1)(epilogue)


def _mlstm_call(layer, depth, x, msc, weights, state, c_stack, bt, tt):
    b, t, _ = x.shape
    r = bt * tt
    rp = max(r, 128)
    nbt = b // bt
    has_state = state is not None
    has_alias = c_stack is not None
    final = layer == depth - 1
    nsub = max(1, min(nbt, PROJ_ROWS // r)) if t == tt else 1
    assert nbt % nsub == 0
    ro = r * nsub
    if nsub == 1:
        grid = (nbt, t // tt)
        tile_idx = lambda bi, si: (bi, si, 0)
        sub_idx = lambda bi, si: bi
    else:
        grid = (nbt // nsub, nsub)
        tile_idx = lambda bi, si: (bi, 0, 0)
        sub_idx = lambda bi, si: bi * nsub + si
    wspec = lambda shape: pl.BlockSpec((None,) + shape, lambda bi, si: (layer,) + (0,) * len(shape),
                                       pipeline_mode=pl.Buffered(1))
    tile = pl.BlockSpec((bt * nsub, tt, D), tile_idx)
    wcol = lambda k: pl.BlockSpec((None, D, D), lambda bi, si: (layer, 0, k), pipeline_mode=pl.Buffered(1))
    nw, w_main, w_if, bias, w_gates, hnw, wpm, wout, fnw = weights
    in_specs = ([tile, tile, wspec((1, D))] + [wcol(OFF_M // D + k) for k in range(5)]
                + [wspec((16, D)), wspec((16, 1)), wcol(2), wspec((1, D)), wspec((D, D)), wspec((D, D)),
                   pl.BlockSpec((1, D), lambda bi, si: (0, 0), pipeline_mode=pl.Buffered(1))])
    args = [x, msc, nw] + [w_main] * 5 + [w_if, bias, w_gates, hnw, wpm, wout, fnw]
    if has_state:
        in_specs += [pl.BlockSpec((None, bt, NH, DH, DH), lambda bi, si: (layer, sub_idx(bi, si), 0, 0, 0)),
                     pl.BlockSpec((None, bt, NH, DH), lambda bi, si: (layer, sub_idx(bi, si), 0, 0)),
                     pl.BlockSpec((None, None, 8, rp), lambda bi, si: (layer, sub_idx(bi, si), 0, 0))]
        args += list(state)
    aliases = {}
    if has_alias:
        aliases = {len(args): 1}
        in_specs.append(pl.BlockSpec(memory_space=pl.ANY))
        args.append(c_stack)
    out_specs = [tile,
                 pl.BlockSpec((None, bt, NH, DH, DH), lambda bi, si: (layer, sub_idx(bi, si), 0, 0, 0)),
                 pl.BlockSpec((bt, NH, DH), lambda bi, si: (sub_idx(bi, si), 0, 0)),
                 pl.BlockSpec((None, 8, rp), lambda bi, si: (sub_idx(bi, si), 0, 0))]
    out_shape = [jax.ShapeDtypeStruct((b, t, D), F32), jax.ShapeDtypeStruct((depth, b, NH, DH, DH), F32),
                 jax.ShapeDtypeStruct((b, NH, DH), F32), jax.ShapeDtypeStruct((nbt, 8, rp), F32)]
    scratch = [pltpu.VMEM((ro, D), BF16),
               pltpu.VMEM((ro, 5 * D), F32),
               pltpu.VMEM((ro, D), F32),
               pltpu.VMEM((ro, D), F32)]
    return pl.pallas_call(
        functools.partial(_mlstm_kernel, has_state, has_alias, final, t == tt, bt, tt, nsub),
        grid=grid,
        in_specs=in_specs, out_specs=out_specs, out_shape=out_shape, scratch_shapes=scratch,
        input_output_aliases=aliases,
        compiler_params=pltpu.CompilerParams(dimension_semantics=("arbitrary", "arbitrary"),
                                             vmem_limit_bytes=VMEM_LIMIT),
        name="mlstm_state" if has_state else "mlstm",
    )(*args)


def _expand_m(m, bt, tt):
    depth, b, _ = m.shape
    r = bt * tt
    rp = max(r, 128)
    v = jnp.repeat(m.reshape(depth, b // bt, bt, NH).transpose(0, 1, 3, 2), tt, axis=-1)
    return jnp.pad(v, ((0, 0), (0, 0), (0, 8 - NH), (0, rp - r)))


def _collapse_m(mrows, bt, tt):
    nbt = mrows.shape[0]
    return mrows[:, :NH, 0:bt * tt:tt].transpose(0, 2, 1).reshape(nbt * bt, NH)


def kernel(x_prompt, x_sample, state_ssm_re, state_ssm_im, state_conv, state_mlstm_c, state_mlstm_n, state_mlstm_m, norm_w, w_in, i_bias, f_bias, lam_re, lam_im, log_dt, b_re, b_im, c_re, c_im, d_skip, w_glu, w_proj_s, conv_w, w_proj_c, mlstm_norm_w, w_proj_m, w_out, final_norm_w):
    depth = norm_w.shape[0]
    bp, tp, _ = x_prompt.shape
    bs, ts, _ = x_sample.shape
    assert tp % PROMPT_CHUNK == 0 and ts % 8 == 0 and ts & (ts - 1) == 0 and bp % 8 == 0 and bs % 64 == 0

    a_re, a_im, bb_re, bb_im, cp_re, cp_im = _s5_discretise(lam_re, lam_im, log_dt, b_re, b_im, c_re, c_im)
    row = lambda a: a.reshape(depth, 1, a.shape[-1])
    nw = row(norm_w)
    w_main = _cast_main_columns(w_in, OFF_IF // D)
    w_gates = w_in[:, :, OFF_G:IN_W].astype(BF16)
    gate_rows = lambda a: jnp.pad(a.reshape(depth, 2, NH, -1), ((0, 0), (0, 0), (0, 8 - NH), (0, 0))).reshape(depth, 16, -1)
    w_if = gate_rows(jnp.swapaxes(w_in[:, :, OFF_IF:OFF_G], 1, 2)).astype(BF16)
    bias = gate_rows(jnp.concatenate([i_bias, f_bias], axis=-1)[:, :, None])
    wa = [nw, w_main, w_gates, a_re, a_im, bb_re, bb_im, cp_re, cp_im, row(d_skip),
          w_glu.astype(BF16), w_proj_s.astype(BF16), conv_w, w_proj_c.astype(BF16)]
    wb = [nw, w_main, w_if, bias, w_gates, row(mlstm_norm_w), w_proj_m.astype(BF16), w_out.astype(BF16),
          final_norm_w.reshape(1, D)]

    sb_a, sb_b = 64, 4
    pb_a, pb_b = 8, 2
    st_a = (state_ssm_re.reshape(depth, bs, S5N), state_ssm_im.reshape(depth, bs, S5N), state_conv)
    st_b = (state_mlstm_c, state_mlstm_n, _expand_m(state_mlstm_m, sb_b, ts))

    yp, ys = x_prompt, x_sample
    cp = cs = None
    outs_p = [[] for _ in range(5)]
    outs_s = [[] for _ in range(5)]
    for l in range(depth):
        msc, sre, sim, cv = _s5conv_call(l, yp, wa, None, pb_a, S5_TILE_T)
        yp, cp, nn, mm = _mlstm_call(l, depth, yp, msc, wb, None, cp, pb_b, PROMPT_CHUNK)
        for i, v in enumerate((sre.reshape(bp, S5G, S5P), sim.reshape(bp, S5G, S5P), cv, nn,
                               _collapse_m(mm, pb_b, PROMPT_CHUNK))):
            outs_p[i].append(v)
        msc, sre, sim, cv = _s5conv_call(l, ys, wa, st_a, sb_a, ts)
        ys, cs, nn, mm = _mlstm_call(l, depth, ys, msc, wb, st_b, cs, sb_b, ts)
        for i, v in enumerate((sre.reshape(bs, S5G, S5P), sim.reshape(bs, S5G, S5P), cv, nn,
                               _collapse_m(mm, sb_b, ts))):
            outs_s[i].append(v)
    sp = [jnp.stack(o) for o in outs_p]
    ss = [jnp.stack(o) for o in outs_s]
    return (yp, ys, sp[0], ss[0], sp[1], ss[1], sp[2], ss[2], cp, cs, sp[3], ss[3], sp[4], ss[4])
```

```python
import functools
import math

import jax
import jax.numpy as jnp
from jax import lax
from jax.experimental import pallas as pl
from jax.experimental.pallas import tpu as pltpu

D = 1024
S5W = 512
S5G = 32
S5P = 64
S5C = 16
S5N = S5G * S5P
CW = 512
CK = 3
NH = 4
DH = 256
PROMPT_CHUNK = 256
S5_TILE_T = 64
SHORT_SEQ = 32
EPS = 1e-6
NEG = -1e30
K_SCALE = DH ** -0.5

OFF_S = 0
OFF_M = 3072
OFF_IF = 8192
OFF_G = 8200
IN_W = 11272

F32 = jnp.float32
BF16 = jnp.bfloat16
VMEM_LIMIT = 58 * 1024 * 1024
ROW_CHUNK = 256
PROJ_ROWS = 256


def _sigmoid(x):
    return 1.0 / (1.0 + jnp.exp(-x))


def _silu(x):
    return x * _sigmoid(x)


def _gelu_tanh(x):
    return 0.5 * x * (1.0 + jnp.tanh(math.sqrt(2.0 / math.pi) * (x + 0.044715 * (x * x * x))))


def _log_sigmoid(x):
    return jnp.minimum(x, 0.0) - jnp.log1p(jnp.exp(-jnp.abs(x)))


def _rmsnorm(x, w):
    return x * lax.rsqrt(jnp.mean(x * x, axis=-1, keepdims=True) + EPS) * w


def _dot(a, b):
    return jnp.dot(a, b, preferred_element_type=F32)


def _dot_nt(a, b):
    return lax.dot_general(a, b, (((1,), (1,)), ((), ())), preferred_element_type=F32)


def _dot_tn(a, b):
    return lax.dot_general(a, b, (((0,), (0,)), ((), ())), preferred_element_type=F32)


def _pitch(n):
    p = n + 8
    return p if (p // 8) % 2 == 1 else p + 8


def _s5_disc_kernel(lre_ref, lim_ref, ldt_ref, bre_ref, bim_ref, cre_ref, cim_ref,
                    are_ref, aim_ref, bbre_ref, bbim_ref, cpre_ref, cpim_ref):
    lr = lre_ref[...]
    li = lim_ref[...]
    dt = jnp.exp(ldt_ref[...])
    ea = jnp.exp(lr * dt)
    ar = ea * jnp.cos(li * dt)
    ai = ea * jnp.sin(li * dt)
    are_ref[...] = ar
    aim_ref[...] = ai
    nr = ar - 1.0
    inv = 1.0 / (lr * lr + li * li)
    cr = (nr * lr + ai * li) * inv
    ci = (ai * lr - nr * li) * inv
    half = S5N // 2
    shr = lax.shift_right_logical
    bdiag = (shr(lax.broadcasted_iota(jnp.int32, (256, half), 0), 4)
             == shr(lax.broadcasted_iota(jnp.int32, (256, half), 1), 6))
    cdiag = (shr(lax.broadcasted_iota(jnp.int32, (half, 256), 0), 6)
             == shr(lax.broadcasted_iota(jnp.int32, (half, 256), 1), 4))
    for h in range(2):
        crh = cr[:, h * half:(h + 1) * half]
        cih = ci[:, h * half:(h + 1) * half]
        br = bre_ref[h]
        bi = bim_ref[h]
        bbre_ref[h] = jnp.where(bdiag, crh * br - cih * bi, 0.0).astype(BF16)
        bbim_ref[h] = jnp.where(bdiag, crh * bi + cih * br, 0.0).astype(BF16)
        cpre_ref[h] = jnp.where(cdiag, cre_ref[h], 0.0).astype(BF16)
        cpim_ref[h] = jnp.where(cdiag, cim_ref[h], 0.0).astype(BF16)


def _s5_discretise(lam_re, lam_im, log_dt, b_re, b_im, c_re, c_im):
    depth = lam_re.shape[0]

    def tile_b(b):
        b = jnp.swapaxes(b.reshape(depth, 2, 16, S5P, S5C), 3, 4).reshape(depth, 2, 256, 1, S5P)
        return jnp.broadcast_to(b, (depth, 2, 256, 16, S5P)).reshape(depth, 2, 256, 1024)

    def tile_c(c):
        c = jnp.swapaxes(c.reshape(depth, 2, 16, S5C, S5P), 3, 4).reshape(depth, 2, 1024, 1, S5C)
        return jnp.broadcast_to(c, (depth, 2, 1024, 16, S5C)).reshape(depth, 2, 1024, 256)

    row = lambda a: a.reshape(depth, 1, S5N)
    ldt = jnp.broadcast_to(log_dt[:, :, None], (depth, S5G, S5P))
    vec = pl.BlockSpec((None, 1, S5N), lambda l: (l, 0, 0))
    bmat = pl.BlockSpec((None, 2, 256, 1024), lambda l: (l, 0, 0, 0))
    cmat = pl.BlockSpec((None, 2, 1024, 256), lambda l: (l, 0, 0, 0))
    return pl.pallas_call(
        _s5_disc_kernel,
        grid=(depth,),
        in_specs=[vec, vec, vec, bmat, bmat, cmat, cmat],
        out_specs=[vec, vec, bmat, bmat, cmat, cmat],
        out_shape=[jax.ShapeDtypeStruct((depth, 1, S5N), F32), jax.ShapeDtypeStruct((depth, 1, S5N), F32),
                   jax.ShapeDtypeStruct((depth, 2, 256, 1024), BF16),
                   jax.ShapeDtypeStruct((depth, 2, 256, 1024), BF16),
                   jax.ShapeDtypeStruct((depth, 2, 1024, 256), BF16),
                   jax.ShapeDtypeStruct((depth, 2, 1024, 256), BF16)],
        name="s5_discretise",
    )(row(lam_re), row(lam_im), row(ldt), tile_b(b_re), tile_b(b_im), tile_c(c_re), tile_c(c_im))


def _cast_kernel(shift, *refs):
    if shift:
        wa_ref, wb_ref, o_ref = refs
        w = jnp.concatenate([wa_ref[shift:, :], wb_ref[:shift, :]], axis=0)
    else:
        w_ref, o_ref = refs
        w = w_ref[...]
    o_ref[...] = w.T.astype(BF16)


def _cast_columns(w_t, first_col, n_blocks):
    depth = w_t.shape[0]
    k0, shift = divmod(first_col, D)
    assert shift % 8 == 0
    in_specs = [pl.BlockSpec((None, D, D), lambda l, k: (l, k0 + k, 0))]
    args = [w_t]
    if shift:
        in_specs.append(pl.BlockSpec((None, D, D), lambda l, k: (l, k0 + k + 1, 0)))
        args.append(w_t)
    return pl.pallas_call(
        functools.partial(_cast_kernel, shift),
        grid=(depth, n_blocks),
        in_specs=in_specs,
        out_specs=pl.BlockSpec((None, D, D), lambda l, k: (l, 0, k)),
        out_shape=jax.ShapeDtypeStruct((depth, D, n_blocks * D), BF16),
        name="cast_weights",
    )(*args)


def _s5conv_kernel(has_state, bt, tt, *refs):
    refs = list(refs)
    (x_ref, nw_ref, ws_ref, wg_ref, are_ref, aim_ref, bbre_ref, bbim_ref, cre_ref, cim_ref,
     dsk_ref, wglu_ref, wps_ref, cw_ref, wpc_ref) = refs[:15]
    refs = refs[15:]
    if has_state:
        s0re_ref, s0im_ref, cv0_ref = refs[:3]
        refs = refs[3:]
    msc_ref, sre_ref, sim_ref, cvo_ref = refs[:4]
    h_scr, ps_scr, g_scr, uslab, utb, bu_scr, yslab, y_scr, vhalo = refs[4:]

    r = bt * tt
    pt = _pitch(tt)
    pb = _pitch(bt)
    rc = min(ROW_CHUNK, r)
    nchunk = r // rc
    ti = pl.program_id(1)

    def merge_gates(ci):
        rows = slice(ci * rc, (ci + 1) * rc)
        g_scr[rows, :] = _sigmoid(_dot(h_scr[rows, :], wg_ref[...]))

    @pl.when(ti == 0)
    def _init():
        if has_state:
            sre_ref[...] = s0re_ref[...]
            sim_ref[...] = s0im_ref[...]
            vhalo[:, 6:8, :] = cv0_ref[...]
        else:
            sre_ref[...] = jnp.zeros_like(sre_ref)
            sim_ref[...] = jnp.zeros_like(sim_ref)
            vhalo[:, 6:8, :] = jnp.zeros((bt, 2, CW), F32)

    x = x_ref[...].reshape(r, D)
    h = _rmsnorm(x, nw_ref[...]).astype(BF16)
    h_scr[...] = h
    ps_scr[...] = _dot(h, ws_ref[...])


    for b in range(bt):
        for j in range(4):
            uslab[j, b * pt:b * pt + tt, :] = ps_scr[b * tt:(b + 1) * tt, j * 128:(j + 1) * 128]
    for t in range(tt):
        for bs in range(bt // 8):
            dst = t * bt + bs * 8
            for j in range(4):
                utb[dst:dst + 8, j * 128:(j + 1) * 128] = uslab[j, pl.ds(bs * 8 * pt + t, 8, stride=pt), :]
    for ci in range(nchunk // 2):
        merge_gates(ci)

    for hf in range(2):
        uh = utb[:, hf * 256:(hf + 1) * 256].astype(BF16)
        bu_scr[:, hf * 1024:(hf + 1) * 1024] = _dot(uh, bbre_ref[hf])
        bu_scr[:, S5N + hf * 1024:S5N + (hf + 1) * 1024] = _dot(uh, bbim_ref[hf])
    for ci in range(nchunk // 2, nchunk):
        merge_gates(ci)

    v = ps_scr[:, 1536:2048] * ps_scr[:, 2048:2560]
    vhalo[:, 8:8 + tt, :] = v.reshape(bt, tt, CW)
    cw = cw_ref[...]
    yc = (cw[0:1, :] * vhalo[:, 6:6 + tt, :] + cw[1:2, :] * vhalo[:, 7:7 + tt, :]
          + cw[2:3, :] * vhalo[:, 8:8 + tt, :])
    ps_scr[:, 2048:2560] = yc.reshape(r, CW)
    new_halo = vhalo[:, tt + 6:tt + 8, :]
    vhalo[:, 6:8, :] = new_halo
    cvo_ref[...] = new_halo

    for ci in range(nchunk):
        rows = slice(ci * rc, (ci + 1) * rc)
        tc_ = ps_scr[rows, 1024:1536] * ps_scr[rows, 2048:2560] * _silu(ps_scr[rows, 2560:3072])
        ps_scr[rows, 1024:2048] = g_scr[rows, D:] * _dot(tc_.astype(BF16), wpc_ref[...])

    for q in range(4):
        c_re = slice(q * 512, (q + 1) * 512)
        c_im = slice(S5N + q * 512, S5N + (q + 1) * 512)
        ar = jnp.broadcast_to(are_ref[:, c_re], (8, 512))
        ai = jnp.broadcast_to(aim_ref[:, c_re], (8, 512))
        for bs in range(bt // 8):
            r0 = bs * 8
            sr = sre_ref[r0:r0 + 8, c_re]
            si = sim_ref[r0:r0 + 8, c_re]
            for t in range(tt):
                row = t * bt + r0
                nr = ar * sr - ai * si + bu_scr[row:row + 8, c_re]
                ni = ar * si + ai * sr + bu_scr[row:row + 8, c_im]
                bu_scr[row:row + 8, c_re] = nr
                bu_scr[row:row + 8, c_im] = ni
                sr, si = nr, ni
            sre_ref[r0:r0 + 8, c_re] = sr
            sim_ref[r0:r0 + 8, c_re] = si

    for m in range(2):
        sre = bu_scr[:, m * 1024:(m + 1) * 1024].astype(BF16)
        sim = bu_scr[:, S5N + m * 1024:S5N + (m + 1) * 1024].astype(BF16)
        ytb = _dot(sre, cre_ref[m]) - _dot(sim, cim_ref[m])
        for t in range(tt):
            for jj in range(2):
                yslab[2 * m + jj, t * pb:t * pb + bt, :] = ytb[t * bt:(t + 1) * bt, jj * 128:(jj + 1) * 128]
    for b in range(bt):
        for ts in range(tt // 8):
            dst = b * tt + ts * 8
            for j in range(4):
                y_scr[dst:dst + 8, j * 128:(j + 1) * 128] = yslab[j, pl.ds(ts * 8 * pb + b, 8, stride=pb), :]

    nb = rc // tt if rc >= tt else 0
    dsk = dsk_ref[...]
    for ci in range(nchunk):
        rows = slice(ci * rc, (ci + 1) * rc)
        y = y_scr[rows, :] + dsk * ps_scr[rows, 0:512]
        gl = _gelu_tanh(y)
        glu = gl * _sigmoid(_dot(gl.astype(BF16), wglu_ref[...]))
        ts_ = glu * _silu(ps_scr[rows, 512:1024])
        ys = _dot(ts_.astype(BF16), wps_ref[...])
        out = g_scr[rows, :D] * ys + ps_scr[rows, 1024:2048]
        if nb:
            msc_ref[ci * nb:(ci + 1) * nb] = out.reshape(nb, tt, D)
        else:
            per = tt // rc
            msc_ref[ci // per, (ci % per) * rc:(ci % per + 1) * rc, :] = out


def _s5conv_call(layer, x, weights, state, bt, tt):
    b, t, _ = x.shape
    r = bt * tt
    has_state = state is not None
    wspec = lambda shape: pl.BlockSpec((None,) + shape, lambda bi, ti: (layer,) + (0,) * len(shape),
                                       pipeline_mode=pl.Buffered(1))
    tile = pl.BlockSpec((bt, tt, D), lambda bi, ti: (bi, ti, 0))
    in_specs = [tile, wspec((1, D)), wspec((D, 3072)), wspec((D, 2048)), wspec((1, S5N)), wspec((1, S5N)),
                wspec((2, 256, 1024)), wspec((2, 256, 1024)), wspec((2, 1024, 256)), wspec((2, 1024, 256)),
                wspec((1, S5W)), wspec((S5W, S5W)), wspec((S5W, D)), wspec((CK, CW)), wspec((CW, D))]
    args = [x] + list(weights)
    if has_state:
        in_specs += [pl.BlockSpec((None, bt, S5N), lambda bi, ti: (layer, bi, 0)),
                     pl.BlockSpec((None, bt, S5N), lambda bi, ti: (layer, bi, 0)),
                     pl.BlockSpec((None, bt, CK - 1, CW), lambda bi, ti: (layer, bi, 0, 0))]
        args += list(state)
    out_specs = [tile,
                 pl.BlockSpec((bt, S5N), lambda bi, ti: (bi, 0)),
                 pl.BlockSpec((bt, S5N), lambda bi, ti: (bi, 0)),
                 pl.BlockSpec((bt, CK - 1, CW), lambda bi, ti: (bi, 0, 0))]
    out_shape = [jax.ShapeDtypeStruct((b, t, D), F32), jax.ShapeDtypeStruct((b, S5N), F32),
                 jax.ShapeDtypeStruct((b, S5N), F32), jax.ShapeDtypeStruct((b, CK - 1, CW), F32)]
    scratch = [pltpu.VMEM((r, D), BF16),
               pltpu.VMEM((r, 3072), F32),
               pltpu.VMEM((r, 2 * D), F32),
               pltpu.VMEM((4, bt * _pitch(tt), 128), F32),
               pltpu.VMEM((r, S5W), F32),
               pltpu.VMEM((r, 2 * S5N), F32),
               pltpu.VMEM((4, tt * _pitch(bt), 128), F32),
               pltpu.VMEM((r, S5W), F32),
               pltpu.VMEM((bt, tt + 8, CW), F32)]
    return pl.pallas_call(
        functools.partial(_s5conv_kernel, has_state, bt, tt),
        grid=(b // bt, t // tt),
        in_specs=in_specs, out_specs=out_specs, out_shape=out_shape, scratch_shapes=scratch,
        compiler_params=pltpu.CompilerParams(dimension_semantics=("arbitrary", "arbitrary"),
                                             vmem_limit_bytes=VMEM_LIMIT),
        name="s5conv_state" if has_state else "s5conv",
    )(*args)


def _seg_scan(x, pos, seg, op, fill):
    s = 1
    while s < seg:
        x = op(x, jnp.where(pos >= s, pltpu.roll(x, s, 1), fill))
        s *= 2
    return x


def _seg_bcast_last(x, pos, seg):
    n = x.shape[1]
    s = 1
    while s < seg:
        x = jnp.where(pos + s <= seg - 1, pltpu.roll(x, n - s, 1), x)
        s *= 2
    return x


def _mlstm_kernel(has_state, has_alias, final, single_tile, bt, tt, nsub, *refs):
    refs = list(refs)
    x_ref, msc_ref, nw_ref = refs[:3]
    wm_refs = refs[3:8]
    wif_ref, bias_ref, wgm_ref, hnw_ref, wpm_ref, wout_ref, fnw_ref = refs[8:15]
    refs = refs[15:]
    if has_state:
        c0_ref, n0_ref, m0_ref = refs[:3]
        refs = refs[3:]
    if has_alias:
        refs = refs[1:]
    y_ref, c_ref, n_ref, m_ref = refs[:4]
    h_scr, p_scr, hm_scr, sg_scr = refs[4:]

    r = bt * tt
    ro = r * nsub
    rp = max(r, 128)
    step = pl.program_id(1)

    def init_state():
        if has_state:
            c_ref[...] = c0_ref[...]
            n_ref[...] = n0_ref[...]
            m_ref[...] = m0_ref[...]
        else:
            c_ref[...] = jnp.zeros_like(c_ref)
            n_ref[...] = jnp.zeros_like(n_ref)
            m_ref[...] = jnp.zeros_like(m_ref)

    def normalise():
        h_scr[...] = _rmsnorm(x_ref[...].reshape(ro, D), nw_ref[...]).astype(BF16)

    def project():
        hh = h_scr[...]
        for k, w_ref in enumerate(wm_refs):
            p_scr[:, k * D:(k + 1) * D] = _dot(hh, w_ref[...])
        if nsub > 1:
            for piece in range(NH):
                merge_gate(piece)

    def merge_gate(piece):
        cols = slice(piece * DH, (piece + 1) * DH)
        sg_scr[:, cols] = _sigmoid(_dot(h_scr[...], wgm_ref[:, cols]))

    if single_tile and has_state:
        c_prev, n_prev, m_prev = c0_ref, n0_ref, m0_ref
    else:
        pl.when(step == 0)(init_state) if not single_tile else init_state()
        c_prev, n_prev, m_prev = c_ref, n_ref, m_ref

    if nsub == 1:
        base = 0
        normalise()
        h = h_scr[...]
    else:
        base = pl.multiple_of(step * r, r)
        pl.when(step == 0)(normalise)
        h = h_scr[pl.ds(base, r), :]

    hp = h if rp == r else jnp.concatenate([h, jnp.zeros((rp - r, D), BF16)], axis=0)
    gt = _dot_nt(wif_ref[...], hp) + bias_ref[...]
    if nsub == 1:
        project()
    else:
        pl.when(step == 0)(project)
    lane = lax.broadcasted_iota(jnp.int32, (8, rp), 1)
    pos = jnp.bitwise_and(lane, tt - 1)
    li = gt[0:8, :]
    lf = _log_sigmoid(gt[8:16, :])
    bcum = _seg_scan(lf, pos, tt, jnp.add, 0.0)
    g = li - bcum
    mprev = m_prev[...]
    mrun = jnp.maximum(mprev, _seg_scan(g, pos, tt, jnp.maximum, NEG))
    mlast = _seg_bcast_last(mrun, pos, tt)
    winter = jnp.exp(mprev - mrun)
    efloor = jnp.exp(-(bcum + mrun))
    wrow = jnp.exp(g - mlast)
    decay = jnp.exp(mprev - mlast)
    m_ref[...] = _seg_bcast_last(bcum, pos, tt) + mlast
    zt = jnp.concatenate([mrun, winter, efloor, wrow], axis=0).T

    if tt <= SHORT_SEQ:
        nst = NH * r
        prow = slice(0, r) if nsub == 1 else pl.ds(base, r)
        stack = lambda off: jnp.concatenate(
            [p_scr[prow, off + hd * DH:off + (hd + 1) * DH] for hd in range(NH)], axis=0)
        qs = stack(0)
        ks = stack(D) * K_SCALE
        vs = stack(2 * D)
        col = lambda c0: jnp.concatenate([zt[0:r, c0 + hd:c0 + hd + 1] for hd in range(NH)], axis=0)
        mcol, wi, ef, wc = col(0), col(8), col(16), col(24)
        grow = jnp.concatenate([g[hd:hd + 1, 0:r] for hd in range(NH)], axis=1)
        row_i = lax.broadcasted_iota(jnp.int32, (nst, nst), 0)
        col_i = lax.broadcasted_iota(jnp.int32, (nst, nst), 1)
        shift = tt.bit_length() - 1
        same = lax.shift_right_logical(row_i, shift) == lax.shift_right_logical(col_i, shift)
        arg = jnp.where(col_i <= row_i, grow - mcol, NEG)
        dm = jnp.exp(jnp.where(same, arg, NEG))
        sc = _dot_nt(qs.astype(BF16), ks.astype(BF16)) * dm
        intra = _dot(sc.astype(BF16), vs.astype(BF16))
        pairs = [(hd, b) for hd in range(NH) for b in range(bt)]
        blk = lambda hd, b: slice(hd * r + b * tt, hd * r + (b + 1) * tt)
        inter = jnp.concatenate([_dot(qs[blk(hd, b)].astype(BF16), c_prev[b, hd].astype(BF16))
                                 for hd, b in pairs], axis=0)
        nfull = jnp.concatenate([jnp.broadcast_to(n_prev[b, hd:hd + 1, :], (tt, DH)) for hd, b in pairs], axis=0)
        num = wi * inter + intra
        den = wi * jnp.sum(qs * nfull, axis=-1, keepdims=True) + jnp.sum(sc, axis=-1, keepdims=True)
        hh = num * (1.0 / jnp.maximum(jnp.abs(den), ef))
        for hd in range(NH):
            hm_scr[prow, hd * DH:(hd + 1) * DH] = hh[hd * r:(hd + 1) * r]
        wk = ks * wc
        for hd, b in pairs:
            dec = decay[hd:hd + 1, b * tt:b * tt + 1]
            upd = _dot_tn(wk[blk(hd, b)].astype(BF16), vs[blk(hd, b)].astype(BF16))
            c_ref[b, hd] = dec * c_prev[b, hd] + upd
            n_ref[b, hd:hd + 1, :] = (dec * n_prev[b, hd:hd + 1, :]
                                      + jnp.sum(wk[blk(hd, b)], axis=0, keepdims=True))
        if nsub == 1:
            for piece in range(NH):
                merge_gate(piece)

    row_i = lax.broadcasted_iota(jnp.int32, (tt, tt), 0)
    col_i = lax.broadcasted_iota(jnp.int32, (tt, tt), 1)
    causal = row_i >= col_i

    for b in range(bt if tt > SHORT_SEQ else 0):
        rows = slice(b * tt, (b + 1) * tt)
        prow = rows if nsub == 1 else pl.ds(base + b * tt, tt)
        for hd in range(NH):
            cs = slice(hd * DH, (hd + 1) * DH)
            qf = p_scr[prow, cs]
            kf = p_scr[prow, D + hd * DH:D + (hd + 1) * DH] * K_SCALE
            vb = p_scr[prow, 2 * D + hd * DH:2 * D + (hd + 1) * DH].astype(BF16)
            qb = qf.astype(BF16)
            kb = kf.astype(BF16)
            mcol = zt[rows, hd:hd + 1]
            wi = zt[rows, 8 + hd:9 + hd]
            ef = zt[rows, 16 + hd:17 + hd]
            wc = zt[rows, 24 + hd:25 + hd]
            grow = g[hd:hd + 1, b * tt:(b + 1) * tt]
            dm = jnp.exp(jnp.where(causal, grow - mcol, NEG))
            sc = _dot_nt(qb, kb) * dm
            cf = c_prev[b, hd]
            nrow = n_prev[b, hd:hd + 1, :]
            num = wi * _dot(qb, cf.astype(BF16)) + _dot(sc.astype(BF16), vb)
            den = wi * jnp.sum(qf * nrow, axis=-1, keepdims=True) + jnp.sum(sc, axis=-1, keepdims=True)
            hm_scr[prow, cs] = num * (1.0 / jnp.maximum(jnp.abs(den), ef))
            dec = decay[hd:hd + 1, b * tt:b * tt + 1]
            wk = kf * wc
            c_ref[b, hd] = dec * cf + _dot_tn(wk.astype(BF16), vb)
            n_ref[b, hd:hd + 1, :] = dec * nrow + jnp.sum(wk, axis=0, keepdims=True)
            done = b * NH + hd + 1
            if nsub == 1 and (done * NH) % (bt * NH) == 0:
                merge_gate(done * NH // (bt * NH) - 1)

    def epilogue():
        rc = min(ROW_CHUNK, ro)
        hnw = hnw_ref[...]
        for ci in range(ro // rc):
            rows = slice(ci * rc, (ci + 1) * rc)
            hmv = hm_scr[rows, :] * _sigmoid(p_scr[rows, 3 * D:4 * D])
            parts = []
            for hd in range(NH):
                hh = hmv[:, hd * DH:(hd + 1) * DH]
                mu = jnp.mean(hh, axis=-1, keepdims=True)
                dv = hh - mu
                var = jnp.mean(dv * dv, axis=-1, keepdims=True)
                parts.append(dv * lax.rsqrt(var + EPS) * hnw[:, hd * DH:(hd + 1) * DH])
            hn = jnp.concatenate(parts, axis=1)
            tm = hn * _silu(p_scr[rows, 4 * D:5 * D])
            ym = _dot(tm.astype(BF16), wpm_ref[...])
            nb = rc // tt if rc >= tt else 0
            if nb:
                msc = msc_ref[ci * nb:(ci + 1) * nb].reshape(rc, D)
                xr = x_ref[ci * nb:(ci + 1) * nb].reshape(rc, D)
            else:
                per = tt // rc
                msc = msc_ref[ci // per, (ci % per) * rc:(ci % per + 1) * rc, :]
                xr = x_ref[ci // per, (ci % per) * rc:(ci % per + 1) * rc, :]
            merged = msc + sg_scr[rows, :] * ym
            y = xr + _dot(merged.astype(BF16), wout_ref[...])
            if final:
                y = _rmsnorm(y, fnw_ref[...])
            if nb:
                y_ref[ci * nb:(ci + 1) * nb] = y.reshape(nb, tt, D)
            else:
                y_ref[ci // per, (ci % per) * rc:(ci % per + 1) * rc, :] = y

    if nsub == 1:
        epilogue()
    else:
        pl.when(step == nsub - 1)(epilogue)


def _mlstm_call(layer, depth, x, msc, weights, state, c_stack, bt, tt):
    b, t, _ = x.shape
    r = bt * tt
    rp = max(r, 128)
    nbt = b // bt
    has_state = state is not None
    has_alias = c_stack is not None
    final = layer == depth - 1
    nsub = max(1, min(nbt, PROJ_ROWS // r)) if t == tt else 1
    assert nbt % nsub == 0
    ro = r * nsub
    if nsub == 1:
        grid = (nbt, t // tt)
        tile_idx = lambda bi, si: (bi, si, 0)
        sub_idx = lambda bi, si: bi
    else:
        grid = (nbt // nsub, nsub)
        tile_idx = lambda bi, si: (bi, 0, 0)
        sub_idx = lambda bi, si: bi * nsub + si
    wspec = lambda shape: pl.BlockSpec((None,) + shape, lambda bi, si: (layer,) + (0,) * len(shape),
                                       pipeline_mode=pl.Buffered(1))
    tile = pl.BlockSpec((bt * nsub, tt, D), tile_idx)
    wcol = lambda k: pl.BlockSpec((None, D, D), lambda bi, si: (layer, 0, k), pipeline_mode=pl.Buffered(1))
    nw, w_main, w_if, bias, w_gates, hnw, wpm, wout, fnw = weights
    in_specs = ([tile, tile, wspec((1, D))] + [wcol(OFF_M // D + k) for k in range(5)]
                + [wspec((16, D)), wspec((16, 1)), wcol(2), wspec((1, D)), wspec((D, D)), wspec((D, D)),
                   pl.BlockSpec((1, D), lambda bi, si: (0, 0), pipeline_mode=pl.Buffered(1))])
    args = [x, msc, nw] + [w_main] * 5 + [w_if, bias, w_gates, hnw, wpm, wout, fnw]
    if has_state:
        in_specs += [pl.BlockSpec((None, bt, NH, DH, DH), lambda bi, si: (layer, sub_idx(bi, si), 0, 0, 0)),
                     pl.BlockSpec((None, bt, NH, DH), lambda bi, si: (layer, sub_idx(bi, si), 0, 0)),
                     pl.BlockSpec((None, None, 8, rp), lambda bi, si: (layer, sub_idx(bi, si), 0, 0))]
        args += list(state)
    aliases = {}
    if has_alias:
        aliases = {len(args): 1}
        in_specs.append(pl.BlockSpec(memory_space=pl.ANY))
        args.append(c_stack)
    out_specs = [tile,
                 pl.BlockSpec((None, bt, NH, DH, DH), lambda bi, si: (layer, sub_idx(bi, si), 0, 0, 0)),
                 pl.BlockSpec((bt, NH, DH), lambda bi, si: (sub_idx(bi, si), 0, 0)),
                 pl.BlockSpec((None, 8, rp), lambda bi, si: (sub_idx(bi, si), 0, 0))]
    out_shape = [jax.ShapeDtypeStruct((b, t, D), F32), jax.ShapeDtypeStruct((depth, b, NH, DH, DH), F32),
                 jax.ShapeDtypeStruct((b, NH, DH), F32), jax.ShapeDtypeStruct((nbt, 8, rp), F32)]
    scratch = [pltpu.VMEM((ro, D), BF16),
               pltpu.VMEM((ro, 5 * D), F32),
               pltpu.VMEM((ro, D), F32),
               pltpu.VMEM((ro, D), F32)]
    return pl.pallas_call(
        functools.partial(_mlstm_kernel, has_state, has_alias, final, t == tt, bt, tt, nsub),
        grid=grid,
        in_specs=in_specs, out_specs=out_specs, out_shape=out_shape, scratch_shapes=scratch,
        input_output_aliases=aliases,
        compiler_params=pltpu.CompilerParams(dimension_semantics=("arbitrary", "arbitrary"),
                                             vmem_limit_bytes=VMEM_LIMIT),
        name="mlstm_state" if has_state else "mlstm",
    )(*args)


def _expand_m(m, bt, tt):
    depth, b, _ = m.shape
    r = bt * tt
    rp = max(r, 128)
    v = jnp.repeat(m.reshape(depth, b // bt, bt, NH).transpose(0, 1, 3, 2), tt, axis=-1)
    return jnp.pad(v, ((0, 0), (0, 0), (0, 8 - NH), (0, rp - r)))


def _collapse_m(mrows, bt, tt):
    nbt = mrows.shape[0]
    return mrows[:, :NH, 0:bt * tt:tt].transpose(0, 2, 1).reshape(nbt * bt, NH)


def kernel(x_prompt, x_sample, state_ssm_re, state_ssm_im, state_conv, state_mlstm_c, state_mlstm_n, state_mlstm_m, norm_w, w_in, i_bias, f_bias, lam_re, lam_im, log_dt, b_re, b_im, c_re, c_im, d_skip, w_glu, w_proj_s, conv_w, w_proj_c, mlstm_norm_w, w_proj_m, w_out, final_norm_w):
    depth = norm_w.shape[0]
    bp, tp, _ = x_prompt.shape
    bs, ts, _ = x_sample.shape
    assert tp % PROMPT_CHUNK == 0 and ts % 8 == 0 and ts & (ts - 1) == 0 and bp % 8 == 0 and bs % 64 == 0

    a_re, a_im, bb_re, bb_im, cp_re, cp_im = _s5_discretise(lam_re, lam_im, log_dt, b_re, b_im, c_re, c_im)
    row = lambda a: a.reshape(depth, 1, a.shape[-1])
    nw = row(norm_w)
    w_t = jnp.swapaxes(w_in, 1, 2)
    w_main = _cast_columns(w_t, 0, OFF_IF // D)
    w_gates = _cast_columns(w_t, OFF_G, 3)
    gate_rows = lambda a: jnp.pad(a.reshape(depth, 2, NH, -1), ((0, 0), (0, 0), (0, 8 - NH), (0, 0))).reshape(depth, 16, -1)
    w_if = gate_rows(w_t[:, OFF_IF:OFF_G, :]).astype(BF16)
    bias = gate_rows(jnp.concatenate([i_bias, f_bias], axis=-1)[:, :, None])
    wa = [nw, w_main, w_gates, a_re, a_im, bb_re, bb_im, cp_re, cp_im, row(d_skip),
          w_glu.astype(BF16), w_proj_s.astype(BF16), conv_w, w_proj_c.astype(BF16)]
    wb = [nw, w_main, w_if, bias, w_gates, row(mlstm_norm_w), w_proj_m.astype(BF16), w_out.astype(BF16),
          final_norm_w.reshape(1, D)]

    sb_a, sb_b = 64, 4
    pb_a, pb_b = 8, 2
    st_a = (state_ssm_re.reshape(depth, bs, S5N), state_ssm_im.reshape(depth, bs, S5N), state_conv)
    st_b = (state_mlstm_c, state_mlstm_n, _expand_m(state_mlstm_m, sb_b, ts))

    yp, ys = x_prompt, x_sample
    cp = cs = None
    outs_p = [[] for _ in range(5)]
    outs_s = [[] for _ in range(5)]
    for l in range(depth):
        msc, sre, sim, cv = _s5conv_call(l, yp, wa, None, pb_a, S5_TILE_T)
        yp, cp, nn, mm = _mlstm_call(l, depth, yp, msc, wb, None, cp, pb_b, PROMPT_CHUNK)
        for i, v in enumerate((sre.reshape(bp, S5G, S5P), sim.reshape(bp, S5G, S5P), cv, nn,
                               _collapse_m(mm, pb_b, PROMPT_CHUNK))):
            outs_p[i].append(v)
        msc, sre, sim, cv = _s5conv_call(l, ys, wa, st_a, sb_a, ts)
        ys, cs, nn, mm = _mlstm_call(l, depth, ys, msc, wb, st_b, cs, sb_b, ts)
        for i, v in enumerate((sre.reshape(bs, S5G, S5P), sim.reshape(bs, S5G, S5P), cv, nn,
                               _collapse_m(mm, sb_b, ts))):
            outs_s[i].append(v)
    sp = [jnp.stack(o) for o in outs_p]
    ss = [jnp.stack(o) for o in outs_s]
    return (yp, ys, sp[0], ss[0], sp[1], ss[1], sp[2], ss[2], cp, cs, sp[3], ss[3], sp[4], ss[4])
```

```python
import functools
import math

import jax
import jax.numpy as jnp
from jax import lax
from jax.experimental import pallas as pl
from jax.experimental.pallas import tpu as pltpu

D = 1024
S5W = 512
S5G = 32
S5P = 64
S5C = 16
S5N = S5G * S5P
CW = 512
CK = 3
NH = 4
DH = 256
PROMPT_CHUNK = 256
S5_TILE_T = 64
SHORT_SEQ = 32
EPS = 1e-6
NEG = -1e30
K_SCALE = DH ** -0.5

OFF_S = 0
OFF_M = 3072
OFF_IF = 8192
OFF_G = 8200
IN_W = 11272

F32 = jnp.float32
BF16 = jnp.bfloat16
VMEM_LIMIT = 58 * 1024 * 1024
ROW_CHUNK = 256
PROJ_ROWS = 256


def _sigmoid(x):
    return 1.0 / (1.0 + jnp.exp(-x))


def _silu(x):
    return x * _sigmoid(x)


def _gelu_tanh(x):
    return 0.5 * x * (1.0 + jnp.tanh(math.sqrt(2.0 / math.pi) * (x + 0.044715 * (x * x * x))))


def _log_sigmoid(x):
    return jnp.minimum(x, 0.0) - jnp.log1p(jnp.exp(-jnp.abs(x)))


def _rmsnorm(x, w):
    return x * lax.rsqrt(jnp.mean(x * x, axis=-1, keepdims=True) + EPS) * w


def _dot(a, b):
    return jnp.dot(a, b, preferred_element_type=F32)


def _dot_nt(a, b):
    return lax.dot_general(a, b, (((1,), (1,)), ((), ())), preferred_element_type=F32)


def _dot_tn(a, b):
    return lax.dot_general(a, b, (((0,), (0,)), ((), ())), preferred_element_type=F32)


def _pitch(n):
    p = n + 8
    return p if (p // 8) % 2 == 1 else p + 8


def _s5_disc_kernel(lre_ref, lim_ref, ldt_ref, bre_ref, bim_ref, cre_ref, cim_ref,
                    are_ref, aim_ref, bbre_ref, bbim_ref, cpre_ref, cpim_ref):
    lr = lre_ref[...]
    li = lim_ref[...]
    dt = jnp.exp(ldt_ref[...])
    ea = jnp.exp(lr * dt)
    ar = ea * jnp.cos(li * dt)
    ai = ea * jnp.sin(li * dt)
    are_ref[...] = ar
    aim_ref[...] = ai
    nr = ar - 1.0
    inv = 1.0 / (lr * lr + li * li)
    cr = (nr * lr + ai * li) * inv
    ci = (ai * lr - nr * li) * inv
    half = S5N // 2
    shr = lax.shift_right_logical
    bdiag = (shr(lax.broadcasted_iota(jnp.int32, (256, half), 0), 4)
             == shr(lax.broadcasted_iota(jnp.int32, (256, half), 1), 6))
    cdiag = (shr(lax.broadcasted_iota(jnp.int32, (half, 256), 0), 6)
             == shr(lax.broadcasted_iota(jnp.int32, (half, 256), 1), 4))
    for h in range(2):
        crh = cr[:, h * half:(h + 1) * half]
        cih = ci[:, h * half:(h + 1) * half]
        br = bre_ref[h]
        bi = bim_ref[h]
        bbre_ref[h] = jnp.where(bdiag, crh * br - cih * bi, 0.0).astype(BF16)
        bbim_ref[h] = jnp.where(bdiag, crh * bi + cih * br, 0.0).astype(BF16)
        cpre_ref[h] = jnp.where(cdiag, cre_ref[h], 0.0).astype(BF16)
        cpim_ref[h] = jnp.where(cdiag, cim_ref[h], 0.0).astype(BF16)


def _s5_discretise(lam_re, lam_im, log_dt, b_re, b_im, c_re, c_im):
    depth = lam_re.shape[0]

    def tile_b(b):
        b = jnp.swapaxes(b.reshape(depth, 2, 16, S5P, S5C), 3, 4).reshape(depth, 2, 256, 1, S5P)
        return jnp.broadcast_to(b, (depth, 2, 256, 16, S5P)).reshape(depth, 2, 256, 1024)

    def tile_c(c):
        c = jnp.swapaxes(c.reshape(depth, 2, 16, S5C, S5P), 3, 4).reshape(depth, 2, 1024, 1, S5C)
        return jnp.broadcast_to(c, (depth, 2, 1024, 16, S5C)).reshape(depth, 2, 1024, 256)

    row = lambda a: a.reshape(depth, 1, S5N)
    ldt = jnp.broadcast_to(log_dt[:, :, None], (depth, S5G, S5P))
    vec = pl.BlockSpec((None, 1, S5N), lambda l: (l, 0, 0))
    bmat = pl.BlockSpec((None, 2, 256, 1024), lambda l: (l, 0, 0, 0))
    cmat = pl.BlockSpec((None, 2, 1024, 256), lambda l: (l, 0, 0, 0))
    return pl.pallas_call(
        _s5_disc_kernel,
        grid=(depth,),
        in_specs=[vec, vec, vec, bmat, bmat, cmat, cmat],
        out_specs=[vec, vec, bmat, bmat, cmat, cmat],
        out_shape=[jax.ShapeDtypeStruct((depth, 1, S5N), F32), jax.ShapeDtypeStruct((depth, 1, S5N), F32),
                   jax.ShapeDtypeStruct((depth, 2, 256, 1024), BF16),
                   jax.ShapeDtypeStruct((depth, 2, 256, 1024), BF16),
                   jax.ShapeDtypeStruct((depth, 2, 1024, 256), BF16),
                   jax.ShapeDtypeStruct((depth, 2, 1024, 256), BF16)],
        name="s5_discretise",
    )(row(lam_re), row(lam_im), row(ldt), tile_b(b_re), tile_b(b_im), tile_c(c_re), tile_c(c_im))


def _cast_kernel(shift, *refs):
    if shift:
        wa_ref, wb_ref, o_ref = refs
        w = jnp.concatenate([wa_ref[shift:, :], wb_ref[:shift, :]], axis=0)
    else:
        w_ref, o_ref = refs
        w = w_ref[...]
    o_ref[...] = w.T.astype(BF16)


def _cast_columns(w_t, first_col, n_blocks):
    depth = w_t.shape[0]
    k0, shift = divmod(first_col, D)
    assert shift % 8 == 0
    in_specs = [pl.BlockSpec((None, D, D), lambda l, k: (l, k0 + k, 0))]
    args = [w_t]
    if shift:
        in_specs.append(pl.BlockSpec((None, D, D), lambda l, k: (l, k0 + k + 1, 0)))
        args.append(w_t)
    return pl.pallas_call(
        functools.partial(_cast_kernel, shift),
        grid=(depth, n_blocks),
        in_specs=in_specs,
        out_specs=pl.BlockSpec((None, D, D), lambda l, k: (l, 0, k)),
        out_shape=jax.ShapeDtypeStruct((depth, D, n_blocks * D), BF16),
        name="cast_weights",
    )(*args)


def _s5conv_kernel(has_state, bt, tt, *refs):
    refs = list(refs)
    (x_ref, nw_ref, ws_ref, wg_ref, are_ref, aim_ref, bbre_ref, bbim_ref, cre_ref, cim_ref,
     dsk_ref, wglu_ref, wps_ref, cw_ref, wpc_ref) = refs[:15]
    refs = refs[15:]
    if has_state:
        s0re_ref, s0im_ref, cv0_ref = refs[:3]
        refs = refs[3:]
    msc_ref, sre_ref, sim_ref, cvo_ref = refs[:4]
    h_scr, ps_scr, g_scr, uslab, utb, bu_scr, yslab, y_scr, vhalo = refs[4:]

    r = bt * tt
    pt = _pitch(tt)
    pb = _pitch(bt)
    rc = min(ROW_CHUNK, r)
    nchunk = r // rc
    ti = pl.program_id(1)

    def merge_gates(ci):
        rows = slice(ci * rc, (ci + 1) * rc)
        g_scr[rows, :] = _sigmoid(_dot(h_scr[rows, :], wg_ref[...]))

    @pl.when(ti == 0)
    def _init():
        if has_state:
            sre_ref[...] = s0re_ref[...]
            sim_ref[...] = s0im_ref[...]
            vhalo[:, 6:8, :] = cv0_ref[...]
        else:
            sre_ref[...] = jnp.zeros_like(sre_ref)
            sim_ref[...] = jnp.zeros_like(sim_ref)
            vhalo[:, 6:8, :] = jnp.zeros((bt, 2, CW), F32)

    x = x_ref[...].reshape(r, D)
    h = _rmsnorm(x, nw_ref[...]).astype(BF16)
    h_scr[...] = h
    ps_scr[...] = _dot(h, ws_ref[...])


    for b in range(bt):
        for j in range(4):
            uslab[j, b * pt:b * pt + tt, :] = ps_scr[b * tt:(b + 1) * tt, j * 128:(j + 1) * 128]
    for t in range(tt):
        for bs in range(bt // 8):
            dst = t * bt + bs * 8
            for j in range(4):
                utb[dst:dst + 8, j * 128:(j + 1) * 128] = uslab[j, pl.ds(bs * 8 * pt + t, 8, stride=pt), :]
    for ci in range(nchunk // 2):
        merge_gates(ci)

    for hf in range(2):
        uh = utb[:, hf * 256:(hf + 1) * 256].astype(BF16)
        bu_scr[:, hf * 1024:(hf + 1) * 1024] = _dot(uh, bbre_ref[hf])
        bu_scr[:, S5N + hf * 1024:S5N + (hf + 1) * 1024] = _dot(uh, bbim_ref[hf])
    for ci in range(nchunk // 2, nchunk):
        merge_gates(ci)

    v = ps_scr[:, 1536:2048] * ps_scr[:, 2048:2560]
    vhalo[:, 8:8 + tt, :] = v.reshape(bt, tt, CW)
    cw = cw_ref[...]
    yc = (cw[0:1, :] * vhalo[:, 6:6 + tt, :] + cw[1:2, :] * vhalo[:, 7:7 + tt, :]
          + cw[2:3, :] * vhalo[:, 8:8 + tt, :])
    ps_scr[:, 2048:2560] = yc.reshape(r, CW)
    new_halo = vhalo[:, tt + 6:tt + 8, :]
    vhalo[:, 6:8, :] = new_halo
    cvo_ref[...] = new_halo

    for ci in range(nchunk):
        rows = slice(ci * rc, (ci + 1) * rc)
        tc_ = ps_scr[rows, 1024:1536] * ps_scr[rows, 2048:2560] * _silu(ps_scr[rows, 2560:3072])
        ps_scr[rows, 1024:2048] = g_scr[rows, D:] * _dot(tc_.astype(BF16), wpc_ref[...])

    for q in range(4):
        c_re = slice(q * 512, (q + 1) * 512)
        c_im = slice(S5N + q * 512, S5N + (q + 1) * 512)
        ar = jnp.broadcast_to(are_ref[:, c_re], (8, 512))
        ai = jnp.broadcast_to(aim_ref[:, c_re], (8, 512))
        for bs in range(bt // 8):
            r0 = bs * 8
            sr = sre_ref[r0:r0 + 8, c_re]
            si = sim_ref[r0:r0 + 8, c_re]
            for t in range(tt):
                row = t * bt + r0
                nr = ar * sr - ai * si + bu_scr[row:row + 8, c_re]
                ni = ar * si + ai * sr + bu_scr[row:row + 8, c_im]
                bu_scr[row:row + 8, c_re] = nr
                bu_scr[row:row + 8, c_im] = ni
                sr, si = nr, ni
            sre_ref[r0:r0 + 8, c_re] = sr
            sim_ref[r0:r0 + 8, c_re] = si

    for m in range(2):
        sre = bu_scr[:, m * 1024:(m + 1) * 1024].astype(BF16)
        sim = bu_scr[:, S5N + m * 1024:S5N + (m + 1) * 1024].astype(BF16)
        ytb = _dot(sre, cre_ref[m]) - _dot(sim, cim_ref[m])
        for t in range(tt):
            for jj in range(2):
                yslab[2 * m + jj, t * pb:t * pb + bt, :] = ytb[t * bt:(t + 1) * bt, jj * 128:(jj + 1) * 128]
    for b in range(bt):
        for ts in range(tt // 8):
            dst = b * tt + ts * 8
            for j in range(4):
                y_scr[dst:dst + 8, j * 128:(j + 1) * 128] = yslab[j, pl.ds(ts * 8 * pb + b, 8, stride=pb), :]

    nb = rc // tt if rc >= tt else 0
    dsk = dsk_ref[...]
    for ci in range(nchunk):
        rows = slice(ci * rc, (ci + 1) * rc)
        y = y_scr[rows, :] + dsk * ps_scr[rows, 0:512]
        gl = _gelu_tanh(y)
        glu = gl * _sigmoid(_dot(gl.astype(BF16), wglu_ref[...]))
        ts_ = glu * _silu(ps_scr[rows, 512:1024])
        ys = _dot(ts_.astype(BF16), wps_ref[...])
        out = g_scr[rows, :D] * ys + ps_scr[rows, 1024:2048]
        if nb:
            msc_ref[ci * nb:(ci + 1) * nb] = out.reshape(nb, tt, D)
        else:
            per = tt // rc
            msc_ref[ci // per, (ci % per) * rc:(ci % per + 1) * rc, :] = out


def _s5conv_call(layer, x, weights, state, bt, tt):
    b, t, _ = x.shape
    r = bt * tt
    has_state = state is not None
    wspec = lambda shape: pl.BlockSpec((None,) + shape, lambda bi, ti: (layer,) + (0,) * len(shape),
                                       pipeline_mode=pl.Buffered(1))
    tile = pl.BlockSpec((bt, tt, D), lambda bi, ti: (bi, ti, 0))
    in_specs = [tile, wspec((1, D)), wspec((D, 3072)), wspec((D, 2048)), wspec((1, S5N)), wspec((1, S5N)),
                wspec((2, 256, 1024)), wspec((2, 256, 1024)), wspec((2, 1024, 256)), wspec((2, 1024, 256)),
                wspec((1, S5W)), wspec((S5W, S5W)), wspec((S5W, D)), wspec((CK, CW)), wspec((CW, D))]
    args = [x] + list(weights)
    if has_state:
        in_specs += [pl.BlockSpec((None, bt, S5N), lambda bi, ti: (layer, bi, 0)),
                     pl.BlockSpec((None, bt, S5N), lambda bi, ti: (layer, bi, 0)),
                     pl.BlockSpec((None, bt, CK - 1, CW), lambda bi, ti: (layer, bi, 0, 0))]
        args += list(state)
    out_specs = [tile,
                 pl.BlockSpec((bt, S5N), lambda bi, ti: (bi, 0)),
                 pl.BlockSpec((bt, S5N), lambda bi, ti: (bi, 0)),
                 pl.BlockSpec((bt, CK - 1, CW), lambda bi, ti: (bi, 0, 0))]
    out_shape = [jax.ShapeDtypeStruct((b, t, D), F32), jax.ShapeDtypeStruct((b, S5N), F32),
                 jax.ShapeDtypeStruct((b, S5N), F32), jax.ShapeDtypeStruct((b, CK - 1, CW), F32)]
    scratch = [pltpu.VMEM((r, D), BF16),
               pltpu.VMEM((r, 3072), F32),
               pltpu.VMEM((r, 2 * D), F32),
               pltpu.VMEM((4, bt * _pitch(tt), 128), F32),
               pltpu.VMEM((r, S5W), F32),
               pltpu.VMEM((r, 2 * S5N), F32),
               pltpu.VMEM((4, tt * _pitch(bt), 128), F32),
               pltpu.VMEM((r, S5W), F32),
               pltpu.VMEM((bt, tt + 8, CW), F32)]
    return pl.pallas_call(
        functools.partial(_s5conv_kernel, has_state, bt, tt),
        grid=(b // bt, t // tt),
        in_specs=in_specs, out_specs=out_specs, out_shape=out_shape, scratch_shapes=scratch,
        compiler_params=pltpu.CompilerParams(dimension_semantics=("arbitrary", "arbitrary"),
                                             vmem_limit_bytes=VMEM_LIMIT),
        name="s5conv_state" if has_state else "s5conv",
    )(*args)


def _seg_scan(x, pos, seg, op, fill):
    s = 1
    while s < seg:
        x = op(x, jnp.where(pos >= s, pltpu.roll(x, s, 1), fill))
        s *= 2
    return x


def _seg_bcast_last(x, pos, seg):
    n = x.shape[1]
    s = 1
    while s < seg:
        x = jnp.where(pos + s <= seg - 1, pltpu.roll(x, n - s, 1), x)
        s *= 2
    return x


def _mlstm_kernel(has_state, has_alias, final, single_tile, bt, tt, nsub, *refs):
    refs = list(refs)
    x_ref, msc_ref, nw_ref = refs[:3]
    wm_refs = refs[3:8]
    wif_ref, bias_ref, wgm_ref, hnw_ref, wpm_ref, wout_ref, fnw_ref = refs[8:15]
    refs = refs[15:]
    if has_state:
        c0_ref, n0_ref, m0_ref = refs[:3]
        refs = refs[3:]
    if has_alias:
        refs = refs[1:]
    y_ref, c_ref, n_ref, m_ref = refs[:4]
    h_scr, p_scr, hm_scr, sg_scr = refs[4:8]
    if nsub > 1:
        zt_scr, gd_scr = refs[8:]

    r = bt * tt
    ro = r * nsub
    rp = max(r, 128)
    step = pl.program_id(1)

    def init_state():
        if has_state:
            c_ref[...] = c0_ref[...]
            n_ref[...] = n0_ref[...]
            m_ref[...] = m0_ref[...]
        else:
            c_ref[...] = jnp.zeros_like(c_ref)
            n_ref[...] = jnp.zeros_like(n_ref)
            m_ref[...] = jnp.zeros_like(m_ref)

    def normalise():
        hh = _rmsnorm(x_ref[...].reshape(ro, D), nw_ref[...]).astype(BF16)
        h_scr[...] = hh
        return hh

    def project(hh):
        for k, w_ref in enumerate(wm_refs):
            p_scr[:, k * D:(k + 1) * D] = _dot(hh, w_ref[...])

    def merge_gate(piece):
        cols = slice(piece * DH, (piece + 1) * DH)
        sg_scr[:, cols] = _sigmoid(_dot(h_scr[...], wgm_ref[:, cols]))

    def gate_chain(hrows, mprev):
        n = hrows.shape[0]
        gt = _dot_nt(wif_ref[...], hrows) + bias_ref[...]
        pos = jnp.bitwise_and(lax.broadcasted_iota(jnp.int32, (8, n), 1), tt - 1)
        bcum = _seg_scan(_log_sigmoid(gt[8:16, :]), pos, tt, jnp.add, 0.0)
        g = gt[0:8, :] - bcum
        mrun = jnp.maximum(mprev, _seg_scan(g, pos, tt, jnp.maximum, NEG))
        mlast = _seg_bcast_last(mrun, pos, tt)
        winter = jnp.exp(mprev - mrun)
        efloor = jnp.exp(-(bcum + mrun))
        wrow = jnp.exp(g - mlast)
        decay = jnp.exp(mprev - mlast)
        mnew = _seg_bcast_last(bcum, pos, tt) + mlast
        zt = jnp.concatenate([mrun, winter, efloor, wrow], axis=0).T
        return g, decay, zt, mnew

    if not single_tile:
        pl.when(step == 0)(init_state)
        c_prev, n_prev, m_start = c_ref, n_ref, (lambda: m_ref[...])
    elif has_state:
        c_prev, n_prev, m_start = c0_ref, n0_ref, (lambda: m0_ref[...])
    else:
        c_ref[...] = jnp.zeros_like(c_ref)
        n_ref[...] = jnp.zeros_like(n_ref)
        c_prev, n_prev, m_start = c_ref, n_ref, (lambda: jnp.zeros(m_ref.shape, F32))

    if nsub == 1:
        base = 0
        h = normalise()
        hp = h if rp == r else jnp.concatenate([h, jnp.zeros((rp - r, D), BF16)], axis=0)
        g, decay, zt, mnew = gate_chain(hp, m_start())
        m_ref[...] = mnew
        project(h)
    else:
        base = pl.multiple_of(step * r, r)

        @pl.when(step == 0)
        def _block_prologue():
            hh = normalise()
            ga, da, za, mnew_all = gate_chain(hh, m_start())
            project(hh)
            m_ref[...] = mnew_all
            zt_scr[:, 0:32] = za
            for s in range(nsub):
                gd_scr[16 * s:16 * s + 8, 0:r] = ga[:, s * r:(s + 1) * r]
                gd_scr[16 * s + 8:16 * s + 16, 0:r] = da[:, s * r:(s + 1) * r]
            for piece in range(NH):
                merge_gate(piece)

        zt = zt_scr[pl.ds(base, r), 0:32]
        gd = gd_scr[pl.ds(pl.multiple_of(step * 16, 16), 16), :]
        g = gd[0:8, 0:r]
        decay = gd[8:16, 0:r]

    if tt <= SHORT_SEQ:
        nst = NH * r
        prow = slice(0, r) if nsub == 1 else pl.ds(base, r)
        stack = lambda off: jnp.concatenate(
            [p_scr[prow, off + hd * DH:off + (hd + 1) * DH] for hd in range(NH)], axis=0)
        qs = stack(0)
        ks = stack(D) * K_SCALE
        vs = stack(2 * D)
        col = lambda c0: jnp.concatenate([zt[0:r, c0 + hd:c0 + hd + 1] for hd in range(NH)], axis=0)
        mcol, wi, ef, wc = col(0), col(8), col(16), col(24)
        grow = jnp.concatenate([g[hd:hd + 1, 0:r] for hd in range(NH)], axis=1)
        row_i = lax.broadcasted_iota(jnp.int32, (nst, nst), 0)
        col_i = lax.broadcasted_iota(jnp.int32, (nst, nst), 1)
        shift = tt.bit_length() - 1
        same = lax.shift_right_logical(row_i, shift) == lax.shift_right_logical(col_i, shift)
        arg = jnp.where(col_i <= row_i, grow - mcol, NEG)
        dm = jnp.exp(jnp.where(same, arg, NEG))
        sc = _dot_nt(qs.astype(BF16), ks.astype(BF16)) * dm
        intra = _dot(sc.astype(BF16), vs.astype(BF16))
        pairs = [(hd, b) for hd in range(NH) for b in range(bt)]
        blk = lambda hd, b: slice(hd * r + b * tt, hd * r + (b + 1) * tt)
        inter = jnp.concatenate([_dot(qs[blk(hd, b)].astype(BF16), c_prev[b, hd].astype(BF16))
                                 for hd, b in pairs], axis=0)
        nfull = jnp.concatenate([jnp.broadcast_to(n_prev[b, hd:hd + 1, :], (tt, DH)) for hd, b in pairs], axis=0)
        num = wi * inter + intra
        den = wi * jnp.sum(qs * nfull, axis=-1, keepdims=True) + jnp.sum(sc, axis=-1, keepdims=True)
        hh = num * (1.0 / jnp.maximum(jnp.abs(den), ef))
        for hd in range(NH):
            hm_scr[prow, hd * DH:(hd + 1) * DH] = hh[hd * r:(hd + 1) * r]
        wk = ks * wc
        for hd, b in pairs:
            dec = decay[hd:hd + 1, b * tt:b * tt + 1]
            upd = _dot_tn(wk[blk(hd, b)].astype(BF16), vs[blk(hd, b)].astype(BF16))
            c_ref[b, hd] = dec * c_prev[b, hd] + upd
            n_ref[b, hd:hd + 1, :] = (dec * n_prev[b, hd:hd + 1, :]
                                      + jnp.sum(wk[blk(hd, b)], axis=0, keepdims=True))
        if nsub == 1:
            for piece in range(NH):
                merge_gate(piece)

    row_i = lax.broadcasted_iota(jnp.int32, (tt, tt), 0)
    col_i = lax.broadcasted_iota(jnp.int32, (tt, tt), 1)
    causal = row_i >= col_i

    for b in range(bt if tt > SHORT_SEQ else 0):
        rows = slice(b * tt, (b + 1) * tt)
        prow = rows if nsub == 1 else pl.ds(base + b * tt, tt)
        for hd in range(NH):
            cs = slice(hd * DH, (hd + 1) * DH)
            qf = p_scr[prow, cs]
            kf = p_scr[prow, D + hd * DH:D + (hd + 1) * DH] * K_SCALE
            vb = p_scr[prow, 2 * D + hd * DH:2 * D + (hd + 1) * DH].astype(BF16)
            qb = qf.astype(BF16)
            kb = kf.astype(BF16)
            mcol = zt[rows, hd:hd + 1]
            wi = zt[rows, 8 + hd:9 + hd]
            ef = zt[rows, 16 + hd:17 + hd]
            wc = zt[rows, 24 + hd:25 + hd]
            grow = g[hd:hd + 1, b * tt:(b + 1) * tt]
            dm = jnp.exp(jnp.where(causal, grow - mcol, NEG))
            sc = _dot_nt(qb, kb) * dm
            cf = c_prev[b, hd]
            nrow = n_prev[b, hd:hd + 1, :]
            num = wi * _dot(qb, cf.astype(BF16)) + _dot(sc.astype(BF16), vb)
            den = wi * jnp.sum(qf * nrow, axis=-1, keepdims=True) + jnp.sum(sc, axis=-1, keepdims=True)
            hm_scr[prow, cs] = num * (1.0 / jnp.maximum(jnp.abs(den), ef))
            dec = decay[hd:hd + 1, b * tt:b * tt + 1]
            wk = kf * wc
            c_ref[b, hd] = dec * cf + _dot_tn(wk.astype(BF16), vb)
            n_ref[b, hd:hd + 1, :] = dec * nrow + jnp.sum(wk, axis=0, keepdims=True)
            done = b * NH + hd + 1
            if nsub == 1 and (done * NH) % (bt * NH) == 0:
                merge_gate(done * NH // (bt * NH) - 1)

    def epilogue():
        rc = min(ROW_CHUNK, ro)
        hnw = hnw_ref[...]
        for ci in range(ro // rc):
            rows = slice(ci * rc, (ci + 1) * rc)
            hmv = hm_scr[rows, :] * _sigmoid(p_scr[rows, 3 * D:4 * D])
            parts = []
            for hd in range(NH):
                hh = hmv[:, hd * DH:(hd + 1) * DH]
                mu = jnp.mean(hh, axis=-1, keepdims=True)
                dv = hh - mu
                var = jnp.mean(dv * dv, axis=-1, keepdims=True)
                parts.append(dv * lax.rsqrt(var + EPS) * hnw[:, hd * DH:(hd + 1) * DH])
            hn = jnp.concatenate(parts, axis=1)
            tm = hn * _silu(p_scr[rows, 4 * D:5 * D])
            ym = _dot(tm.astype(BF16), wpm_ref[...])
            nb = rc // tt if rc >= tt else 0
            if nb:
                msc = msc_ref[ci * nb:(ci + 1) * nb].reshape(rc, D)
                xr = x_ref[ci * nb:(ci + 1) * nb].reshape(rc, D)
            else:
                per = tt // rc
                msc = msc_ref[ci // per, (ci % per) * rc:(ci % per + 1) * rc, :]
                xr = x_ref[ci // per, (ci % per) * rc:(ci % per + 1) * rc, :]
            merged = msc + sg_scr[rows, :] * ym
            y = xr + _dot(merged.astype(BF16), wout_ref[...])
            if final:
                y = _rmsnorm(y, fnw_ref[...])
            if nb:
                y_ref[ci * nb:(ci + 1) * nb] = y.reshape(nb, tt, D)
            else:
                y_ref[ci // per, (ci % per) * rc:(ci % per + 1) * rc, :] = y

    if nsub == 1:
        epilogue()
    else:
        pl.when(step == nsub - 1)(epilogue)


def _mlstm_call(layer, depth, x, msc, weights, state, c_stack, bt, tt):
    b, t, _ = x.shape
    r = bt * tt
    rp = max(r, 128)
    nbt = b // bt
    has_state = state is not None
    has_alias = c_stack is not None
    final = layer == depth - 1
    nsub = max(1, min(nbt, PROJ_ROWS // r)) if t == tt else 1
    assert nbt % nsub == 0
    ro = r * nsub
    rpo = max(ro, 128)
    if nsub == 1:
        grid = (nbt, t // tt)
        tile_idx = lambda bi, si: (bi, si, 0)
        sub_idx = lambda bi, si: bi
    else:
        grid = (nbt // nsub, nsub)
        tile_idx = lambda bi, si: (bi, 0, 0)
        sub_idx = lambda bi, si: bi * nsub + si
    wspec = lambda shape: pl.BlockSpec((None,) + shape, lambda bi, si: (layer,) + (0,) * len(shape),
                                       pipeline_mode=pl.Buffered(1))
    tile = pl.BlockSpec((bt * nsub, tt, D), tile_idx)
    wcol = lambda k: pl.BlockSpec((None, D, D), lambda bi, si: (layer, 0, k), pipeline_mode=pl.Buffered(1))
    nw, w_main, w_if, bias, w_gates, hnw, wpm, wout, fnw = weights
    in_specs = ([tile, tile, wspec((1, D))] + [wcol(OFF_M // D + k) for k in range(5)]
                + [wspec((16, D)), wspec((16, 1)), wcol(2), wspec((1, D)), wspec((D, D)), wspec((D, D)),
                   pl.BlockSpec((1, D), lambda bi, si: (0, 0), pipeline_mode=pl.Buffered(1))])
    args = [x, msc, nw] + [w_main] * 5 + [w_if, bias, w_gates, hnw, wpm, wout, fnw]
    if has_state:
        in_specs += [pl.BlockSpec((None, bt, NH, DH, DH), lambda bi, si: (layer, sub_idx(bi, si), 0, 0, 0)),
                     pl.BlockSpec((None, bt, NH, DH), lambda bi, si: (layer, sub_idx(bi, si), 0, 0)),
                     pl.BlockSpec((None, None, 8, rpo), lambda bi, si: (layer, bi, 0, 0))]
        c0, n0, m0 = state
        args += [c0, n0, _expand_m(m0, bt * nsub, tt)]
    aliases = {}
    if has_alias:
        aliases = {len(args): 1}
        in_specs.append(pl.BlockSpec(memory_space=pl.ANY))
        args.append(c_stack)
    out_specs = [tile,
                 pl.BlockSpec((None, bt, NH, DH, DH), lambda bi, si: (layer, sub_idx(bi, si), 0, 0, 0)),
                 pl.BlockSpec((bt, NH, DH), lambda bi, si: (sub_idx(bi, si), 0, 0)),
                 pl.BlockSpec((None, 8, rpo), lambda bi, si: (bi, 0, 0))]
    out_shape = [jax.ShapeDtypeStruct((b, t, D), F32), jax.ShapeDtypeStruct((depth, b, NH, DH, DH), F32),
                 jax.ShapeDtypeStruct((b, NH, DH), F32), jax.ShapeDtypeStruct((nbt // nsub, 8, rpo), F32)]
    scratch = [pltpu.VMEM((ro, D), BF16),
               pltpu.VMEM((ro, 5 * D), F32),
               pltpu.VMEM((ro, D), F32),
               pltpu.VMEM((ro, D), F32)]
    if nsub > 1:
        scratch += [pltpu.VMEM((ro, 128), F32),
                    pltpu.VMEM((16 * nsub, 128), F32)]
    y, c_new, n_new, m_rows = pl.pallas_call(
        functools.partial(_mlstm_kernel, has_state, has_alias, final, t == tt, bt, tt, nsub),
        grid=grid,
        in_specs=in_specs, out_specs=out_specs, out_shape=out_shape, scratch_shapes=scratch,
        input_output_aliases=aliases,
        compiler_params=pltpu.CompilerParams(dimension_semantics=("arbitrary", "arbitrary"),
                                             vmem_limit_bytes=VMEM_LIMIT),
        name="mlstm_state" if has_state else "mlstm",
    )(*args)
    return y, c_new, n_new, _collapse_m(m_rows, bt * nsub, tt)


def _expand_m(m, bt, tt):
    depth, b, _ = m.shape
    r = bt * tt
    rp = max(r, 128)
    v = jnp.repeat(m.reshape(depth, b // bt, bt, NH).transpose(0, 1, 3, 2), tt, axis=-1)
    return jnp.pad(v, ((0, 0), (0, 0), (0, 8 - NH), (0, rp - r)))


def _collapse_m(mrows, bt, tt):
    nbt = mrows.shape[0]
    return mrows[:, :NH, 0:bt * tt:tt].transpose(0, 2, 1).reshape(nbt * bt, NH)


def kernel(x_prompt, x_sample, state_ssm_re, state_ssm_im, state_conv, state_mlstm_c, state_mlstm_n, state_mlstm_m, norm_w, w_in, i_bias, f_bias, lam_re, lam_im, log_dt, b_re, b_im, c_re, c_im, d_skip, w_glu, w_proj_s, conv_w, w_proj_c, mlstm_norm_w, w_proj_m, w_out, final_norm_w):
    depth = norm_w.shape[0]
    bp, tp, _ = x_prompt.shape
    bs, ts, _ = x_sample.shape
    assert tp % PROMPT_CHUNK == 0 and ts % 8 == 0 and ts & (ts - 1) == 0 and bp % 8 == 0 and bs % 64 == 0

    a_re, a_im, bb_re, bb_im, cp_re, cp_im = _s5_discretise(lam_re, lam_im, log_dt, b_re, b_im, c_re, c_im)
    row = lambda a: a.reshape(depth, 1, a.shape[-1])
    nw = row(norm_w)
    w_t = jnp.swapaxes(w_in, 1, 2)
    w_main = _cast_columns(w_t, 0, OFF_IF // D)
    w_gates = _cast_columns(w_t, OFF_G, 3)
    gate_rows = lambda a: jnp.pad(a.reshape(depth, 2, NH, -1), ((0, 0), (0, 0), (0, 8 - NH), (0, 0))).reshape(depth, 16, -1)
    w_if = gate_rows(w_t[:, OFF_IF:OFF_G, :]).astype(BF16)
    bias = gate_rows(jnp.concatenate([i_bias, f_bias], axis=-1)[:, :, None])
    wa = [nw, w_main, w_gates, a_re, a_im, bb_re, bb_im, cp_re, cp_im, row(d_skip),
          w_glu.astype(BF16), w_proj_s.astype(BF16), conv_w, w_proj_c.astype(BF16)]
    wb = [nw, w_main, w_if, bias, w_gates, row(mlstm_norm_w), w_proj_m.astype(BF16), w_out.astype(BF16),
          final_norm_w.reshape(1, D)]

    sb_a, sb_b = 64, 4
    pb_a, pb_b = 8, 2
    st_a = (state_ssm_re.reshape(depth, bs, S5N), state_ssm_im.reshape(depth, bs, S5N), state_conv)
    st_b = (state_mlstm_c, state_mlstm_n, state_mlstm_m)

    yp, ys = x_prompt, x_sample
    cp = cs = None
    outs_p = [[] for _ in range(5)]
    outs_s = [[] for _ in range(5)]
    for l in range(depth):
        msc, sre, sim, cv = _s5conv_call(l, yp, wa, None, pb_a, S5_TILE_T)
        yp, cp, nn, mm = _mlstm_call(l, depth, yp, msc, wb, None, cp, pb_b, PROMPT_CHUNK)
        for i, v in enumerate((sre.reshape(bp, S5G, S5P), sim.reshape(bp, S5G, S5P), cv, nn, mm)):
            outs_p[i].append(v)
        msc, sre, sim, cv = _s5conv_call(l, ys, wa, st_a, sb_a, ts)
        ys, cs, nn, mm = _mlstm_call(l, depth, ys, msc, wb, st_b, cs, sb_b, ts)
        for i, v in enumerate((sre.reshape(bs, S5G, S5P), sim.reshape(bs, S5G, S5P), cv, nn, mm)):
            outs_s[i].append(v)
    sp = [jnp.stack(o) for o in outs_p]
    ss = [jnp.stack(o) for o in outs_s]
    return (yp, ys, sp[0], ss[0], sp[1], ss[1], sp[2], ss[2], cp, cs, sp[3], ss[3], sp[4], ss[4])
```

```python
import functools
import math

import jax
import jax.numpy as jnp
from jax import lax
from jax.experimental import pallas as pl
from jax.experimental.pallas import tpu as pltpu

D = 1024
S5W = 512
S5G = 32
S5P = 64
S5C = 16
S5N = S5G * S5P
CW = 512
CK = 3
NH = 4
DH = 256
PROMPT_CHUNK = 256
S5_TILE_T = 64
SHORT_SEQ = 32
EPS = 1e-6
NEG = -1e30
K_SCALE = DH ** -0.5

OFF_S = 0
OFF_M = 3072
OFF_IF = 8192
OFF_G = 8200
IN_W = 11272

F32 = jnp.float32
BF16 = jnp.bfloat16
VMEM_LIMIT = 58 * 1024 * 1024
ROW_CHUNK = 256
PROJ_ROWS = 256


def _sigmoid(x):
    return 1.0 / (1.0 + jnp.exp(-x))


def _silu(x):
    return x * _sigmoid(x)


def _gelu_tanh(x):
    return 0.5 * x * (1.0 + jnp.tanh(math.sqrt(2.0 / math.pi) * (x + 0.044715 * (x * x * x))))


def _log_sigmoid(x):
    return jnp.minimum(x, 0.0) - jnp.log1p(jnp.exp(-jnp.abs(x)))


def _rmsnorm(x, w):
    return x * lax.rsqrt(jnp.mean(x * x, axis=-1, keepdims=True) + EPS) * w


def _dot(a, b):
    return jnp.dot(a, b, preferred_element_type=F32)


def _dot_nt(a, b):
    return lax.dot_general(a, b, (((1,), (1,)), ((), ())), preferred_element_type=F32)


def _dot_tn(a, b):
    return lax.dot_general(a, b, (((0,), (0,)), ((), ())), preferred_element_type=F32)


def _pitch(n):
    p = n + 8
    return p if (p // 8) % 2 == 1 else p + 8


def _s5_disc_kernel(lre_ref, lim_ref, ldt_ref, bre_ref, bim_ref, cre_ref, cim_ref,
                    are_ref, aim_ref, bbre_ref, bbim_ref, cpre_ref, cpim_ref):
    lr = lre_ref[...]
    li = lim_ref[...]
    dt = jnp.exp(ldt_ref[...])
    ea = jnp.exp(lr * dt)
    ar = ea * jnp.cos(li * dt)
    ai = ea * jnp.sin(li * dt)
    are_ref[...] = ar
    aim_ref[...] = ai
    nr = ar - 1.0
    inv = 1.0 / (lr * lr + li * li)
    cr = (nr * lr + ai * li) * inv
    ci = (ai * lr - nr * li) * inv
    half = S5N // 2
    shr = lax.shift_right_logical
    bdiag = (shr(lax.broadcasted_iota(jnp.int32, (256, half), 0), 4)
             == shr(lax.broadcasted_iota(jnp.int32, (256, half), 1), 6))
    cdiag = (shr(lax.broadcasted_iota(jnp.int32, (half, 256), 0), 6)
             == shr(lax.broadcasted_iota(jnp.int32, (half, 256), 1), 4))
    rep = lambda x: jnp.concatenate([x] * 16, axis=0)
    for h in range(2):
        crh = cr[:, h * half:(h + 1) * half]
        cih = ci[:, h * half:(h + 1) * half]
        br = bre_ref[h]
        bi = bim_ref[h]
        bbre_ref[h] = jnp.where(bdiag, rep(crh * br - cih * bi), 0.0).astype(BF16)
        bbim_ref[h] = jnp.where(bdiag, rep(crh * bi + cih * br), 0.0).astype(BF16)
        cpre_ref[h] = jnp.where(cdiag, rep(cre_ref[h]), 0.0).astype(BF16)
        cpim_ref[h] = jnp.where(cdiag, rep(cim_ref[h]), 0.0).astype(BF16)


def _s5_discretise(lam_re, lam_im, log_dt, b_re, b_im, c_re, c_im):
    depth = lam_re.shape[0]

    def rows_b(b):
        return jnp.swapaxes(b.reshape(depth, 2, 16 * S5P, S5C), 2, 3)

    def rows_c(c):
        return jnp.swapaxes(c.reshape(depth, 2, 16 * S5C, S5P), 2, 3)

    row = lambda a: a.reshape(depth, 1, S5N)
    ldt = jnp.broadcast_to(log_dt[:, :, None], (depth, S5G, S5P))
    vec = pl.BlockSpec((None, 1, S5N), lambda l: (l, 0, 0))
    bmat = pl.BlockSpec((None, 2, 256, 1024), lambda l: (l, 0, 0, 0))
    cmat = pl.BlockSpec((None, 2, 1024, 256), lambda l: (l, 0, 0, 0))
    brow = pl.BlockSpec((None, 2, S5C, 1024), lambda l: (l, 0, 0, 0))
    crow = pl.BlockSpec((None, 2, S5P, 256), lambda l: (l, 0, 0, 0))
    return pl.pallas_call(
        _s5_disc_kernel,
        grid=(depth,),
        in_specs=[vec, vec, vec, brow, brow, crow, crow],
        out_specs=[vec, vec, bmat, bmat, cmat, cmat],
        out_shape=[jax.ShapeDtypeStruct((depth, 1, S5N), F32), jax.ShapeDtypeStruct((depth, 1, S5N), F32),
                   jax.ShapeDtypeStruct((depth, 2, 256, 1024), BF16),
                   jax.ShapeDtypeStruct((depth, 2, 256, 1024), BF16),
                   jax.ShapeDtypeStruct((depth, 2, 1024, 256), BF16),
                   jax.ShapeDtypeStruct((depth, 2, 1024, 256), BF16)],
        name="s5_discretise",
    )(row(lam_re), row(lam_im), row(ldt), rows_b(b_re), rows_b(b_im), rows_c(c_re), rows_c(c_im))


def _cast_kernel(shift, *refs):
    if shift:
        wa_ref, wb_ref, o_ref = refs
        w = jnp.concatenate([wa_ref[shift:, :], wb_ref[...]], axis=0)
    else:
        w_ref, o_ref = refs
        w = w_ref[...]
    o_ref[...] = w.T.astype(BF16)


def _cast_columns(w_t, first_col, n_blocks):
    depth = w_t.shape[0]
    k0, shift = divmod(first_col, D)
    assert shift % 8 == 0
    in_specs = [pl.BlockSpec((None, D, D), lambda l, k: (l, k0 + k, 0))]
    args = [w_t]
    if shift:
        assert D % shift == 0
        in_specs.append(pl.BlockSpec((None, shift, D), lambda l, k: (l, (k0 + k + 1) * (D // shift), 0)))
        args.append(w_t)
    return pl.pallas_call(
        functools.partial(_cast_kernel, shift),
        grid=(depth, n_blocks),
        in_specs=in_specs,
        out_specs=pl.BlockSpec((None, D, D), lambda l, k: (l, 0, k)),
        out_shape=jax.ShapeDtypeStruct((depth, D, n_blocks * D), BF16),
        name="cast_weights",
    )(*args)


def _s5conv_kernel(has_state, bt, tt, *refs):
    refs = list(refs)
    (x_ref, nw_ref, ws_ref, wg_ref, are_ref, aim_ref, bbre_ref, bbim_ref, cre_ref, cim_ref,
     dsk_ref, wglu_ref, wps_ref, cw_ref, wpc_ref) = refs[:15]
    refs = refs[15:]
    if has_state:
        s0re_ref, s0im_ref, cv0_ref = refs[:3]
        refs = refs[3:]
    msc_ref, sre_ref, sim_ref, cvo_ref = refs[:4]
    h_scr, ps_scr, g_scr, uslab, utb, bu_scr, yslab, y_scr, vhalo = refs[4:]

    r = bt * tt
    pt = _pitch(tt)
    pb = _pitch(bt)
    rc = min(ROW_CHUNK, r)
    nchunk = r // rc
    ti = pl.program_id(1)

    def merge_gates(ci):
        rows = slice(ci * rc, (ci + 1) * rc)
        g_scr[rows, :] = _sigmoid(_dot(h_scr[rows, :], wg_ref[...]))

    @pl.when(ti == 0)
    def _init():
        if has_state:
            sre_ref[...] = s0re_ref[...]
            sim_ref[...] = s0im_ref[...]
            vhalo[:, 6:8, :] = cv0_ref[...]
        else:
            sre_ref[...] = jnp.zeros_like(sre_ref)
            sim_ref[...] = jnp.zeros_like(sim_ref)
            vhalo[:, 6:8, :] = jnp.zeros((bt, 2, CW), F32)

    x = x_ref[...].reshape(r, D)
    h = _rmsnorm(x, nw_ref[...]).astype(BF16)
    h_scr[...] = h
    ps_scr[...] = _dot(h, ws_ref[...])


    for b in range(bt):
        for j in range(4):
            uslab[j, b * pt:b * pt + tt, :] = ps_scr[b * tt:(b + 1) * tt, j * 128:(j + 1) * 128]
    for t in range(tt):
        for bs in range(bt // 8):
            dst = t * bt + bs * 8
            for j in range(4):
                utb[dst:dst + 8, j * 128:(j + 1) * 128] = uslab[j, pl.ds(bs * 8 * pt + t, 8, stride=pt), :]
    for ci in range(nchunk // 2):
        merge_gates(ci)

    for hf in range(2):
        uh = utb[:, hf * 256:(hf + 1) * 256].astype(BF16)
        bu_scr[:, hf * 1024:(hf + 1) * 1024] = _dot(uh, bbre_ref[hf])
        bu_scr[:, S5N + hf * 1024:S5N + (hf + 1) * 1024] = _dot(uh, bbim_ref[hf])
    for ci in range(nchunk // 2, nchunk):
        merge_gates(ci)

    v = ps_scr[:, 1536:2048] * ps_scr[:, 2048:2560]
    vhalo[:, 8:8 + tt, :] = v.reshape(bt, tt, CW)
    cw = cw_ref[...]
    yc = (cw[0:1, :] * vhalo[:, 6:6 + tt, :] + cw[1:2, :] * vhalo[:, 7:7 + tt, :]
          + cw[2:3, :] * vhalo[:, 8:8 + tt, :])
    ps_scr[:, 2048:2560] = yc.reshape(r, CW)
    new_halo = vhalo[:, tt + 6:tt + 8, :]
    vhalo[:, 6:8, :] = new_halo
    cvo_ref[...] = new_halo

    for ci in range(nchunk):
        rows = slice(ci * rc, (ci + 1) * rc)
        tc_ = ps_scr[rows, 1024:1536] * ps_scr[rows, 2048:2560] * _silu(ps_scr[rows, 2560:3072])
        ps_scr[rows, 1024:2048] = g_scr[rows, D:] * _dot(tc_.astype(BF16), wpc_ref[...])

    for q in range(4):
        c_re = slice(q * 512, (q + 1) * 512)
        c_im = slice(S5N + q * 512, S5N + (q + 1) * 512)
        ar = jnp.broadcast_to(are_ref[:, c_re], (8, 512))
        ai = jnp.broadcast_to(aim_ref[:, c_re], (8, 512))
        for bs in range(bt // 8):
            r0 = bs * 8
            sr = sre_ref[r0:r0 + 8, c_re]
            si = sim_ref[r0:r0 + 8, c_re]
            for t in range(tt):
                row = t * bt + r0
                nr = ar * sr - ai * si + bu_scr[row:row + 8, c_re]
                ni = ar * si + ai * sr + bu_scr[row:row + 8, c_im]
                bu_scr[row:row + 8, c_re] = nr
                bu_scr[row:row + 8, c_im] = ni
                sr, si = nr, ni
            sre_ref[r0:r0 + 8, c_re] = sr
            sim_ref[r0:r0 + 8, c_re] = si

    for m in range(2):
        sre = bu_scr[:, m * 1024:(m + 1) * 1024].astype(BF16)
        sim = bu_scr[:, S5N + m * 1024:S5N + (m + 1) * 1024].astype(BF16)
        ytb = _dot(sre, cre_ref[m]) - _dot(sim, cim_ref[m])
        for t in range(tt):
            for jj in range(2):
                yslab[2 * m + jj, t * pb:t * pb + bt, :] = ytb[t * bt:(t + 1) * bt, jj * 128:(jj + 1) * 128]
    for b in range(bt):
        for ts in range(tt // 8):
            dst = b * tt + ts * 8
            for j in range(4):
                y_scr[dst:dst + 8, j * 128:(j + 1) * 128] = yslab[j, pl.ds(ts * 8 * pb + b, 8, stride=pb), :]

    nb = rc // tt if rc >= tt else 0
    dsk = dsk_ref[...]
    for ci in range(nchunk):
        rows = slice(ci * rc, (ci + 1) * rc)
        y = y_scr[rows, :] + dsk * ps_scr[rows, 0:512]
        gl = _gelu_tanh(y)
        glu = gl * _sigmoid(_dot(gl.astype(BF16), wglu_ref[...]))
        ts_ = glu * _silu(ps_scr[rows, 512:1024])
        ys = _dot(ts_.astype(BF16), wps_ref[...])
        out = g_scr[rows, :D] * ys + ps_scr[rows, 1024:2048]
        if nb:
            msc_ref[ci * nb:(ci + 1) * nb] = out.reshape(nb, tt, D)
        else:
            per = tt // rc
            msc_ref[ci // per, (ci % per) * rc:(ci % per + 1) * rc, :] = out


def _s5conv_call(layer, x, weights, state, bt, tt):
    b, t, _ = x.shape
    r = bt * tt
    has_state = state is not None
    wspec = lambda shape: pl.BlockSpec((None,) + shape, lambda bi, ti: (layer,) + (0,) * len(shape),
                                       pipeline_mode=pl.Buffered(1))
    tile = pl.BlockSpec((bt, tt, D), lambda bi, ti: (bi, ti, 0))
    in_specs = [tile, wspec((1, D)), wspec((D, 3072)), wspec((D, 2048)), wspec((1, S5N)), wspec((1, S5N)),
                wspec((2, 256, 1024)), wspec((2, 256, 1024)), wspec((2, 1024, 256)), wspec((2, 1024, 256)),
                wspec((1, S5W)), wspec((S5W, S5W)), wspec((S5W, D)), wspec((CK, CW)), wspec((CW, D))]
    args = [x] + list(weights)
    if has_state:
        in_specs += [pl.BlockSpec((None, bt, S5N), lambda bi, ti: (layer, bi, 0)),
                     pl.BlockSpec((None, bt, S5N), lambda bi, ti: (layer, bi, 0)),
                     pl.BlockSpec((None, bt, CK - 1, CW), lambda bi, ti: (layer, bi, 0, 0))]
        args += list(state)
    out_specs = [tile,
                 pl.BlockSpec((bt, S5N), lambda bi, ti: (bi, 0)),
                 pl.BlockSpec((bt, S5N), lambda bi, ti: (bi, 0)),
                 pl.BlockSpec((bt, CK - 1, CW), lambda bi, ti: (bi, 0, 0))]
    out_shape = [jax.ShapeDtypeStruct((b, t, D), F32), jax.ShapeDtypeStruct((b, S5N), F32),
                 jax.ShapeDtypeStruct((b, S5N), F32), jax.ShapeDtypeStruct((b, CK - 1, CW), F32)]
    scratch = [pltpu.VMEM((r, D), BF16),
               pltpu.VMEM((r, 3072), F32),
               pltpu.VMEM((r, 2 * D), F32),
               pltpu.VMEM((4, bt * _pitch(tt), 128), F32),
               pltpu.VMEM((r, S5W), F32),
               pltpu.VMEM((r, 2 * S5N), F32),
               pltpu.VMEM((4, tt * _pitch(bt), 128), F32),
               pltpu.VMEM((r, S5W), F32),
               pltpu.VMEM((bt, tt + 8, CW), F32)]
    return pl.pallas_call(
        functools.partial(_s5conv_kernel, has_state, bt, tt),
        grid=(b // bt, t // tt),
        in_specs=in_specs, out_specs=out_specs, out_shape=out_shape, scratch_shapes=scratch,
        compiler_params=pltpu.CompilerParams(dimension_semantics=("arbitrary", "arbitrary"),
                                             vmem_limit_bytes=VMEM_LIMIT),
        name="s5conv_state" if has_state else "s5conv",
    )(*args)


def _seg_scan(x, pos, seg, op, fill):
    s = 1
    while s < seg:
        x = op(x, jnp.where(pos >= s, pltpu.roll(x, s, 1), fill))
        s *= 2
    return x


def _seg_bcast_last(x, pos, seg):
    n = x.shape[1]
    s = 1
    while s < seg:
        x = jnp.where(pos + s <= seg - 1, pltpu.roll(x, n - s, 1), x)
        s *= 2
    return x


def _mlstm_kernel(has_state, has_alias, final, single_tile, bt, tt, nsub, *refs):
    refs = list(refs)
    x_ref, msc_ref, nw_ref = refs[:3]
    wm_refs = refs[3:8]
    wif_ref, bias_ref, wgm_ref, hnw_ref, wpm_ref, wout_ref, fnw_ref = refs[8:15]
    refs = refs[15:]
    if has_state:
        c0_ref, n0_ref, m0_ref = refs[:3]
        refs = refs[3:]
    if has_alias:
        refs = refs[1:]
    y_ref, c_ref, n_ref, m_ref = refs[:4]
    h_scr, p_scr, hm_scr, sg_scr = refs[4:8]
    if nsub > 1:
        zt_scr, gd_scr = refs[8:]

    r = bt * tt
    ro = r * nsub
    rp = max(r, 128)
    step = pl.program_id(1)

    def init_state():
        if has_state:
            c_ref[...] = c0_ref[...]
            n_ref[...] = n0_ref[...]
            m_ref[...] = m0_ref[...]
        else:
            c_ref[...] = jnp.zeros_like(c_ref)
            n_ref[...] = jnp.zeros_like(n_ref)
            m_ref[...] = jnp.zeros_like(m_ref)

    def normalise():
        hh = _rmsnorm(x_ref[...].reshape(ro, D), nw_ref[...]).astype(BF16)
        h_scr[...] = hh
        return hh

    def project(hh):
        for k, w_ref in enumerate(wm_refs):
            p_scr[:, k * D:(k + 1) * D] = _dot(hh, w_ref[...])

    def merge_gate(piece):
        cols = slice(piece * DH, (piece + 1) * DH)
        sg_scr[:, cols] = _sigmoid(_dot(h_scr[...], wgm_ref[:, cols]))

    def gate_chain(hrows, mprev):
        n = hrows.shape[0]
        gt = _dot_nt(wif_ref[...], hrows) + bias_ref[...]
        pos = jnp.bitwise_and(lax.broadcasted_iota(jnp.int32, (8, n), 1), tt - 1)
        bcum = _seg_scan(_log_sigmoid(gt[8:16, :]), pos, tt, jnp.add, 0.0)
        g = gt[0:8, :] - bcum
        mrun = jnp.maximum(mprev, _seg_scan(g, pos, tt, jnp.maximum, NEG))
        mlast = _seg_bcast_last(mrun, pos, tt)
        winter = jnp.exp(mprev - mrun)
        efloor = jnp.exp(-(bcum + mrun))
        wrow = jnp.exp(g - mlast)
        decay = jnp.exp(mprev - mlast)
        mnew = _seg_bcast_last(bcum, pos, tt) + mlast
        zt = jnp.concatenate([mrun, winter, efloor, wrow], axis=0).T
        return g, decay, zt, mnew

    if not single_tile:
        pl.when(step == 0)(init_state)
        c_prev, n_prev, m_start = c_ref, n_ref, (lambda: m_ref[...])
    elif has_state:
        c_prev, n_prev, m_start = c0_ref, n0_ref, (lambda: m0_ref[...])
    else:
        c_ref[...] = jnp.zeros_like(c_ref)
        n_ref[...] = jnp.zeros_like(n_ref)
        c_prev, n_prev, m_start = c_ref, n_ref, (lambda: jnp.zeros(m_ref.shape, F32))

    if nsub == 1:
        base = 0
        h = normalise()
        hp = h if rp == r else jnp.concatenate([h, jnp.zeros((rp - r, D), BF16)], axis=0)
        g, decay, zt, mnew = gate_chain(hp, m_start())
        m_ref[...] = mnew
        project(h)
    else:
        base = pl.multiple_of(step * r, r)

        @pl.when(step == 0)
        def _block_prologue():
            hh = normalise()
            ga, da, za, mnew_all = gate_chain(hh, m_start())
            project(hh)
            m_ref[...] = mnew_all
            zt_scr[:, 0:32] = za
            for s in range(nsub):
                gd_scr[16 * s:16 * s + 8, 0:r] = ga[:, s * r:(s + 1) * r]
                gd_scr[16 * s + 8:16 * s + 16, 0:r] = da[:, s * r:(s + 1) * r]
            for piece in range(NH):
                merge_gate(piece)

        zt = zt_scr[pl.ds(base, r), 0:32]
        gd = gd_scr[pl.ds(pl.multiple_of(step * 16, 16), 16), :]
        g = gd[0:8, 0:r]
        decay = gd[8:16, 0:r]

    if tt <= SHORT_SEQ:
        nst = NH * r
        prow = slice(0, r) if nsub == 1 else pl.ds(base, r)
        stack = lambda off: jnp.concatenate(
            [p_scr[prow, off + hd * DH:off + (hd + 1) * DH] for hd in range(NH)], axis=0)
        qs = stack(0)
        ks = stack(D) * K_SCALE
        vs = stack(2 * D)
        col = lambda c0: jnp.concatenate([zt[0:r, c0 + hd:c0 + hd + 1] for hd in range(NH)], axis=0)
        mcol, wi, ef, wc = col(0), col(8), col(16), col(24)
        grow = jnp.concatenate([g[hd:hd + 1, 0:r] for hd in range(NH)], axis=1)
        row_i = lax.broadcasted_iota(jnp.int32, (nst, nst), 0)
        col_i = lax.broadcasted_iota(jnp.int32, (nst, nst), 1)
        shift = tt.bit_length() - 1
        same = lax.shift_right_logical(row_i, shift) == lax.shift_right_logical(col_i, shift)
        arg = jnp.where(col_i <= row_i, grow - mcol, NEG)
        dm = jnp.exp(jnp.where(same, arg, NEG))
        sc = _dot_nt(qs.astype(BF16), ks.astype(BF16)) * dm
        intra = _dot(sc.astype(BF16), vs.astype(BF16))
        pairs = [(hd, b) for hd in range(NH) for b in range(bt)]
        blk = lambda hd, b: slice(hd * r + b * tt, hd * r + (b + 1) * tt)
        inter = jnp.concatenate([_dot(qs[blk(hd, b)].astype(BF16), c_prev[b, hd].astype(BF16))
                                 for hd, b in pairs], axis=0)
        nfull = jnp.concatenate([jnp.broadcast_to(n_prev[b, hd:hd + 1, :], (tt, DH)) for hd, b in pairs], axis=0)
        num = wi * inter + intra
        den = wi * jnp.sum(qs * nfull, axis=-1, keepdims=True) + jnp.sum(sc, axis=-1, keepdims=True)
        hh = num * (1.0 / jnp.maximum(jnp.abs(den), ef))
        for hd in range(NH):
            hm_scr[prow, hd * DH:(hd + 1) * DH] = hh[hd * r:(hd + 1) * r]
        wk = ks * wc
        for hd, b in pairs:
            dec = decay[hd:hd + 1, b * tt:b * tt + 1]
            upd = _dot_tn(wk[blk(hd, b)].astype(BF16), vs[blk(hd, b)].astype(BF16))
            c_ref[b, hd] = dec * c_prev[b, hd] + upd
            n_ref[b, hd:hd + 1, :] = (dec * n_prev[b, hd:hd + 1, :]
                                      + jnp.sum(wk[blk(hd, b)], axis=0, keepdims=True))
        if nsub == 1:
            for piece in range(NH):
                merge_gate(piece)

    row_i = lax.broadcasted_iota(jnp.int32, (tt, tt), 0)
    col_i = lax.broadcasted_iota(jnp.int32, (tt, tt), 1)
    causal = row_i >= col_i

    for b in range(bt if tt > SHORT_SEQ else 0):
        rows = slice(b * tt, (b + 1) * tt)
        prow = rows if nsub == 1 else pl.ds(base + b * tt, tt)
        for hd in range(NH):
            cs = slice(hd * DH, (hd + 1) * DH)
            qf = p_scr[prow, cs]
            kf = p_scr[prow, D + hd * DH:D + (hd + 1) * DH] * K_SCALE
            vb = p_scr[prow, 2 * D + hd * DH:2 * D + (hd + 1) * DH].astype(BF16)
            qb = qf.astype(BF16)
            kb = kf.astype(BF16)
            mcol = zt[rows, hd:hd + 1]
            wi = zt[rows, 8 + hd:9 + hd]
            ef = zt[rows, 16 + hd:17 + hd]
            wc = zt[rows, 24 + hd:25 + hd]
            grow = g[hd:hd + 1, b * tt:(b + 1) * tt]
            dm = jnp.exp(jnp.where(causal, grow - mcol, NEG))
            sc = _dot_nt(qb, kb) * dm
            cf = c_prev[b, hd]
            nrow = n_prev[b, hd:hd + 1, :]
            num = wi * _dot(qb, cf.astype(BF16)) + _dot(sc.astype(BF16), vb)
            den = wi * jnp.sum(qf * nrow, axis=-1, keepdims=True) + jnp.sum(sc, axis=-1, keepdims=True)
            hm_scr[prow, cs] = num * (1.0 / jnp.maximum(jnp.abs(den), ef))
            dec = decay[hd:hd + 1, b * tt:b * tt + 1]
            wk = kf * wc
            c_ref[b, hd] = dec * cf + _dot_tn(wk.astype(BF16), vb)
            n_ref[b, hd:hd + 1, :] = dec * nrow + jnp.sum(wk, axis=0, keepdims=True)
            done = b * NH + hd + 1
            if nsub == 1 and (done * NH) % (bt * NH) == 0:
                merge_gate(done * NH // (bt * NH) - 1)

    def epilogue():
        rc = min(ROW_CHUNK, ro)
        hnw = hnw_ref[...]
        for ci in range(ro // rc):
            rows = slice(ci * rc, (ci + 1) * rc)
            hmv = hm_scr[rows, :] * _sigmoid(p_scr[rows, 3 * D:4 * D])
            parts = []
            for hd in range(NH):
                hh = hmv[:, hd * DH:(hd + 1) * DH]
                mu = jnp.mean(hh, axis=-1, keepdims=True)
                dv = hh - mu
                var = jnp.mean(dv * dv, axis=-1, keepdims=True)
                parts.append(dv * lax.rsqrt(var + EPS) * hnw[:, hd * DH:(hd + 1) * DH])
            hn = jnp.concatenate(parts, axis=1)
            tm = hn * _silu(p_scr[rows, 4 * D:5 * D])
            ym = _dot(tm.astype(BF16), wpm_ref[...])
            nb = rc // tt if rc >= tt else 0
            if nb:
                msc = msc_ref[ci * nb:(ci + 1) * nb].reshape(rc, D)
                xr = x_ref[ci * nb:(ci + 1) * nb].reshape(rc, D)
            else:
                per = tt // rc
                msc = msc_ref[ci // per, (ci % per) * rc:(ci % per + 1) * rc, :]
                xr = x_ref[ci // per, (ci % per) * rc:(ci % per + 1) * rc, :]
            merged = msc + sg_scr[rows, :] * ym
            y = xr + _dot(merged.astype(BF16), wout_ref[...])
            if final:
                y = _rmsnorm(y, fnw_ref[...])
            if nb:
                y_ref[ci * nb:(ci + 1) * nb] = y.reshape(nb, tt, D)
            else:
                y_ref[ci // per, (ci % per) * rc:(ci % per + 1) * rc, :] = y

    if nsub == 1:
        epilogue()
    else:
        pl.when(step == nsub - 1)(epilogue)


def _mlstm_call(layer, depth, x, msc, weights, state, c_stack, bt, tt):
    b, t, _ = x.shape
    r = bt * tt
    rp = max(r, 128)
    nbt = b // bt
    has_state = state is not None
    has_alias = c_stack is not None
    final = layer == depth - 1
    nsub = max(1, min(nbt, PROJ_ROWS // r)) if t == tt else 1
    assert nbt % nsub == 0
    ro = r * nsub
    rpo = max(ro, 128)
    if nsub == 1:
        grid = (nbt, t // tt)
        tile_idx = lambda bi, si: (bi, si, 0)
        sub_idx = lambda bi, si: bi
    else:
        grid = (nbt // nsub, nsub)
        tile_idx = lambda bi, si: (bi, 0, 0)
        sub_idx = lambda bi, si: bi * nsub + si
    wspec = lambda shape: pl.BlockSpec((None,) + shape, lambda bi, si: (layer,) + (0,) * len(shape),
                                       pipeline_mode=pl.Buffered(1))
    tile = pl.BlockSpec((bt * nsub, tt, D), tile_idx)
    wcol = lambda k: pl.BlockSpec((None, D, D), lambda bi, si: (layer, 0, k), pipeline_mode=pl.Buffered(1))
    nw, w_main, w_if, bias, w_gates, hnw, wpm, wout, fnw = weights
    in_specs = ([tile, tile, wspec((1, D))] + [wcol(OFF_M // D + k) for k in range(5)]
                + [wspec((16, D)), wspec((16, 1)), wcol(2), wspec((1, D)), wspec((D, D)), wspec((D, D)),
                   pl.BlockSpec((1, D), lambda bi, si: (0, 0), pipeline_mode=pl.Buffered(1))])
    args = [x, msc, nw] + [w_main] * 5 + [w_if, bias, w_gates, hnw, wpm, wout, fnw]
    if has_state:
        in_specs += [pl.BlockSpec((None, bt, NH, DH, DH), lambda bi, si: (layer, sub_idx(bi, si), 0, 0, 0)),
                     pl.BlockSpec((None, bt, NH, DH), lambda bi, si: (layer, sub_idx(bi, si), 0, 0)),
                     pl.BlockSpec((None, None, 8, rpo), lambda bi, si: (layer, bi, 0, 0))]
        c0, n0, m0 = state
        args += [c0, n0, _expand_m(m0, bt * nsub, tt)]
    aliases = {}
    if has_alias:
        aliases = {len(args): 1}
        in_specs.append(pl.BlockSpec(memory_space=pl.ANY))
        args.append(c_stack)
    out_specs = [tile,
                 pl.BlockSpec((None, bt, NH, DH, DH), lambda bi, si: (layer, sub_idx(bi, si), 0, 0, 0)),
                 pl.BlockSpec((bt, NH, DH), lambda bi, si: (sub_idx(bi, si), 0, 0)),
                 pl.BlockSpec((None, 8, rpo), lambda bi, si: (bi, 0, 0))]
    out_shape = [jax.ShapeDtypeStruct((b, t, D), F32), jax.ShapeDtypeStruct((depth, b, NH, DH, DH), F32),
                 jax.ShapeDtypeStruct((b, NH, DH), F32), jax.ShapeDtypeStruct((nbt // nsub, 8, rpo), F32)]
    scratch = [pltpu.VMEM((ro, D), BF16),
               pltpu.VMEM((ro, 5 * D), F32),
               pltpu.VMEM((ro, D), F32),
               pltpu.VMEM((ro, D), F32)]
    if nsub > 1:
        scratch += [pltpu.VMEM((ro, 128), F32),
                    pltpu.VMEM((16 * nsub, 128), F32)]
    y, c_new, n_new, m_rows = pl.pallas_call(
        functools.partial(_mlstm_kernel, has_state, has_alias, final, t == tt, bt, tt, nsub),
        grid=grid,
        in_specs=in_specs, out_specs=out_specs, out_shape=out_shape, scratch_shapes=scratch,
        input_output_aliases=aliases,
        compiler_params=pltpu.CompilerParams(dimension_semantics=("arbitrary", "arbitrary"),
                                             vmem_limit_bytes=VMEM_LIMIT),
        name="mlstm_state" if has_state else "mlstm",
    )(*args)
    return y, c_new, n_new, _collapse_m(m_rows, bt * nsub, tt)


def _expand_m(m, bt, tt):
    depth, b, _ = m.shape
    r = bt * tt
    rp = max(r, 128)
    v = jnp.repeat(m.reshape(depth, b // bt, bt, NH).transpose(0, 1, 3, 2), tt, axis=-1)
    return jnp.pad(v, ((0, 0), (0, 0), (0, 8 - NH), (0, rp - r)))


def _collapse_m(mrows, bt, tt):
    nbt = mrows.shape[0]
    return mrows[:, :NH, 0:bt * tt:tt].transpose(0, 2, 1).reshape(nbt * bt, NH)


def kernel(x_prompt, x_sample, state_ssm_re, state_ssm_im, state_conv, state_mlstm_c, state_mlstm_n, state_mlstm_m, norm_w, w_in, i_bias, f_bias, lam_re, lam_im, log_dt, b_re, b_im, c_re, c_im, d_skip, w_glu, w_proj_s, conv_w, w_proj_c, mlstm_norm_w, w_proj_m, w_out, final_norm_w):
    depth = norm_w.shape[0]
    bp, tp, _ = x_prompt.shape
    bs, ts, _ = x_sample.shape
    assert tp % PROMPT_CHUNK == 0 and ts % 8 == 0 and ts & (ts - 1) == 0 and bp % 8 == 0 and bs % 64 == 0

    a_re, a_im, bb_re, bb_im, cp_re, cp_im = _s5_discretise(lam_re, lam_im, log_dt, b_re, b_im, c_re, c_im)
    row = lambda a: a.reshape(depth, 1, a.shape[-1])
    nw = row(norm_w)
    w_t = jnp.swapaxes(w_in, 1, 2)
    w_main = _cast_columns(w_t, 0, OFF_IF // D)
    w_gates = _cast_columns(w_t, OFF_G, 3)
    gate_rows = lambda a: jnp.pad(a.reshape(depth, 2, NH, -1), ((0, 0), (0, 0), (0, 8 - NH), (0, 0))).reshape(depth, 16, -1)
    w_if = gate_rows(w_t[:, OFF_IF:OFF_G, :]).astype(BF16)
    bias = gate_rows(jnp.concatenate([i_bias, f_bias], axis=-1)[:, :, None])
    wa = [nw, w_main, w_gates, a_re, a_im, bb_re, bb_im, cp_re, cp_im, row(d_skip),
          w_glu.astype(BF16), w_proj_s.astype(BF16), conv_w, w_proj_c.astype(BF16)]
    wb = [nw, w_main, w_if, bias, w_gates, row(mlstm_norm_w), w_proj_m.astype(BF16), w_out.astype(BF16),
          final_norm_w.reshape(1, D)]

    sb_a, sb_b = 64, 4
    pb_a, pb_b = 8, 2
    st_a = (state_ssm_re.reshape(depth, bs, S5N), state_ssm_im.reshape(depth, bs, S5N), state_conv)
    st_b = (state_mlstm_c, state_mlstm_n, state_mlstm_m)

    yp, ys = x_prompt, x_sample
    cp = cs = None
    outs_p = [[] for _ in range(5)]
    outs_s = [[] for _ in range(5)]
    for l in range(depth):
        msc, sre, sim, cv = _s5conv_call(l, yp, wa, None, pb_a, S5_TILE_T)
        yp, cp, nn, mm = _mlstm_call(l, depth, yp, msc, wb, None, cp, pb_b, PROMPT_CHUNK)
        for i, v in enumerate((sre.reshape(bp, S5G, S5P), sim.reshape(bp, S5G, S5P), cv, nn, mm)):
            outs_p[i].append(v)
        msc, sre, sim, cv = _s5conv_call(l, ys, wa, st_a, sb_a, ts)
        ys, cs, nn, mm = _mlstm_call(l, depth, ys, msc, wb, st_b, cs, sb_b, ts)
        for i, v in enumerate((sre.reshape(bs, S5G, S5P), sim.reshape(bs, S5G, S5P), cv, nn, mm)):
            outs_s[i].append(v)
    sp = [jnp.stack(o) for o in outs_p]
    ss = [jnp.stack(o) for o in outs_s]
    return (yp, ys, sp[0], ss[0], sp[1], ss[1], sp[2], ss[2], cp, cs, sp[3], ss[3], sp[4], ss[4])
```

```python
import functools
import math

import jax
import jax.numpy as jnp
from jax import lax
from jax.experimental import pallas as pl
from jax.experimental.pallas import tpu as pltpu

D = 1024
S5W = 512
S5G = 32
S5P = 64
S5C = 16
S5N = S5G * S5P
CW = 512
CK = 3
NH = 4
DH = 256
PROMPT_CHUNK = 256
S5_TILE_T = 64
SHORT_SEQ = 32
EPS = 1e-6
NEG = -1e30
K_SCALE = DH ** -0.5

OFF_S = 0
OFF_M = 3072
OFF_IF = 8192
OFF_G = 8200
IN_W = 11272

F32 = jnp.float32
BF16 = jnp.bfloat16
VMEM_LIMIT = 58 * 1024 * 1024
ROW_CHUNK = 256
GATE_PIECES = 4
MLSTM_ROW_CHUNK = 512
PROJ_ROWS = 256


def _sigmoid(x):
    return 1.0 / (1.0 + jnp.exp(-x))


def _silu(x):
    return x * _sigmoid(x)


def _gelu_tanh(x):
    return 0.5 * x * (1.0 + jnp.tanh(math.sqrt(2.0 / math.pi) * (x + 0.044715 * (x * x * x))))


def _log_sigmoid(x):
    return jnp.minimum(x, 0.0) - jnp.log1p(jnp.exp(-jnp.abs(x)))


def _rmsnorm(x, w):
    return x * lax.rsqrt(jnp.mean(x * x, axis=-1, keepdims=True) + EPS) * w


def _dot(a, b):
    return jnp.dot(a, b, preferred_element_type=F32)


def _dot_nt(a, b):
    return lax.dot_general(a, b, (((1,), (1,)), ((), ())), preferred_element_type=F32)


def _dot_tn(a, b):
    return lax.dot_general(a, b, (((0,), (0,)), ((), ())), preferred_element_type=F32)


def _pitch(n):
    p = n + 8
    return p if (p // 8) % 2 == 1 else p + 8


def _s5_disc_kernel(lre_ref, lim_ref, ldt_ref, bre_ref, bim_ref, cre_ref, cim_ref,
                    are_ref, aim_ref, bbre_ref, bbim_ref, cpre_ref, cpim_ref):
    lr = lre_ref[...]
    li = lim_ref[...]
    dt = jnp.exp(ldt_ref[...])
    ea = jnp.exp(lr * dt)
    ar = ea * jnp.cos(li * dt)
    ai = ea * jnp.sin(li * dt)
    are_ref[...] = ar
    aim_ref[...] = ai
    nr = ar - 1.0
    inv = 1.0 / (lr * lr + li * li)
    cr = (nr * lr + ai * li) * inv
    ci = (ai * lr - nr * li) * inv
    half = S5N // 2
    shr = lax.shift_right_logical
    bdiag = (shr(lax.broadcasted_iota(jnp.int32, (256, half), 0), 4)
             == shr(lax.broadcasted_iota(jnp.int32, (256, half), 1), 6))
    cdiag = (shr(lax.broadcasted_iota(jnp.int32, (half, 256), 0), 6)
             == shr(lax.broadcasted_iota(jnp.int32, (half, 256), 1), 4))
    rep = lambda x: jnp.concatenate([x] * 16, axis=0)
    for h in range(2):
        crh = cr[:, h * half:(h + 1) * half]
        cih = ci[:, h * half:(h + 1) * half]
        br = bre_ref[h]
        bi = bim_ref[h]
        bbre_ref[h] = jnp.where(bdiag, rep(crh * br - cih * bi), 0.0).astype(BF16)
        bbim_ref[h] = jnp.where(bdiag, rep(crh * bi + cih * br), 0.0).astype(BF16)
        cpre_ref[h] = jnp.where(cdiag, rep(cre_ref[h]), 0.0).astype(BF16)
        cpim_ref[h] = jnp.where(cdiag, rep(cim_ref[h]), 0.0).astype(BF16)


def _s5_discretise(lam_re, lam_im, log_dt, b_re, b_im, c_re, c_im):
    depth = lam_re.shape[0]

    def rows_b(b):
        return jnp.swapaxes(b.reshape(depth, 2, 16 * S5P, S5C), 2, 3)

    def rows_c(c):
        return jnp.swapaxes(c.reshape(depth, 2, 16 * S5C, S5P), 2, 3)

    row = lambda a: a.reshape(depth, 1, S5N)
    ldt = jnp.broadcast_to(log_dt[:, :, None], (depth, S5G, S5P))
    vec = pl.BlockSpec((None, 1, S5N), lambda l: (l, 0, 0))
    bmat = pl.BlockSpec((None, 2, 256, 1024), lambda l: (l, 0, 0, 0))
    cmat = pl.BlockSpec((None, 2, 1024, 256), lambda l: (l, 0, 0, 0))
    brow = pl.BlockSpec((None, 2, S5C, 1024), lambda l: (l, 0, 0, 0))
    crow = pl.BlockSpec((None, 2, S5P, 256), lambda l: (l, 0, 0, 0))
    return pl.pallas_call(
        _s5_disc_kernel,
        grid=(depth,),
        in_specs=[vec, vec, vec, brow, brow, crow, crow],
        out_specs=[vec, vec, bmat, bmat, cmat, cmat],
        out_shape=[jax.ShapeDtypeStruct((depth, 1, S5N), F32), jax.ShapeDtypeStruct((depth, 1, S5N), F32),
                   jax.ShapeDtypeStruct((depth, 2, 256, 1024), BF16),
                   jax.ShapeDtypeStruct((depth, 2, 256, 1024), BF16),
                   jax.ShapeDtypeStruct((depth, 2, 1024, 256), BF16),
                   jax.ShapeDtypeStruct((depth, 2, 1024, 256), BF16)],
        name="s5_discretise",
    )(row(lam_re), row(lam_im), row(ldt), rows_b(b_re), rows_b(b_im), rows_c(c_re), rows_c(c_im))


def _cast_kernel(shift, *refs):
    if shift:
        wa_ref, wb_ref, o_ref = refs
        w = jnp.concatenate([wa_ref[shift:, :], wb_ref[...]], axis=0)
    else:
        w_ref, o_ref = refs
        w = w_ref[...]
    o_ref[...] = w.T.astype(BF16)


def _cast_columns(w_t, first_col, n_blocks):
    depth = w_t.shape[0]
    k0, shift = divmod(first_col, D)
    assert shift % 8 == 0
    in_specs = [pl.BlockSpec((None, D, D), lambda l, k: (l, k0 + k, 0))]
    args = [w_t]
    if shift:
        assert D % shift == 0
        in_specs.append(pl.BlockSpec((None, shift, D), lambda l, k: (l, (k0 + k + 1) * (D // shift), 0)))
        args.append(w_t)
    return pl.pallas_call(
        functools.partial(_cast_kernel, shift),
        grid=(depth, n_blocks),
        in_specs=in_specs,
        out_specs=pl.BlockSpec((None, D, D), lambda l, k: (l, 0, k)),
        out_shape=jax.ShapeDtypeStruct((depth, D, n_blocks * D), BF16),
        name="cast_weights",
    )(*args)


def _s5conv_kernel(has_state, bt, tt, *refs):
    refs = list(refs)
    (x_ref, nw_ref, ws_ref, wg_ref, are_ref, aim_ref, bbre_ref, bbim_ref, cre_ref, cim_ref,
     dsk_ref, wglu_ref, wps_ref, cw_ref, wpc_ref) = refs[:15]
    refs = refs[15:]
    if has_state:
        s0re_ref, s0im_ref, cv0_ref = refs[:3]
        refs = refs[3:]
    msc_ref, sre_ref, sim_ref, cvo_ref = refs[:4]
    h_scr, ps_scr, g_scr, uslab, utb, bu_scr, yslab, y_scr, vhalo = refs[4:]

    r = bt * tt
    pt = _pitch(tt)
    pb = _pitch(bt)
    rc = min(ROW_CHUNK, r)
    nchunk = r // rc
    ti = pl.program_id(1)

    def merge_gates(ci):
        rows = slice(ci * rc, (ci + 1) * rc)
        g_scr[rows, :] = _sigmoid(_dot(h_scr[rows, :], wg_ref[...]))

    @pl.when(ti == 0)
    def _init():
        if has_state:
            sre_ref[...] = s0re_ref[...]
            sim_ref[...] = s0im_ref[...]
            vhalo[:, 6:8, :] = cv0_ref[...]
        else:
            sre_ref[...] = jnp.zeros_like(sre_ref)
            sim_ref[...] = jnp.zeros_like(sim_ref)
            vhalo[:, 6:8, :] = jnp.zeros((bt, 2, CW), F32)

    x = x_ref[...].reshape(r, D)
    h = _rmsnorm(x, nw_ref[...]).astype(BF16)
    h_scr[...] = h
    ps_scr[...] = _dot(h, ws_ref[...])


    for b in range(bt):
        for j in range(4):
            uslab[j, b * pt:b * pt + tt, :] = ps_scr[b * tt:(b + 1) * tt, j * 128:(j + 1) * 128]
    for t in range(tt):
        for bs in range(bt // 8):
            dst = t * bt + bs * 8
            for j in range(4):
                utb[dst:dst + 8, j * 128:(j + 1) * 128] = uslab[j, pl.ds(bs * 8 * pt + t, 8, stride=pt), :]
    for ci in range(nchunk // 2):
        merge_gates(ci)

    for hf in range(2):
        uh = utb[:, hf * 256:(hf + 1) * 256].astype(BF16)
        bu_scr[:, hf * 1024:(hf + 1) * 1024] = _dot(uh, bbre_ref[hf])
        bu_scr[:, S5N + hf * 1024:S5N + (hf + 1) * 1024] = _dot(uh, bbim_ref[hf])
    for ci in range(nchunk // 2, nchunk):
        merge_gates(ci)

    v = ps_scr[:, 1536:2048] * ps_scr[:, 2048:2560]
    vhalo[:, 8:8 + tt, :] = v.reshape(bt, tt, CW)
    cw = cw_ref[...]
    yc = (cw[0:1, :] * vhalo[:, 6:6 + tt, :] + cw[1:2, :] * vhalo[:, 7:7 + tt, :]
          + cw[2:3, :] * vhalo[:, 8:8 + tt, :])
    ps_scr[:, 2048:2560] = yc.reshape(r, CW)
    new_halo = vhalo[:, tt + 6:tt + 8, :]
    vhalo[:, 6:8, :] = new_halo
    cvo_ref[...] = new_halo

    for ci in range(nchunk):
        rows = slice(ci * rc, (ci + 1) * rc)
        tc_ = ps_scr[rows, 1024:1536] * ps_scr[rows, 2048:2560] * _silu(ps_scr[rows, 2560:3072])
        ps_scr[rows, 1024:2048] = g_scr[rows, D:] * _dot(tc_.astype(BF16), wpc_ref[...])

    for q in range(4):
        c_re = slice(q * 512, (q + 1) * 512)
        c_im = slice(S5N + q * 512, S5N + (q + 1) * 512)
        ar = jnp.broadcast_to(are_ref[:, c_re], (8, 512))
        ai = jnp.broadcast_to(aim_ref[:, c_re], (8, 512))
        for bs in range(bt // 8):
            r0 = bs * 8
            sr = sre_ref[r0:r0 + 8, c_re]
            si = sim_ref[r0:r0 + 8, c_re]
            for t in range(tt):
                row = t * bt + r0
                nr = ar * sr - ai * si + bu_scr[row:row + 8, c_re]
                ni = ar * si + ai * sr + bu_scr[row:row + 8, c_im]
                bu_scr[row:row + 8, c_re] = nr
                bu_scr[row:row + 8, c_im] = ni
                sr, si = nr, ni
            sre_ref[r0:r0 + 8, c_re] = sr
            sim_ref[r0:r0 + 8, c_re] = si

    for m in range(2):
        sre = bu_scr[:, m * 1024:(m + 1) * 1024].astype(BF16)
        sim = bu_scr[:, S5N + m * 1024:S5N + (m + 1) * 1024].astype(BF16)
        ytb = _dot(sre, cre_ref[m]) - _dot(sim, cim_ref[m])
        for t in range(tt):
            for jj in range(2):
                yslab[2 * m + jj, t * pb:t * pb + bt, :] = ytb[t * bt:(t + 1) * bt, jj * 128:(jj + 1) * 128]
    for b in range(bt):
        for ts in range(tt // 8):
            dst = b * tt + ts * 8
            for j in range(4):
                y_scr[dst:dst + 8, j * 128:(j + 1) * 128] = yslab[j, pl.ds(ts * 8 * pb + b, 8, stride=pb), :]

    nb = rc // tt if rc >= tt else 0
    dsk = dsk_ref[...]
    for ci in range(nchunk):
        rows = slice(ci * rc, (ci + 1) * rc)
        y = y_scr[rows, :] + dsk * ps_scr[rows, 0:512]
        gl = _gelu_tanh(y)
        glu = gl * _sigmoid(_dot(gl.astype(BF16), wglu_ref[...]))
        ts_ = glu * _silu(ps_scr[rows, 512:1024])
        ys = _dot(ts_.astype(BF16), wps_ref[...])
        out = g_scr[rows, :D] * ys + ps_scr[rows, 1024:2048]
        if nb:
            msc_ref[ci * nb:(ci + 1) * nb] = out.reshape(nb, tt, D)
        else:
            per = tt // rc
            msc_ref[ci // per, (ci % per) * rc:(ci % per + 1) * rc, :] = out


def _s5conv_call(layer, x, weights, state, bt, tt):
    b, t, _ = x.shape
    r = bt * tt
    has_state = state is not None
    wspec = lambda shape: pl.BlockSpec((None,) + shape, lambda bi, ti: (layer,) + (0,) * len(shape),
                                       pipeline_mode=pl.Buffered(1))
    tile = pl.BlockSpec((bt, tt, D), lambda bi, ti: (bi, ti, 0))
    in_specs = [tile, wspec((1, D)), wspec((D, 3072)), wspec((D, 2048)), wspec((1, S5N)), wspec((1, S5N)),
                wspec((2, 256, 1024)), wspec((2, 256, 1024)), wspec((2, 1024, 256)), wspec((2, 1024, 256)),
                wspec((1, S5W)), wspec((S5W, S5W)), wspec((S5W, D)), wspec((CK, CW)), wspec((CW, D))]
    args = [x] + list(weights)
    if has_state:
        in_specs += [pl.BlockSpec((None, bt, S5N), lambda bi, ti: (layer, bi, 0)),
                     pl.BlockSpec((None, bt, S5N), lambda bi, ti: (layer, bi, 0)),
                     pl.BlockSpec((None, bt, CK - 1, CW), lambda bi, ti: (layer, bi, 0, 0))]
        args += list(state)
    out_specs = [tile,
                 pl.BlockSpec((bt, S5N), lambda bi, ti: (bi, 0)),
                 pl.BlockSpec((bt, S5N), lambda bi, ti: (bi, 0)),
                 pl.BlockSpec((bt, CK - 1, CW), lambda bi, ti: (bi, 0, 0))]
    out_shape = [jax.ShapeDtypeStruct((b, t, D), F32), jax.ShapeDtypeStruct((b, S5N), F32),
                 jax.ShapeDtypeStruct((b, S5N), F32), jax.ShapeDtypeStruct((b, CK - 1, CW), F32)]
    scratch = [pltpu.VMEM((r, D), BF16),
               pltpu.VMEM((r, 3072), F32),
               pltpu.VMEM((r, 2 * D), F32),
               pltpu.VMEM((4, bt * _pitch(tt), 128), F32),
               pltpu.VMEM((r, S5W), F32),
               pltpu.VMEM((r, 2 * S5N), F32),
               pltpu.VMEM((4, tt * _pitch(bt), 128), F32),
               pltpu.VMEM((r, S5W), F32),
               pltpu.VMEM((bt, tt + 8, CW), F32)]
    return pl.pallas_call(
        functools.partial(_s5conv_kernel, has_state, bt, tt),
        grid=(b // bt, t // tt),
        in_specs=in_specs, out_specs=out_specs, out_shape=out_shape, scratch_shapes=scratch,
        compiler_params=pltpu.CompilerParams(dimension_semantics=("arbitrary", "arbitrary"),
                                             vmem_limit_bytes=VMEM_LIMIT),
        name="s5conv_state" if has_state else "s5conv",
    )(*args)


def _seg_scan(x, pos, seg, op, fill):
    s = 1
    while s < seg:
        x = op(x, jnp.where(pos >= s, pltpu.roll(x, s, 1), fill))
        s *= 2
    return x


def _seg_bcast_last(x, pos, seg):
    n = x.shape[1]
    s = 1
    while s < seg:
        x = jnp.where(pos + s <= seg - 1, pltpu.roll(x, n - s, 1), x)
        s *= 2
    return x


def _mlstm_kernel(has_state, has_alias, final, single_tile, bt, tt, nsub, *refs):
    refs = list(refs)
    x_ref, msc_ref, nw_ref = refs[:3]
    wm_refs = refs[3:8]
    wif_ref, bias_ref, wgm_ref, hnw_ref, wpm_ref, wout_ref, fnw_ref = refs[8:15]
    refs = refs[15:]
    if has_state:
        c0_ref, n0_ref, m0_ref = refs[:3]
        refs = refs[3:]
    if has_alias:
        refs = refs[1:]
    y_ref, c_ref, n_ref, m_ref = refs[:4]
    h_scr, p_scr, hm_scr, sg_scr = refs[4:8]
    if nsub > 1:
        zt_scr, gd_scr = refs[8:]

    r = bt * tt
    ro = r * nsub
    rp = max(r, 128)
    step = pl.program_id(1)

    def init_state():
        if has_state:
            c_ref[...] = c0_ref[...]
            n_ref[...] = n0_ref[...]
            m_ref[...] = m0_ref[...]
        else:
            c_ref[...] = jnp.zeros_like(c_ref)
            n_ref[...] = jnp.zeros_like(n_ref)
            m_ref[...] = jnp.zeros_like(m_ref)

    def normalise():
        hh = _rmsnorm(x_ref[...].reshape(ro, D), nw_ref[...]).astype(BF16)
        h_scr[...] = hh
        return hh

    def project(hh):
        for k, w_ref in enumerate(wm_refs):
            p_scr[:, k * D:(k + 1) * D] = _dot(hh, w_ref[...])

    def merge_gate(piece):
        cols = slice(piece * (D // GATE_PIECES), (piece + 1) * (D // GATE_PIECES))
        sg_scr[:, cols] = _sigmoid(_dot(h_scr[...], wgm_ref[:, cols]))

    def gate_chain(hrows, mprev):
        n = hrows.shape[0]
        gt = _dot_nt(wif_ref[...], hrows) + bias_ref[...]
        pos = jnp.bitwise_and(lax.broadcasted_iota(jnp.int32, (8, n), 1), tt - 1)
        bcum = _seg_scan(_log_sigmoid(gt[8:16, :]), pos, tt, jnp.add, 0.0)
        g = gt[0:8, :] - bcum
        mrun = jnp.maximum(mprev, _seg_scan(g, pos, tt, jnp.maximum, NEG))
        mlast = _seg_bcast_last(mrun, pos, tt)
        winter = jnp.exp(mprev - mrun)
        efloor = jnp.exp(-(bcum + mrun))
        wrow = jnp.exp(g - mlast)
        decay = jnp.exp(mprev - mlast)
        mnew = _seg_bcast_last(bcum, pos, tt) + mlast
        zt = jnp.concatenate([mrun, winter, efloor, wrow], axis=0).T
        return g, decay, zt, mnew

    if not single_tile:
        pl.when(step == 0)(init_state)
        c_prev, n_prev, m_start = c_ref, n_ref, (lambda: m_ref[...])
    elif has_state:
        c_prev, n_prev, m_start = c0_ref, n0_ref, (lambda: m0_ref[...])
    else:
        c_ref[...] = jnp.zeros_like(c_ref)
        n_ref[...] = jnp.zeros_like(n_ref)
        c_prev, n_prev, m_start = c_ref, n_ref, (lambda: jnp.zeros(m_ref.shape, F32))

    if nsub == 1:
        base = 0
        h = normalise()
        hp = h if rp == r else jnp.concatenate([h, jnp.zeros((rp - r, D), BF16)], axis=0)
        g, decay, zt, mnew = gate_chain(hp, m_start())
        m_ref[...] = mnew
        project(h)
    else:
        base = pl.multiple_of(step * r, r)

        @pl.when(step == 0)
        def _block_prologue():
            hh = normalise()
            ga, da, za, mnew_all = gate_chain(hh, m_start())
            project(hh)
            m_ref[...] = mnew_all
            zt_scr[:, 0:32] = za
            for s in range(nsub):
                gd_scr[16 * s:16 * s + 8, 0:r] = ga[:, s * r:(s + 1) * r]
                gd_scr[16 * s + 8:16 * s + 16, 0:r] = da[:, s * r:(s + 1) * r]
            for piece in range(GATE_PIECES):
                merge_gate(piece)

        zt = zt_scr[pl.ds(base, r), 0:32]
        gd = gd_scr[pl.ds(pl.multiple_of(step * 16, 16), 16), :]
        g = gd[0:8, 0:r]
        decay = gd[8:16, 0:r]

    if tt <= SHORT_SEQ:
        nst = NH * r
        prow = slice(0, r) if nsub == 1 else pl.ds(base, r)
        stack = lambda off: jnp.concatenate(
            [p_scr[prow, off + hd * DH:off + (hd + 1) * DH] for hd in range(NH)], axis=0)
        qs = stack(0)
        ks = stack(D) * K_SCALE
        vs = stack(2 * D)
        col = lambda c0: jnp.concatenate([zt[0:r, c0 + hd:c0 + hd + 1] for hd in range(NH)], axis=0)
        mcol, wi, ef, wc = col(0), col(8), col(16), col(24)
        grow = jnp.concatenate([g[hd:hd + 1, 0:r] for hd in range(NH)], axis=1)
        row_i = lax.broadcasted_iota(jnp.int32, (nst, nst), 0)
        col_i = lax.broadcasted_iota(jnp.int32, (nst, nst), 1)
        shift = tt.bit_length() - 1
        same = lax.shift_right_logical(row_i, shift) == lax.shift_right_logical(col_i, shift)
        arg = jnp.where(col_i <= row_i, grow - mcol, NEG)
        dm = jnp.exp(jnp.where(same, arg, NEG))
        sc = _dot_nt(qs.astype(BF16), ks.astype(BF16)) * dm
        intra = _dot(sc.astype(BF16), vs.astype(BF16))
        pairs = [(hd, b) for hd in range(NH) for b in range(bt)]
        blk = lambda hd, b: slice(hd * r + b * tt, hd * r + (b + 1) * tt)
        inter = jnp.concatenate([_dot(qs[blk(hd, b)].astype(BF16), c_prev[b, hd].astype(BF16))
                                 for hd, b in pairs], axis=0)
        nfull = jnp.concatenate([jnp.broadcast_to(n_prev[b, hd:hd + 1, :], (tt, DH)) for hd, b in pairs], axis=0)
        num = wi * inter + intra
        den = wi * jnp.sum(qs * nfull, axis=-1, keepdims=True) + jnp.sum(sc, axis=-1, keepdims=True)
        hh = num * (1.0 / jnp.maximum(jnp.abs(den), ef))
        for hd in range(NH):
            hm_scr[prow, hd * DH:(hd + 1) * DH] = hh[hd * r:(hd + 1) * r]
        wk = ks * wc
        for hd, b in pairs:
            dec = decay[hd:hd + 1, b * tt:b * tt + 1]
            upd = _dot_tn(wk[blk(hd, b)].astype(BF16), vs[blk(hd, b)].astype(BF16))
            c_ref[b, hd] = dec * c_prev[b, hd] + upd
            n_ref[b, hd:hd + 1, :] = (dec * n_prev[b, hd:hd + 1, :]
                                      + jnp.sum(wk[blk(hd, b)], axis=0, keepdims=True))
        if nsub == 1:
            for piece in range(GATE_PIECES):
                merge_gate(piece)

    row_i = lax.broadcasted_iota(jnp.int32, (tt, tt), 0)
    col_i = lax.broadcasted_iota(jnp.int32, (tt, tt), 1)
    causal = row_i >= col_i

    for b in range(bt if tt > SHORT_SEQ else 0):
        rows = slice(b * tt, (b + 1) * tt)
        prow = rows if nsub == 1 else pl.ds(base + b * tt, tt)
        for hd in range(NH):
            cs = slice(hd * DH, (hd + 1) * DH)
            qf = p_scr[prow, cs]
            kf = p_scr[prow, D + hd * DH:D + (hd + 1) * DH] * K_SCALE
            vb = p_scr[prow, 2 * D + hd * DH:2 * D + (hd + 1) * DH].astype(BF16)
            qb = qf.astype(BF16)
            kb = kf.astype(BF16)
            mcol = zt[rows, hd:hd + 1]
            wi = zt[rows, 8 + hd:9 + hd]
            ef = zt[rows, 16 + hd:17 + hd]
            wc = zt[rows, 24 + hd:25 + hd]
            grow = g[hd:hd + 1, b * tt:(b + 1) * tt]
            dm = jnp.exp(jnp.where(causal, grow - mcol, NEG))
            s_raw = _dot_nt(qb, kb)
            cf = c_prev[b, hd]
            nrow = n_prev[b, hd:hd + 1, :]
            inter = _dot(qb, cf.astype(BF16))
            dec = decay[hd:hd + 1, b * tt:b * tt + 1]
            wk = kf * wc
            c_ref[b, hd] = dec * cf + _dot_tn(wk.astype(BF16), vb)
            n_ref[b, hd:hd + 1, :] = dec * nrow + jnp.sum(wk, axis=0, keepdims=True)
            sc = s_raw * dm
            num = wi * inter + _dot(sc.astype(BF16), vb)
            den = wi * jnp.sum(qf * nrow, axis=-1, keepdims=True) + jnp.sum(sc, axis=-1, keepdims=True)
            hm_scr[prow, cs] = num * (1.0 / jnp.maximum(jnp.abs(den), ef))
            done = b * NH + hd + 1
            if nsub == 1 and (done * GATE_PIECES) % (bt * NH) == 0:
                merge_gate(done * GATE_PIECES // (bt * NH) - 1)

    def epilogue():
        rc = min(MLSTM_ROW_CHUNK, ro)
        hnw = hnw_ref[...]
        for ci in range(ro // rc):
            rows = slice(ci * rc, (ci + 1) * rc)
            hmv = hm_scr[rows, :] * _sigmoid(p_scr[rows, 3 * D:4 * D])
            parts = []
            for hd in range(NH):
                hh = hmv[:, hd * DH:(hd + 1) * DH]
                mu = jnp.mean(hh, axis=-1, keepdims=True)
                dv = hh - mu
                var = jnp.mean(dv * dv, axis=-1, keepdims=True)
                parts.append(dv * lax.rsqrt(var + EPS) * hnw[:, hd * DH:(hd + 1) * DH])
            hn = jnp.concatenate(parts, axis=1)
            tm = hn * _silu(p_scr[rows, 4 * D:5 * D])
            ym = _dot(tm.astype(BF16), wpm_ref[...])
            nb = rc // tt if rc >= tt else 0
            if nb:
                msc = msc_ref[ci * nb:(ci + 1) * nb].reshape(rc, D)
                xr = x_ref[ci * nb:(ci + 1) * nb].reshape(rc, D)
            else:
                per = tt // rc
                msc = msc_ref[ci // per, (ci % per) * rc:(ci % per + 1) * rc, :]
                xr = x_ref[ci // per, (ci % per) * rc:(ci % per + 1) * rc, :]
            merged = msc + sg_scr[rows, :] * ym
            y = xr + _dot(merged.astype(BF16), wout_ref[...])
            if final:
                y = _rmsnorm(y, fnw_ref[...])
            if nb:
                y_ref[ci * nb:(ci + 1) * nb] = y.reshape(nb, tt, D)
            else:
                y_ref[ci // per, (ci % per) * rc:(ci % per + 1) * rc, :] = y

    if nsub == 1:
        epilogue()
    else:
        pl.when(step == nsub - 1)(epilogue)


def _mlstm_call(layer, depth, x, msc, weights, state, c_stack, bt, tt):
    b, t, _ = x.shape
    r = bt * tt
    rp = max(r, 128)
    nbt = b // bt
    has_state = state is not None
    has_alias = c_stack is not None
    final = layer == depth - 1
    nsub = max(1, min(nbt, PROJ_ROWS // r)) if t == tt else 1
    assert nbt % nsub == 0
    ro = r * nsub
    rpo = max(ro, 128)
    if nsub == 1:
        grid = (nbt, t // tt)
        tile_idx = lambda bi, si: (bi, si, 0)
        sub_idx = lambda bi, si: bi
    else:
        grid = (nbt // nsub, nsub)
        tile_idx = lambda bi, si: (bi, 0, 0)
        sub_idx = lambda bi, si: bi * nsub + si
    wspec = lambda shape: pl.BlockSpec((None,) + shape, lambda bi, si: (layer,) + (0,) * len(shape),
                                       pipeline_mode=pl.Buffered(1))
    tile = pl.BlockSpec((bt * nsub, tt, D), tile_idx)
    wcol = lambda k: pl.BlockSpec((None, D, D), lambda bi, si: (layer, 0, k), pipeline_mode=pl.Buffered(1))
    nw, w_main, w_if, bias, w_gates, hnw, wpm, wout, fnw = weights
    in_specs = ([tile, tile, wspec((1, D))] + [wcol(OFF_M // D + k) for k in range(5)]
                + [wspec((16, D)), wspec((16, 1)), wcol(2), wspec((1, D)), wspec((D, D)), wspec((D, D)),
                   pl.BlockSpec((1, D), lambda bi, si: (0, 0), pipeline_mode=pl.Buffered(1))])
    args = [x, msc, nw] + [w_main] * 5 + [w_if, bias, w_gates, hnw, wpm, wout, fnw]
    if has_state:
        in_specs += [pl.BlockSpec((None, bt, NH, DH, DH), lambda bi, si: (layer, sub_idx(bi, si), 0, 0, 0)),
                     pl.BlockSpec((None, bt, NH, DH), lambda bi, si: (layer, sub_idx(bi, si), 0, 0)),
                     pl.BlockSpec((None, None, 8, rpo), lambda bi, si: (layer, bi, 0, 0))]
        c0, n0, m0 = state
        args += [c0, n0, _expand_m(m0, bt * nsub, tt)]
    aliases = {}
    if has_alias:
        aliases = {len(args): 1}
        in_specs.append(pl.BlockSpec(memory_space=pl.ANY))
        args.append(c_stack)
    out_specs = [tile,
                 pl.BlockSpec((None, bt, NH, DH, DH), lambda bi, si: (layer, sub_idx(bi, si), 0, 0, 0)),
                 pl.BlockSpec((bt, NH, DH), lambda bi, si: (sub_idx(bi, si), 0, 0)),
                 pl.BlockSpec((None, 8, rpo), lambda bi, si: (bi, 0, 0))]
    out_shape = [jax.ShapeDtypeStruct((b, t, D), F32), jax.ShapeDtypeStruct((depth, b, NH, DH, DH), F32),
                 jax.ShapeDtypeStruct((b, NH, DH), F32), jax.ShapeDtypeStruct((nbt // nsub, 8, rpo), F32)]
    scratch = [pltpu.VMEM((ro, D), BF16),
               pltpu.VMEM((ro, 5 * D), F32),
               pltpu.VMEM((ro, D), F32),
               pltpu.VMEM((ro, D), F32)]
    if nsub > 1:
        scratch += [pltpu.VMEM((ro, 128), F32),
                    pltpu.VMEM((16 * nsub, 128), F32)]
    y, c_new, n_new, m_rows = pl.pallas_call(
        functools.partial(_mlstm_kernel, has_state, has_alias, final, t == tt, bt, tt, nsub),
        grid=grid,
        in_specs=in_specs, out_specs=out_specs, out_shape=out_shape, scratch_shapes=scratch,
        input_output_aliases=aliases,
        compiler_params=pltpu.CompilerParams(dimension_semantics=("arbitrary", "arbitrary"),
                                             vmem_limit_bytes=VMEM_LIMIT),
        name="mlstm_state" if has_state else "mlstm",
    )(*args)
    return y, c_new, n_new, _collapse_m(m_rows, bt * nsub, tt)


def _expand_m(m, bt, tt):
    depth, b, _ = m.shape
    r = bt * tt
    rp = max(r, 128)
    v = jnp.repeat(m.reshape(depth, b // bt, bt, NH).transpose(0, 1, 3, 2), tt, axis=-1)
    return jnp.pad(v, ((0, 0), (0, 0), (0, 8 - NH), (0, rp - r)))


def _collapse_m(mrows, bt, tt):
    nbt = mrows.shape[0]
    return mrows[:, :NH, 0:bt * tt:tt].transpose(0, 2, 1).reshape(nbt * bt, NH)


def kernel(x_prompt, x_sample, state_ssm_re, state_ssm_im, state_conv, state_mlstm_c, state_mlstm_n, state_mlstm_m, norm_w, w_in, i_bias, f_bias, lam_re, lam_im, log_dt, b_re, b_im, c_re, c_im, d_skip, w_glu, w_proj_s, conv_w, w_proj_c, mlstm_norm_w, w_proj_m, w_out, final_norm_w):
    depth = norm_w.shape[0]
    bp, tp, _ = x_prompt.shape
    bs, ts, _ = x_sample.shape
    assert tp % PROMPT_CHUNK == 0 and ts % 8 == 0 and ts & (ts - 1) == 0 and bp % 8 == 0 and bs % 64 == 0

    a_re, a_im, bb_re, bb_im, cp_re, cp_im = _s5_discretise(lam_re, lam_im, log_dt, b_re, b_im, c_re, c_im)
    row = lambda a: a.reshape(depth, 1, a.shape[-1])
    nw = row(norm_w)
    w_t = jnp.swapaxes(w_in, 1, 2)
    w_main = _cast_columns(w_t, 0, OFF_IF // D)
    w_gates = _cast_columns(w_t, OFF_G, 3)
    gate_rows = lambda a: jnp.pad(a.reshape(depth, 2, NH, -1), ((0, 0), (0, 0), (0, 8 - NH), (0, 0))).reshape(depth, 16, -1)
    w_if = gate_rows(w_t[:, OFF_IF:OFF_G, :]).astype(BF16)
    bias = gate_rows(jnp.concatenate([i_bias, f_bias], axis=-1)[:, :, None])
    wa = [nw, w_main, w_gates, a_re, a_im, bb_re, bb_im, cp_re, cp_im, row(d_skip),
          w_glu.astype(BF16), w_proj_s.astype(BF16), conv_w, w_proj_c.astype(BF16)]
    wb = [nw, w_main, w_if, bias, w_gates, row(mlstm_norm_w), w_proj_m.astype(BF16), w_out.astype(BF16),
          final_norm_w.reshape(1, D)]

    sb_a, sb_b = 64, 4
    pb_a, pb_b = 8, 2
    st_a = (state_ssm_re.reshape(depth, bs, S5N), state_ssm_im.reshape(depth, bs, S5N), state_conv)
    st_b = (state_mlstm_c, state_mlstm_n, state_mlstm_m)

    yp, ys = x_prompt, x_sample
    cp = cs = None
    outs_p = [[] for _ in range(5)]
    outs_s = [[] for _ in range(5)]
    for l in range(depth):
        msc, sre, sim, cv = _s5conv_call(l, yp, wa, None, pb_a, S5_TILE_T)
        yp, cp, nn, mm = _mlstm_call(l, depth, yp, msc, wb, None, cp, pb_b, PROMPT_CHUNK)
        for i, v in enumerate((sre.reshape(bp, S5G, S5P), sim.reshape(bp, S5G, S5P), cv, nn, mm)):
            outs_p[i].append(v)
        msc, sre, sim, cv = _s5conv_call(l, ys, wa, st_a, sb_a, ts)
        ys, cs, nn, mm = _mlstm_call(l, depth, ys, msc, wb, st_b, cs, sb_b, ts)
        for i, v in enumerate((sre.reshape(bs, S5G, S5P), sim.reshape(bs, S5G, S5P), cv, nn, mm)):
            outs_s[i].append(v)
    sp = [jnp.stack(o) for o in outs_p]
    ss = [jnp.stack(o) for o in outs_s]
    return (yp, ys, sp[0], ss[0], sp[1], ss[1], sp[2], ss[2], cp, cs, sp[3], ss[3], sp[4], ss[4])
```

```python
import functools
import math

import jax
import jax.numpy as jnp
from jax import lax
from jax.experimental import pallas as pl
from jax.experimental.pallas import tpu as pltpu

D = 1024
S5W = 512
S5G = 32
S5P = 64
S5C = 16
S5N = S5G * S5P
CW = 512
CK = 3
NH = 4
DH = 256
PROMPT_CHUNK = 256
S5_TILE_T = 64
S5_TILES_PER_STEP = 2
SHORT_SEQ = 32
EPS = 1e-6
NEG = -1e30
K_SCALE = DH ** -0.5

OFF_S = 0
OFF_M = 3072
OFF_IF = 8192
OFF_G = 8200
IN_W = 11272

F32 = jnp.float32
BF16 = jnp.bfloat16
VMEM_LIMIT = 58 * 1024 * 1024
ROW_CHUNK = 256
GATE_PIECES = 4
MLSTM_ROW_CHUNK = 512
PROJ_ROWS = 256


def _sigmoid(x):
    return 1.0 / (1.0 + jnp.exp(-x))


def _silu(x):
    return x * _sigmoid(x)


def _gelu_tanh(x):
    return 0.5 * x * (1.0 + jnp.tanh(math.sqrt(2.0 / math.pi) * (x + 0.044715 * (x * x * x))))


def _log_sigmoid(x):
    return jnp.minimum(x, 0.0) - jnp.log1p(jnp.exp(-jnp.abs(x)))


def _rmsnorm(x, w):
    return x * lax.rsqrt(jnp.mean(x * x, axis=-1, keepdims=True) + EPS) * w


def _dot(a, b):
    return jnp.dot(a, b, preferred_element_type=F32)


def _dot_nt(a, b):
    return lax.dot_general(a, b, (((1,), (1,)), ((), ())), preferred_element_type=F32)


def _dot_tn(a, b):
    return lax.dot_general(a, b, (((0,), (0,)), ((), ())), preferred_element_type=F32)


def _pitch(n):
    p = n + 8
    return p if (p // 8) % 2 == 1 else p + 8


def _s5_disc_kernel(lre_ref, lim_ref, ldt_ref, bre_ref, bim_ref, cre_ref, cim_ref,
                    are_ref, aim_ref, bbre_ref, bbim_ref, cpre_ref, cpim_ref):
    lr = lre_ref[...]
    li = lim_ref[...]
    dt = jnp.exp(ldt_ref[...])
    ea = jnp.exp(lr * dt)
    ar = ea * jnp.cos(li * dt)
    ai = ea * jnp.sin(li * dt)
    are_ref[...] = ar
    aim_ref[...] = ai
    nr = ar - 1.0
    inv = 1.0 / (lr * lr + li * li)
    cr = (nr * lr + ai * li) * inv
    ci = (ai * lr - nr * li) * inv
    half = S5N // 2
    shr = lax.shift_right_logical
    bdiag = (shr(lax.broadcasted_iota(jnp.int32, (256, half), 0), 4)
             == shr(lax.broadcasted_iota(jnp.int32, (256, half), 1), 6))
    cdiag = (shr(lax.broadcasted_iota(jnp.int32, (half, 256), 0), 6)
             == shr(lax.broadcasted_iota(jnp.int32, (half, 256), 1), 4))
    rep = lambda x: jnp.concatenate([x] * 16, axis=0)
    for h in range(2):
        crh = cr[:, h * half:(h + 1) * half]
        cih = ci[:, h * half:(h + 1) * half]
        br = bre_ref[h]
        bi = bim_ref[h]
        bbre_ref[h] = jnp.where(bdiag, rep(crh * br - cih * bi), 0.0).astype(BF16)
        bbim_ref[h] = jnp.where(bdiag, rep(crh * bi + cih * br), 0.0).astype(BF16)
        cpre_ref[h] = jnp.where(cdiag, rep(cre_ref[h]), 0.0).astype(BF16)
        cpim_ref[h] = jnp.where(cdiag, rep(cim_ref[h]), 0.0).astype(BF16)


def _s5_discretise(lam_re, lam_im, log_dt, b_re, b_im, c_re, c_im):
    depth = lam_re.shape[0]

    def rows_b(b):
        return jnp.swapaxes(b.reshape(depth, 2, 16 * S5P, S5C), 2, 3)

    def rows_c(c):
        return jnp.swapaxes(c.reshape(depth, 2, 16 * S5C, S5P), 2, 3)

    row = lambda a: a.reshape(depth, 1, S5N)
    ldt = jnp.broadcast_to(log_dt[:, :, None], (depth, S5G, S5P))
    vec = pl.BlockSpec((None, 1, S5N), lambda l: (l, 0, 0))
    bmat = pl.BlockSpec((None, 2, 256, 1024), lambda l: (l, 0, 0, 0))
    cmat = pl.BlockSpec((None, 2, 1024, 256), lambda l: (l, 0, 0, 0))
    brow = pl.BlockSpec((None, 2, S5C, 1024), lambda l: (l, 0, 0, 0))
    crow = pl.BlockSpec((None, 2, S5P, 256), lambda l: (l, 0, 0, 0))
    return pl.pallas_call(
        _s5_disc_kernel,
        grid=(depth,),
        in_specs=[vec, vec, vec, brow, brow, crow, crow],
        out_specs=[vec, vec, bmat, bmat, cmat, cmat],
        out_shape=[jax.ShapeDtypeStruct((depth, 1, S5N), F32), jax.ShapeDtypeStruct((depth, 1, S5N), F32),
                   jax.ShapeDtypeStruct((depth, 2, 256, 1024), BF16),
                   jax.ShapeDtypeStruct((depth, 2, 256, 1024), BF16),
                   jax.ShapeDtypeStruct((depth, 2, 1024, 256), BF16),
                   jax.ShapeDtypeStruct((depth, 2, 1024, 256), BF16)],
        name="s5_discretise",
    )(row(lam_re), row(lam_im), row(ldt), rows_b(b_re), rows_b(b_im), rows_c(c_re), rows_c(c_im))


def _cast_kernel(shift, *refs):
    if shift:
        wa_ref, wb_ref, o_ref = refs
        w = jnp.concatenate([wa_ref[shift:, :], wb_ref[...]], axis=0)
    else:
        w_ref, o_ref = refs
        w = w_ref[...]
    o_ref[...] = w.T.astype(BF16)


def _cast_columns(w_t, first_col, n_blocks):
    depth = w_t.shape[0]
    k0, shift = divmod(first_col, D)
    assert shift % 8 == 0
    in_specs = [pl.BlockSpec((None, D, D), lambda l, k: (l, k0 + k, 0))]
    args = [w_t]
    if shift:
        assert D % shift == 0
        in_specs.append(pl.BlockSpec((None, shift, D), lambda l, k: (l, (k0 + k + 1) * (D // shift), 0)))
        args.append(w_t)
    return pl.pallas_call(
        functools.partial(_cast_kernel, shift),
        grid=(depth, n_blocks),
        in_specs=in_specs,
        out_specs=pl.BlockSpec((None, D, D), lambda l, k: (l, 0, k)),
        out_shape=jax.ShapeDtypeStruct((depth, D, n_blocks * D), BF16),
        name="cast_weights",
    )(*args)


def _s5conv_kernel(has_state, bt, tt, nrep, *refs):
    for rep in range(nrep):
        _s5conv_tile(has_state, bt, tt, rep, *refs)


def _s5conv_tile(has_state, bt, tt, rep, *refs):
    refs = list(refs)
    (x_ref, nw_ref, ws_ref, wg_ref, are_ref, aim_ref, bbre_ref, bbim_ref, cre_ref, cim_ref,
     dsk_ref, wglu_ref, wps_ref, cw_ref, wpc_ref) = refs[:15]
    refs = refs[15:]
    if has_state:
        s0re_ref, s0im_ref, cv0_ref = refs[:3]
        refs = refs[3:]
    msc_ref, sre_ref, sim_ref, cvo_ref = refs[:4]
    h_scr, ps_scr, g_scr, uslab, utb, bu_scr, yslab, y_scr, vhalo = refs[4:]

    r = bt * tt
    pt = _pitch(tt)
    pb = _pitch(bt)
    rc = min(ROW_CHUNK, r)
    nchunk = r // rc
    ti = pl.program_id(1)

    def merge_gates(ci):
        rows = slice(ci * rc, (ci + 1) * rc)
        g_scr[rows, :] = _sigmoid(_dot(h_scr[rows, :], wg_ref[...]))

    def _init():
        if has_state:
            sre_ref[...] = s0re_ref[...]
            sim_ref[...] = s0im_ref[...]
            vhalo[:, 6:8, :] = cv0_ref[...]
        else:
            sre_ref[...] = jnp.zeros_like(sre_ref)
            sim_ref[...] = jnp.zeros_like(sim_ref)
            vhalo[:, 6:8, :] = jnp.zeros((bt, 2, CW), F32)

    if rep == 0:
        pl.when(ti == 0)(_init)

    tsl = slice(rep * tt, (rep + 1) * tt)
    x = x_ref[:, tsl, :].reshape(r, D)
    h = _rmsnorm(x, nw_ref[...]).astype(BF16)
    h_scr[...] = h
    ps_scr[...] = _dot(h, ws_ref[...])


    for b in range(bt):
        for j in range(4):
            uslab[j, b * pt:b * pt + tt, :] = ps_scr[b * tt:(b + 1) * tt, j * 128:(j + 1) * 128]
    for t in range(tt):
        for bs in range(bt // 8):
            dst = t * bt + bs * 8
            for j in range(4):
                utb[dst:dst + 8, j * 128:(j + 1) * 128] = uslab[j, pl.ds(bs * 8 * pt + t, 8, stride=pt), :]
    for ci in range(nchunk // 2):
        merge_gates(ci)

    for hf in range(2):
        uh = utb[:, hf * 256:(hf + 1) * 256].astype(BF16)
        bu_scr[:, hf * 1024:(hf + 1) * 1024] = _dot(uh, bbre_ref[hf])
        bu_scr[:, S5N + hf * 1024:S5N + (hf + 1) * 1024] = _dot(uh, bbim_ref[hf])
    for ci in range(nchunk // 2, nchunk):
        merge_gates(ci)

    v = ps_scr[:, 1536:2048] * ps_scr[:, 2048:2560]
    vhalo[:, 8:8 + tt, :] = v.reshape(bt, tt, CW)
    cw = cw_ref[...]
    yc = (cw[0:1, :] * vhalo[:, 6:6 + tt, :] + cw[1:2, :] * vhalo[:, 7:7 + tt, :]
          + cw[2:3, :] * vhalo[:, 8:8 + tt, :])
    ps_scr[:, 2048:2560] = yc.reshape(r, CW)
    new_halo = vhalo[:, tt + 6:tt + 8, :]
    vhalo[:, 6:8, :] = new_halo
    cvo_ref[...] = new_halo

    for ci in range(nchunk):
        rows = slice(ci * rc, (ci + 1) * rc)
        tc_ = ps_scr[rows, 1024:1536] * ps_scr[rows, 2048:2560] * _silu(ps_scr[rows, 2560:3072])
        ps_scr[rows, 1024:2048] = g_scr[rows, D:] * _dot(tc_.astype(BF16), wpc_ref[...])

    for q in range(4):
        c_re = slice(q * 512, (q + 1) * 512)
        c_im = slice(S5N + q * 512, S5N + (q + 1) * 512)
        ar = jnp.broadcast_to(are_ref[:, c_re], (8, 512))
        ai = jnp.broadcast_to(aim_ref[:, c_re], (8, 512))
        for bs in range(bt // 8):
            r0 = bs * 8
            sr = sre_ref[r0:r0 + 8, c_re]
            si = sim_ref[r0:r0 + 8, c_re]
            for t in range(tt):
                row = t * bt + r0
                nr = ar * sr - ai * si + bu_scr[row:row + 8, c_re]
                ni = ar * si + ai * sr + bu_scr[row:row + 8, c_im]
                bu_scr[row:row + 8, c_re] = nr
                bu_scr[row:row + 8, c_im] = ni
                sr, si = nr, ni
            sre_ref[r0:r0 + 8, c_re] = sr
            sim_ref[r0:r0 + 8, c_re] = si

    for m in range(2):
        sre = bu_scr[:, m * 1024:(m + 1) * 1024].astype(BF16)
        sim = bu_scr[:, S5N + m * 1024:S5N + (m + 1) * 1024].astype(BF16)
        ytb = _dot(sre, cre_ref[m]) - _dot(sim, cim_ref[m])
        for t in range(tt):
            for jj in range(2):
                yslab[2 * m + jj, t * pb:t * pb + bt, :] = ytb[t * bt:(t + 1) * bt, jj * 128:(jj + 1) * 128]
    for b in range(bt):
        for ts in range(tt // 8):
            dst = b * tt + ts * 8
            for j in range(4):
                y_scr[dst:dst + 8, j * 128:(j + 1) * 128] = yslab[j, pl.ds(ts * 8 * pb + b, 8, stride=pb), :]

    nb = rc // tt if rc >= tt else 0
    dsk = dsk_ref[...]
    for ci in range(nchunk):
        rows = slice(ci * rc, (ci + 1) * rc)
        y = y_scr[rows, :] + dsk * ps_scr[rows, 0:512]
        gl = _gelu_tanh(y)
        glu = gl * _sigmoid(_dot(gl.astype(BF16), wglu_ref[...]))
        ts_ = glu * _silu(ps_scr[rows, 512:1024])
        ys = _dot(ts_.astype(BF16), wps_ref[...])
        out = g_scr[rows, :D] * ys + ps_scr[rows, 1024:2048]
        if nb:
            msc_ref[ci * nb:(ci + 1) * nb, tsl, :] = out.reshape(nb, tt, D)
        else:
            per = tt // rc
            msc_ref[ci // per, rep * tt + (ci % per) * rc:rep * tt + (ci % per + 1) * rc, :] = out


def _s5conv_call(layer, x, weights, state, bt, tt, nrep=1):
    b, t, _ = x.shape
    r = bt * tt
    has_state = state is not None
    wspec = lambda shape: pl.BlockSpec((None,) + shape, lambda bi, ti: (layer,) + (0,) * len(shape),
                                       pipeline_mode=pl.Buffered(1))
    tile = pl.BlockSpec((bt, tt * nrep, D), lambda bi, ti: (bi, ti, 0))
    in_specs = [tile, wspec((1, D)), wspec((D, 3072)), wspec((D, 2048)), wspec((1, S5N)), wspec((1, S5N)),
                wspec((2, 256, 1024)), wspec((2, 256, 1024)), wspec((2, 1024, 256)), wspec((2, 1024, 256)),
                wspec((1, S5W)), wspec((S5W, S5W)), wspec((S5W, D)), wspec((CK, CW)), wspec((CW, D))]
    args = [x] + list(weights)
    if has_state:
        in_specs += [pl.BlockSpec((None, bt, S5N), lambda bi, ti: (layer, bi, 0)),
                     pl.BlockSpec((None, bt, S5N), lambda bi, ti: (layer, bi, 0)),
                     pl.BlockSpec((None, bt, CK - 1, CW), lambda bi, ti: (layer, bi, 0, 0))]
        args += list(state)
    out_specs = [tile,
                 pl.BlockSpec((bt, S5N), lambda bi, ti: (bi, 0)),
                 pl.BlockSpec((bt, S5N), lambda bi, ti: (bi, 0)),
                 pl.BlockSpec((bt, CK - 1, CW), lambda bi, ti: (bi, 0, 0))]
    out_shape = [jax.ShapeDtypeStruct((b, t, D), F32), jax.ShapeDtypeStruct((b, S5N), F32),
                 jax.ShapeDtypeStruct((b, S5N), F32), jax.ShapeDtypeStruct((b, CK - 1, CW), F32)]
    scratch = [pltpu.VMEM((r, D), BF16),
               pltpu.VMEM((r, 3072), F32),
               pltpu.VMEM((r, 2 * D), F32),
               pltpu.VMEM((4, bt * _pitch(tt), 128), F32),
               pltpu.VMEM((r, S5W), F32),
               pltpu.VMEM((r, 2 * S5N), F32),
               pltpu.VMEM((4, tt * _pitch(bt), 128), F32),
               pltpu.VMEM((r, S5W), F32),
               pltpu.VMEM((bt, tt + 8, CW), F32)]
    return pl.pallas_call(
        functools.partial(_s5conv_kernel, has_state, bt, tt, nrep),
        grid=(b // bt, t // (tt * nrep)),
        in_specs=in_specs, out_specs=out_specs, out_shape=out_shape, scratch_shapes=scratch,
        compiler_params=pltpu.CompilerParams(dimension_semantics=("arbitrary", "arbitrary"),
                                             vmem_limit_bytes=VMEM_LIMIT),
        name="s5conv_state" if has_state else "s5conv",
    )(*args)


def _seg_scan(x, pos, seg, op, fill):
    s = 1
    while s < seg:
        x = op(x, jnp.where(pos >= s, pltpu.roll(x, s, 1), fill))
        s *= 2
    return x


def _seg_bcast_last(x, pos, seg):
    n = x.shape[1]
    s = 1
    while s < seg:
        x = jnp.where(pos + s <= seg - 1, pltpu.roll(x, n - s, 1), x)
        s *= 2
    return x


def _mlstm_kernel(has_state, has_alias, final, single_tile, bt, tt, nsub, *refs):
    refs = list(refs)
    x_ref, msc_ref, nw_ref = refs[:3]
    wm_refs = refs[3:8]
    wif_ref, bias_ref, wgm_ref, hnw_ref, wpm_ref, wout_ref, fnw_ref = refs[8:15]
    refs = refs[15:]
    if has_state:
        c0_ref, n0_ref, m0_ref = refs[:3]
        refs = refs[3:]
    if has_alias:
        refs = refs[1:]
    y_ref, c_ref, n_ref, m_ref = refs[:4]
    h_scr, p_scr, hm_scr, sg_scr = refs[4:8]
    if nsub > 1:
        zt_scr, gd_scr = refs[8:]

    r = bt * tt
    ro = r * nsub
    rp = max(r, 128)
    step = pl.program_id(1)

    def init_state():
        if has_state:
            c_ref[...] = c0_ref[...]
            n_ref[...] = n0_ref[...]
            m_ref[...] = m0_ref[...]
        else:
            c_ref[...] = jnp.zeros_like(c_ref)
            n_ref[...] = jnp.zeros_like(n_ref)
            m_ref[...] = jnp.zeros_like(m_ref)

    def normalise():
        hh = _rmsnorm(x_ref[...].reshape(ro, D), nw_ref[...]).astype(BF16)
        h_scr[...] = hh
        return hh

    def project(hh):
        for k, w_ref in enumerate(wm_refs):
            p_scr[:, k * D:(k + 1) * D] = _dot(hh, w_ref[...])

    def merge_gate(piece):
        cols = slice(piece * (D // GATE_PIECES), (piece + 1) * (D // GATE_PIECES))
        sg_scr[:, cols] = _sigmoid(_dot(h_scr[...], wgm_ref[:, cols]))

    def gate_chain(hrows, mprev):
        n = hrows.shape[0]
        gt = _dot_nt(wif_ref[...], hrows) + bias_ref[...]
        pos = jnp.bitwise_and(lax.broadcasted_iota(jnp.int32, (8, n), 1), tt - 1)
        bcum = _seg_scan(_log_sigmoid(gt[8:16, :]), pos, tt, jnp.add, 0.0)
        g = gt[0:8, :] - bcum
        mrun = jnp.maximum(mprev, _seg_scan(g, pos, tt, jnp.maximum, NEG))
        mlast = _seg_bcast_last(mrun, pos, tt)
        winter = jnp.exp(mprev - mrun)
        efloor = jnp.exp(-(bcum + mrun))
        wrow = jnp.exp(g - mlast)
        decay = jnp.exp(mprev - mlast)
        mnew = _seg_bcast_last(bcum, pos, tt) + mlast
        zt = jnp.concatenate([mrun, winter, efloor, wrow], axis=0).T
        return g, decay, zt, mnew

    if not single_tile:
        pl.when(step == 0)(init_state)
        c_prev, n_prev, m_start = c_ref, n_ref, (lambda: m_ref[...])
    elif has_state:
        c_prev, n_prev, m_start = c0_ref, n0_ref, (lambda: m0_ref[...])
    else:
        c_ref[...] = jnp.zeros_like(c_ref)
        n_ref[...] = jnp.zeros_like(n_ref)
        c_prev, n_prev, m_start = c_ref, n_ref, (lambda: jnp.zeros(m_ref.shape, F32))

    if nsub == 1:
        base = 0
        h = normalise()
        hp = h if rp == r else jnp.concatenate([h, jnp.zeros((rp - r, D), BF16)], axis=0)
        g, decay, zt, mnew = gate_chain(hp, m_start())
        m_ref[...] = mnew
        project(h)
    else:
        base = pl.multiple_of(step * r, r)

        @pl.when(step == 0)
        def _block_prologue():
            hh = normalise()
            ga, da, za, mnew_all = gate_chain(hh, m_start())
            project(hh)
            m_ref[...] = mnew_all
            zt_scr[:, 0:32] = za
            for s in range(nsub):
                gd_scr[16 * s:16 * s + 8, 0:r] = ga[:, s * r:(s + 1) * r]
                gd_scr[16 * s + 8:16 * s + 16, 0:r] = da[:, s * r:(s + 1) * r]
            for piece in range(GATE_PIECES):
                merge_gate(piece)

        zt = zt_scr[pl.ds(base, r), 0:32]
        gd = gd_scr[pl.ds(pl.multiple_of(step * 16, 16), 16), :]
        g = gd[0:8, 0:r]
        decay = gd[8:16, 0:r]

    if tt <= SHORT_SEQ:
        nst = NH * r
        prow = slice(0, r) if nsub == 1 else pl.ds(base, r)
        stack = lambda off: jnp.concatenate(
            [p_scr[prow, off + hd * DH:off + (hd + 1) * DH] for hd in range(NH)], axis=0)
        qs = stack(0)
        ks = stack(D) * K_SCALE
        vs = stack(2 * D)
        col = lambda c0: jnp.concatenate([zt[0:r, c0 + hd:c0 + hd + 1] for hd in range(NH)], axis=0)
        mcol, wi, ef, wc = col(0), col(8), col(16), col(24)
        grow = jnp.concatenate([g[hd:hd + 1, 0:r] for hd in range(NH)], axis=1)
        row_i = lax.broadcasted_iota(jnp.int32, (nst, nst), 0)
        col_i = lax.broadcasted_iota(jnp.int32, (nst, nst), 1)
        shift = tt.bit_length() - 1
        same = lax.shift_right_logical(row_i, shift) == lax.shift_right_logical(col_i, shift)
        arg = jnp.where(col_i <= row_i, grow - mcol, NEG)
        dm = jnp.exp(jnp.where(same, arg, NEG))
        sc = _dot_nt(qs.astype(BF16), ks.astype(BF16)) * dm
        intra = _dot(sc.astype(BF16), vs.astype(BF16))
        pairs = [(hd, b) for hd in range(NH) for b in range(bt)]
        blk = lambda hd, b: slice(hd * r + b * tt, hd * r + (b + 1) * tt)
        inter = jnp.concatenate([_dot(qs[blk(hd, b)].astype(BF16), c_prev[b, hd].astype(BF16))
                                 for hd, b in pairs], axis=0)
        nfull = jnp.concatenate([jnp.broadcast_to(n_prev[b, hd:hd + 1, :], (tt, DH)) for hd, b in pairs], axis=0)
        num = wi * inter + intra
        den = wi * jnp.sum(qs * nfull, axis=-1, keepdims=True) + jnp.sum(sc, axis=-1, keepdims=True)
        hh = num * (1.0 / jnp.maximum(jnp.abs(den), ef))
        for hd in range(NH):
            hm_scr[prow, hd * DH:(hd + 1) * DH] = hh[hd * r:(hd + 1) * r]
        wk = ks * wc
        for hd, b in pairs:
            dec = decay[hd:hd + 1, b * tt:b * tt + 1]
            upd = _dot_tn(wk[blk(hd, b)].astype(BF16), vs[blk(hd, b)].astype(BF16))
            c_ref[b, hd] = dec * c_prev[b, hd] + upd
            n_ref[b, hd:hd + 1, :] = (dec * n_prev[b, hd:hd + 1, :]
                                      + jnp.sum(wk[blk(hd, b)], axis=0, keepdims=True))
        if nsub == 1:
            for piece in range(GATE_PIECES):
                merge_gate(piece)

    row_i = lax.broadcasted_iota(jnp.int32, (tt, tt), 0)
    col_i = lax.broadcasted_iota(jnp.int32, (tt, tt), 1)
    causal = row_i >= col_i

    for b in range(bt if tt > SHORT_SEQ else 0):
        rows = slice(b * tt, (b + 1) * tt)
        prow = rows if nsub == 1 else pl.ds(base + b * tt, tt)
        for hd in range(NH):
            cs = slice(hd * DH, (hd + 1) * DH)
            qf = p_scr[prow, cs]
            kf = p_scr[prow, D + hd * DH:D + (hd + 1) * DH] * K_SCALE
            vb = p_scr[prow, 2 * D + hd * DH:2 * D + (hd + 1) * DH].astype(BF16)
            qb = qf.astype(BF16)
            kb = kf.astype(BF16)
            mcol = zt[rows, hd:hd + 1]
            wi = zt[rows, 8 + hd:9 + hd]
            ef = zt[rows, 16 + hd:17 + hd]
            wc = zt[rows, 24 + hd:25 + hd]
            grow = g[hd:hd + 1, b * tt:(b + 1) * tt]
            dm = jnp.exp(jnp.where(causal, grow - mcol, NEG))
            s_raw = _dot_nt(qb, kb)
            cf = c_prev[b, hd]
            nrow = n_prev[b, hd:hd + 1, :]
            inter = _dot(qb, cf.astype(BF16))
            dec = decay[hd:hd + 1, b * tt:b * tt + 1]
            wk = kf * wc
            c_ref[b, hd] = dec * cf + _dot_tn(wk.astype(BF16), vb)
            n_ref[b, hd:hd + 1, :] = dec * nrow + jnp.sum(wk, axis=0, keepdims=True)
            sc = s_raw * dm
            num = wi * inter + _dot(sc.astype(BF16), vb)
            den = wi * jnp.sum(qf * nrow, axis=-1, keepdims=True) + jnp.sum(sc, axis=-1, keepdims=True)
            hm_scr[prow, cs] = num * (1.0 / jnp.maximum(jnp.abs(den), ef))
            done = b * NH + hd + 1
            if nsub == 1 and (done * GATE_PIECES) % (bt * NH) == 0:
                merge_gate(done * GATE_PIECES // (bt * NH) - 1)

    def epilogue():
        rc = min(MLSTM_ROW_CHUNK, ro)
        hnw = hnw_ref[...]
        for ci in range(ro // rc):
            rows = slice(ci * rc, (ci + 1) * rc)
            hmv = hm_scr[rows, :] * _sigmoid(p_scr[rows, 3 * D:4 * D])
            parts = []
            for hd in range(NH):
                hh = hmv[:, hd * DH:(hd + 1) * DH]
                mu = jnp.mean(hh, axis=-1, keepdims=True)
                dv = hh - mu
                var = jnp.mean(dv * dv, axis=-1, keepdims=True)
                parts.append(dv * lax.rsqrt(var + EPS) * hnw[:, hd * DH:(hd + 1) * DH])
            hn = jnp.concatenate(parts, axis=1)
            tm = hn * _silu(p_scr[rows, 4 * D:5 * D])
            ym = _dot(tm.astype(BF16), wpm_ref[...])
            nb = rc // tt if rc >= tt else 0
            if nb:
                msc = msc_ref[ci * nb:(ci + 1) * nb].reshape(rc, D)
                xr = x_ref[ci * nb:(ci + 1) * nb].reshape(rc, D)
            else:
                per = tt // rc
                msc = msc_ref[ci // per, (ci % per) * rc:(ci % per + 1) * rc, :]
                xr = x_ref[ci // per, (ci % per) * rc:(ci % per + 1) * rc, :]
            merged = msc + sg_scr[rows, :] * ym
            y = xr + _dot(merged.astype(BF16), wout_ref[...])
            if final:
                y = _rmsnorm(y, fnw_ref[...])
            if nb:
                y_ref[ci * nb:(ci + 1) * nb] = y.reshape(nb, tt, D)
            else:
                y_ref[ci // per, (ci % per) * rc:(ci % per + 1) * rc, :] = y

    if nsub == 1:
        epilogue()
    else:
        pl.when(step == nsub - 1)(epilogue)


def _mlstm_call(layer, depth, x, msc, weights, state, c_stack, bt, tt):
    b, t, _ = x.shape
    r = bt * tt
    rp = max(r, 128)
    nbt = b // bt
    has_state = state is not None
    has_alias = c_stack is not None
    final = layer == depth - 1
    nsub = max(1, min(nbt, PROJ_ROWS // r)) if t == tt else 1
    assert nbt % nsub == 0
    ro = r * nsub
    rpo = max(ro, 128)
    if nsub == 1:
        grid = (nbt, t // tt)
        tile_idx = lambda bi, si: (bi, si, 0)
        sub_idx = lambda bi, si: bi
    else:
        grid = (nbt // nsub, nsub)
        tile_idx = lambda bi, si: (bi, 0, 0)
        sub_idx = lambda bi, si: bi * nsub + si
    wspec = lambda shape: pl.BlockSpec((None,) + shape, lambda bi, si: (layer,) + (0,) * len(shape),
                                       pipeline_mode=pl.Buffered(1))
    tile = pl.BlockSpec((bt * nsub, tt, D), tile_idx)
    wcol = lambda k: pl.BlockSpec((None, D, D), lambda bi, si: (layer, 0, k), pipeline_mode=pl.Buffered(1))
    nw, w_main, w_if, bias, w_gates, hnw, wpm, wout, fnw = weights
    in_specs = ([tile, tile, wspec((1, D))] + [wcol(OFF_M // D + k) for k in range(5)]
                + [wspec((16, D)), wspec((16, 1)), wcol(2), wspec((1, D)), wspec((D, D)), wspec((D, D)),
                   pl.BlockSpec((1, D), lambda bi, si: (0, 0), pipeline_mode=pl.Buffered(1))])
    args = [x, msc, nw] + [w_main] * 5 + [w_if, bias, w_gates, hnw, wpm, wout, fnw]
    if has_state:
        in_specs += [pl.BlockSpec((None, bt, NH, DH, DH), lambda bi, si: (layer, sub_idx(bi, si), 0, 0, 0)),
                     pl.BlockSpec((None, bt, NH, DH), lambda bi, si: (layer, sub_idx(bi, si), 0, 0)),
                     pl.BlockSpec((None, None, 8, rpo), lambda bi, si: (layer, bi, 0, 0))]
        c0, n0, m0 = state
        args += [c0, n0, _expand_m(m0, bt * nsub, tt)]
    aliases = {}
    if has_alias:
        aliases = {len(args): 1}
        in_specs.append(pl.BlockSpec(memory_space=pl.ANY))
        args.append(c_stack)
    out_specs = [tile,
                 pl.BlockSpec((None, bt, NH, DH, DH), lambda bi, si: (layer, sub_idx(bi, si), 0, 0, 0)),
                 pl.BlockSpec((bt, NH, DH), lambda bi, si: (sub_idx(bi, si), 0, 0)),
                 pl.BlockSpec((None, 8, rpo), lambda bi, si: (bi, 0, 0))]
    out_shape = [jax.ShapeDtypeStruct((b, t, D), F32), jax.ShapeDtypeStruct((depth, b, NH, DH, DH), F32),
                 jax.ShapeDtypeStruct((b, NH, DH), F32), jax.ShapeDtypeStruct((nbt // nsub, 8, rpo), F32)]
    scratch = [pltpu.VMEM((ro, D), BF16),
               pltpu.VMEM((ro, 5 * D), F32),
               pltpu.VMEM((ro, D), F32),
               pltpu.VMEM((ro, D), F32)]
    if nsub > 1:
        scratch += [pltpu.VMEM((ro, 128), F32),
                    pltpu.VMEM((16 * nsub, 128), F32)]
    y, c_new, n_new, m_rows = pl.pallas_call(
        functools.partial(_mlstm_kernel, has_state, has_alias, final, t == tt, bt, tt, nsub),
        grid=grid,
        in_specs=in_specs, out_specs=out_specs, out_shape=out_shape, scratch_shapes=scratch,
        input_output_aliases=aliases,
        compiler_params=pltpu.CompilerParams(dimension_semantics=("arbitrary", "arbitrary"),
                                             vmem_limit_bytes=VMEM_LIMIT),
        name="mlstm_state" if has_state else "mlstm",
    )(*args)
    return y, c_new, n_new, _collapse_m(m_rows, bt * nsub, tt)


def _expand_m(m, bt, tt):
    depth, b, _ = m.shape
    r = bt * tt
    rp = max(r, 128)
    v = jnp.repeat(m.reshape(depth, b // bt, bt, NH).transpose(0, 1, 3, 2), tt, axis=-1)
    return jnp.pad(v, ((0, 0), (0, 0), (0, 8 - NH), (0, rp - r)))


def _collapse_m(mrows, bt, tt):
    nbt = mrows.shape[0]
    return mrows[:, :NH, 0:bt * tt:tt].transpose(0, 2, 1).reshape(nbt * bt, NH)


def kernel(x_prompt, x_sample, state_ssm_re, state_ssm_im, state_conv, state_mlstm_c, state_mlstm_n, state_mlstm_m, norm_w, w_in, i_bias, f_bias, lam_re, lam_im, log_dt, b_re, b_im, c_re, c_im, d_skip, w_glu, w_proj_s, conv_w, w_proj_c, mlstm_norm_w, w_proj_m, w_out, final_norm_w):
    depth = norm_w.shape[0]
    bp, tp, _ = x_prompt.shape
    bs, ts, _ = x_sample.shape
    assert tp % PROMPT_CHUNK == 0 and ts % 8 == 0 and ts & (ts - 1) == 0 and bp % 8 == 0 and bs % 64 == 0

    a_re, a_im, bb_re, bb_im, cp_re, cp_im = _s5_discretise(lam_re, lam_im, log_dt, b_re, b_im, c_re, c_im)
    row = lambda a: a.reshape(depth, 1, a.shape[-1])
    nw = row(norm_w)
    w_t = jnp.swapaxes(w_in, 1, 2)
    w_main = _cast_columns(w_t, 0, OFF_IF // D)
    w_gates = _cast_columns(w_t, OFF_G, 3)
    gate_rows = lambda a: jnp.pad(a.reshape(depth, 2, NH, -1), ((0, 0), (0, 0), (0, 8 - NH), (0, 0))).reshape(depth, 16, -1)
    w_if = gate_rows(w_t[:, OFF_IF:OFF_G, :]).astype(BF16)
    bias = gate_rows(jnp.concatenate([i_bias, f_bias], axis=-1)[:, :, None])
    wa = [nw, w_main, w_gates, a_re, a_im, bb_re, bb_im, cp_re, cp_im, row(d_skip),
          w_glu.astype(BF16), w_proj_s.astype(BF16), conv_w, w_proj_c.astype(BF16)]
    wb = [nw, w_main, w_if, bias, w_gates, row(mlstm_norm_w), w_proj_m.astype(BF16), w_out.astype(BF16),
          final_norm_w.reshape(1, D)]

    sb_a, sb_b = 64, 4
    pb_a, pb_b = 8, 2
    st_a = (state_ssm_re.reshape(depth, bs, S5N), state_ssm_im.reshape(depth, bs, S5N), state_conv)
    st_b = (state_mlstm_c, state_mlstm_n, state_mlstm_m)

    yp, ys = x_prompt, x_sample
    cp = cs = None
    outs_p = [[] for _ in range(5)]
    outs_s = [[] for _ in range(5)]
    for l in range(depth):
        msc, sre, sim, cv = _s5conv_call(l, yp, wa, None, pb_a, S5_TILE_T, S5_TILES_PER_STEP)
        yp, cp, nn, mm = _mlstm_call(l, depth, yp, msc, wb, None, cp, pb_b, PROMPT_CHUNK)
        for i, v in enumerate((sre.reshape(bp, S5G, S5P), sim.reshape(bp, S5G, S5P), cv, nn, mm)):
            outs_p[i].append(v)
        msc, sre, sim, cv = _s5conv_call(l, ys, wa, st_a, sb_a, ts)
        ys, cs, nn, mm = _mlstm_call(l, depth, ys, msc, wb, st_b, cs, sb_b, ts)
        for i, v in enumerate((sre.reshape(bs, S5G, S5P), sim.reshape(bs, S5G, S5P), cv, nn, mm)):
            outs_s[i].append(v)
    sp = [jnp.stack(o) for o in outs_p]
    ss = [jnp.stack(o) for o in outs_s]
    return (yp, ys, sp[0], ss[0], sp[1], ss[1], sp[2], ss[2], cp, cs, sp[3], ss[3], sp[4], ss[4])
```

```python
import functools
import math

import jax
import jax.numpy as jnp
from jax import lax
from jax.experimental import pallas as pl
from jax.experimental.pallas import tpu as pltpu

D = 1024
S5W = 512
S5G = 32
S5P = 64
S5C = 16
S5N = S5G * S5P
CW = 512
CK = 3
NH = 4
DH = 256
PROMPT_CHUNK = 256
S5_TILE_T = 64
SHORT_SEQ = 32
EPS = 1e-6
NEG = -1e30
K_SCALE = DH ** -0.5

OFF_S = 0
OFF_M = 3072
OFF_IF = 8192
OFF_G = 8200
IN_W = 11272

F32 = jnp.float32
BF16 = jnp.bfloat16
VMEM_LIMIT = 58 * 1024 * 1024
ROW_CHUNK = 256
GATE_PIECES = 4
MLSTM_ROW_CHUNK = 512
PROJ_ROWS = 256


def _sigmoid(x):
    return 1.0 / (1.0 + jnp.exp(-x))


def _silu(x):
    return x * _sigmoid(x)


def _gelu_tanh(x):
    return 0.5 * x * (1.0 + jnp.tanh(math.sqrt(2.0 / math.pi) * (x + 0.044715 * (x * x * x))))


def _log_sigmoid(x):
    return jnp.minimum(x, 0.0) - jnp.log1p(jnp.exp(-jnp.abs(x)))


def _rmsnorm(x, w):
    return x * lax.rsqrt(jnp.mean(x * x, axis=-1, keepdims=True) + EPS) * w


def _dot(a, b):
    return jnp.dot(a, b, preferred_element_type=F32)


def _dot_nt(a, b):
    return lax.dot_general(a, b, (((1,), (1,)), ((), ())), preferred_element_type=F32)


def _dot_tn(a, b):
    return lax.dot_general(a, b, (((0,), (0,)), ((), ())), preferred_element_type=F32)


def _pitch(n):
    p = n + 8
    return p if (p // 8) % 2 == 1 else p + 8


def _s5_disc_kernel(lre_ref, lim_ref, ldt_ref, bre_ref, bim_ref, cre_ref, cim_ref,
                    are_ref, aim_ref, bbre_ref, bbim_ref, cpre_ref, cpim_ref):
    lr = lre_ref[...]
    li = lim_ref[...]
    dt = jnp.exp(ldt_ref[...])
    ea = jnp.exp(lr * dt)
    ar = ea * jnp.cos(li * dt)
    ai = ea * jnp.sin(li * dt)
    are_ref[...] = ar
    aim_ref[...] = ai
    nr = ar - 1.0
    inv = 1.0 / (lr * lr + li * li)
    cr = (nr * lr + ai * li) * inv
    ci = (ai * lr - nr * li) * inv
    half = S5N // 2
    shr = lax.shift_right_logical
    bdiag = (shr(lax.broadcasted_iota(jnp.int32, (256, half), 0), 4)
             == shr(lax.broadcasted_iota(jnp.int32, (256, half), 1), 6))
    cdiag = (shr(lax.broadcasted_iota(jnp.int32, (half, 256), 0), 6)
             == shr(lax.broadcasted_iota(jnp.int32, (half, 256), 1), 4))
    rep = lambda x: jnp.concatenate([x] * 16, axis=0)
    for h in range(2):
        crh = cr[:, h * half:(h + 1) * half]
        cih = ci[:, h * half:(h + 1) * half]
        br = bre_ref[h]
        bi = bim_ref[h]
        bbre_ref[h] = jnp.where(bdiag, rep(crh * br - cih * bi), 0.0).astype(BF16)
        bbim_ref[h] = jnp.where(bdiag, rep(crh * bi + cih * br), 0.0).astype(BF16)
        cpre_ref[h] = jnp.where(cdiag, rep(cre_ref[h]), 0.0).astype(BF16)
        cpim_ref[h] = jnp.where(cdiag, rep(cim_ref[h]), 0.0).astype(BF16)


def _s5_discretise(lam_re, lam_im, log_dt, b_re, b_im, c_re, c_im):
    depth = lam_re.shape[0]

    def rows_b(b):
        return jnp.swapaxes(b.reshape(depth, 2, 16 * S5P, S5C), 2, 3)

    def rows_c(c):
        return jnp.swapaxes(c.reshape(depth, 2, 16 * S5C, S5P), 2, 3)

    row = lambda a: a.reshape(depth, 1, S5N)
    ldt = jnp.broadcast_to(log_dt[:, :, None], (depth, S5G, S5P))
    vec = pl.BlockSpec((None, 1, S5N), lambda l: (l, 0, 0))
    bmat = pl.BlockSpec((None, 2, 256, 1024), lambda l: (l, 0, 0, 0))
    cmat = pl.BlockSpec((None, 2, 1024, 256), lambda l: (l, 0, 0, 0))
    brow = pl.BlockSpec((None, 2, S5C, 1024), lambda l: (l, 0, 0, 0))
    crow = pl.BlockSpec((None, 2, S5P, 256), lambda l: (l, 0, 0, 0))
    return pl.pallas_call(
        _s5_disc_kernel,
        grid=(depth,),
        in_specs=[vec, vec, vec, brow, brow, crow, crow],
        out_specs=[vec, vec, bmat, bmat, cmat, cmat],
        out_shape=[jax.ShapeDtypeStruct((depth, 1, S5N), F32), jax.ShapeDtypeStruct((depth, 1, S5N), F32),
                   jax.ShapeDtypeStruct((depth, 2, 256, 1024), BF16),
                   jax.ShapeDtypeStruct((depth, 2, 256, 1024), BF16),
                   jax.ShapeDtypeStruct((depth, 2, 1024, 256), BF16),
                   jax.ShapeDtypeStruct((depth, 2, 1024, 256), BF16)],
        name="s5_discretise",
    )(row(lam_re), row(lam_im), row(ldt), rows_b(b_re), rows_b(b_im), rows_c(c_re), rows_c(c_im))


def _cast_kernel(shift, *refs):
    if shift:
        wa_ref, wb_ref, o_ref = refs
        w = jnp.concatenate([wa_ref[shift:, :], wb_ref[...]], axis=0)
    else:
        w_ref, o_ref = refs
        w = w_ref[...]
    o_ref[...] = w.T.astype(BF16)


def _cast_columns(w_t, first_col, n_blocks):
    depth = w_t.shape[0]
    k0, shift = divmod(first_col, D)
    assert shift % 8 == 0
    in_specs = [pl.BlockSpec((None, D, D), lambda l, k: (l, k0 + k, 0))]
    args = [w_t]
    if shift:
        assert D % shift == 0
        in_specs.append(pl.BlockSpec((None, shift, D), lambda l, k: (l, (k0 + k + 1) * (D // shift), 0)))
        args.append(w_t)
    return pl.pallas_call(
        functools.partial(_cast_kernel, shift),
        grid=(depth, n_blocks),
        in_specs=in_specs,
        out_specs=pl.BlockSpec((None, D, D), lambda l, k: (l, 0, k)),
        out_shape=jax.ShapeDtypeStruct((depth, D, n_blocks * D), BF16),
        name="cast_weights",
    )(*args)


def _s5conv_kernel(has_state, bt, tt, *refs):
    refs = list(refs)
    (x_ref, nw_ref, ws_ref, wg_ref, are_ref, aim_ref, bbre_ref, bbim_ref, cre_ref, cim_ref,
     dsk_ref, wglu_ref, wps_ref, cw_ref, wpc_ref) = refs[:15]
    refs = refs[15:]
    if has_state:
        s0re_ref, s0im_ref, cv0_ref = refs[:3]
        refs = refs[3:]
    msc_ref, sre_ref, sim_ref, cvo_ref = refs[:4]
    h_scr, ps_scr, g_scr, uslab, utb, bu_scr, yslab, y_scr, vhalo = refs[4:]

    r = bt * tt
    pt = _pitch(tt)
    pb = _pitch(bt)
    rc = min(ROW_CHUNK, r)
    nchunk = r // rc
    ti = pl.program_id(1)

    def merge_gates(ci):
        rows = slice(ci * rc, (ci + 1) * rc)
        g_scr[rows, :] = _sigmoid(_dot(h_scr[rows, :], wg_ref[...]))

    @pl.when(ti == 0)
    def _init():
        if has_state:
            sre_ref[...] = s0re_ref[...]
            sim_ref[...] = s0im_ref[...]
            vhalo[:, 6:8, :] = cv0_ref[...]
        else:
            sre_ref[...] = jnp.zeros_like(sre_ref)
            sim_ref[...] = jnp.zeros_like(sim_ref)
            vhalo[:, 6:8, :] = jnp.zeros((bt, 2, CW), F32)

    x = x_ref[...].reshape(r, D)
    h = _rmsnorm(x, nw_ref[...]).astype(BF16)
    h_scr[...] = h
    ps_scr[...] = _dot(h, ws_ref[...])


    for b in range(bt):
        for j in range(4):
            uslab[j, b * pt:b * pt + tt, :] = ps_scr[b * tt:(b + 1) * tt, j * 128:(j + 1) * 128]
    for t in range(tt):
        for bs in range(bt // 8):
            dst = t * bt + bs * 8
            for j in range(4):
                utb[dst:dst + 8, j * 128:(j + 1) * 128] = uslab[j, pl.ds(bs * 8 * pt + t, 8, stride=pt), :]
    for ci in range(nchunk // 2):
        merge_gates(ci)

    for hf in range(2):
        uh = utb[:, hf * 256:(hf + 1) * 256].astype(BF16)
        bu_scr[:, hf * 1024:(hf + 1) * 1024] = _dot(uh, bbre_ref[hf])
        bu_scr[:, S5N + hf * 1024:S5N + (hf + 1) * 1024] = _dot(uh, bbim_ref[hf])
    for ci in range(nchunk // 2, nchunk):
        merge_gates(ci)

    v = ps_scr[:, 1536:2048] * ps_scr[:, 2048:2560]
    vhalo[:, 8:8 + tt, :] = v.reshape(bt, tt, CW)
    cw = cw_ref[...]
    yc = (cw[0:1, :] * vhalo[:, 6:6 + tt, :] + cw[1:2, :] * vhalo[:, 7:7 + tt, :]
          + cw[2:3, :] * vhalo[:, 8:8 + tt, :])
    ps_scr[:, 2048:2560] = yc.reshape(r, CW)
    new_halo = vhalo[:, tt + 6:tt + 8, :]
    vhalo[:, 6:8, :] = new_halo
    cvo_ref[...] = new_halo

    for ci in range(nchunk):
        rows = slice(ci * rc, (ci + 1) * rc)
        tc_ = ps_scr[rows, 1024:1536] * ps_scr[rows, 2048:2560] * _silu(ps_scr[rows, 2560:3072])
        ps_scr[rows, 1024:2048] = g_scr[rows, D:] * _dot(tc_.astype(BF16), wpc_ref[...])

    for q in range(4):
        c_re = slice(q * 512, (q + 1) * 512)
        c_im = slice(S5N + q * 512, S5N + (q + 1) * 512)
        ar = jnp.broadcast_to(are_ref[:, c_re], (8, 512))
        ai = jnp.broadcast_to(aim_ref[:, c_re], (8, 512))
        for bs in range(bt // 8):
            r0 = bs * 8
            sr = sre_ref[r0:r0 + 8, c_re]
            si = sim_ref[r0:r0 + 8, c_re]
            for t in range(tt):
                row = t * bt + r0
                nr = ar * sr - ai * si + bu_scr[row:row + 8, c_re]
                ni = ar * si + ai * sr + bu_scr[row:row + 8, c_im]
                bu_scr[row:row + 8, c_re] = nr
                bu_scr[row:row + 8, c_im] = ni
                sr, si = nr, ni
            sre_ref[r0:r0 + 8, c_re] = sr
            sim_ref[r0:r0 + 8, c_re] = si

    for m in range(2):
        sre = bu_scr[:, m * 1024:(m + 1) * 1024].astype(BF16)
        sim = bu_scr[:, S5N + m * 1024:S5N + (m + 1) * 1024].astype(BF16)
        ytb = _dot(sre, cre_ref[m]) - _dot(sim, cim_ref[m])
        for t in range(tt):
            for jj in range(2):
                yslab[2 * m + jj, t * pb:t * pb + bt, :] = ytb[t * bt:(t + 1) * bt, jj * 128:(jj + 1) * 128]
    for b in range(bt):
        for ts in range(tt // 8):
            dst = b * tt + ts * 8
            for j in range(4):
                y_scr[dst:dst + 8, j * 128:(j + 1) * 128] = yslab[j, pl.ds(ts * 8 * pb + b, 8, stride=pb), :]

    nb = rc // tt if rc >= tt else 0
    dsk = dsk_ref[...]
    for ci in range(nchunk):
        rows = slice(ci * rc, (ci + 1) * rc)
        y = y_scr[rows, :] + dsk * ps_scr[rows, 0:512]
        gl = _gelu_tanh(y)
        glu = gl * _sigmoid(_dot(gl.astype(BF16), wglu_ref[...]))
        ts_ = glu * _silu(ps_scr[rows, 512:1024])
        ys = _dot(ts_.astype(BF16), wps_ref[...])
        out = g_scr[rows, :D] * ys + ps_scr[rows, 1024:2048]
        if nb:
            msc_ref[ci * nb:(ci + 1) * nb] = out.reshape(nb, tt, D)
        else:
            per = tt // rc
            msc_ref[ci // per, (ci % per) * rc:(ci % per + 1) * rc, :] = out


def _s5conv_call(layer, x, weights, state, bt, tt):
    b, t, _ = x.shape
    r = bt * tt
    has_state = state is not None
    wspec = lambda shape: pl.BlockSpec((None,) + shape, lambda bi, ti: (layer,) + (0,) * len(shape),
                                       pipeline_mode=pl.Buffered(1))
    tile = pl.BlockSpec((bt, tt, D), lambda bi, ti: (bi, ti, 0))
    in_specs = [tile, wspec((1, D)), wspec((D, 3072)), wspec((D, 2048)), wspec((1, S5N)), wspec((1, S5N)),
                wspec((2, 256, 1024)), wspec((2, 256, 1024)), wspec((2, 1024, 256)), wspec((2, 1024, 256)),
                wspec((1, S5W)), wspec((S5W, S5W)), wspec((S5W, D)), wspec((CK, CW)), wspec((CW, D))]
    args = [x] + list(weights)
    if has_state:
        in_specs += [pl.BlockSpec((None, bt, S5N), lambda bi, ti: (layer, bi, 0)),
                     pl.BlockSpec((None, bt, S5N), lambda bi, ti: (layer, bi, 0)),
                     pl.BlockSpec((None, bt, CK - 1, CW), lambda bi, ti: (layer, bi, 0, 0))]
        args += list(state)
    out_specs = [tile,
                 pl.BlockSpec((bt, S5N), lambda bi, ti: (bi, 0)),
                 pl.BlockSpec((bt, S5N), lambda bi, ti: (bi, 0)),
                 pl.BlockSpec((bt, CK - 1, CW), lambda bi, ti: (bi, 0, 0))]
    out_shape = [jax.ShapeDtypeStruct((b, t, D), F32), jax.ShapeDtypeStruct((b, S5N), F32),
                 jax.ShapeDtypeStruct((b, S5N), F32), jax.ShapeDtypeStruct((b, CK - 1, CW), F32)]
    scratch = [pltpu.VMEM((r, D), BF16),
               pltpu.VMEM((r, 3072), F32),
               pltpu.VMEM((r, 2 * D), F32),
               pltpu.VMEM((4, bt * _pitch(tt), 128), F32),
               pltpu.VMEM((r, S5W), F32),
               pltpu.VMEM((r, 2 * S5N), F32),
               pltpu.VMEM((4, tt * _pitch(bt), 128), F32),
               pltpu.VMEM((r, S5W), F32),
               pltpu.VMEM((bt, tt + 8, CW), F32)]
    return pl.pallas_call(
        functools.partial(_s5conv_kernel, has_state, bt, tt),
        grid=(b // bt, t // tt),
        in_specs=in_specs, out_specs=out_specs, out_shape=out_shape, scratch_shapes=scratch,
        compiler_params=pltpu.CompilerParams(dimension_semantics=("arbitrary", "arbitrary"),
                                             vmem_limit_bytes=VMEM_LIMIT),
        name="s5conv_state" if has_state else "s5conv",
    )(*args)


def _seg_scan(x, pos, seg, op, fill):
    s = 1
    while s < seg:
        x = op(x, jnp.where(pos >= s, pltpu.roll(x, s, 1), fill))
        s *= 2
    return x


def _seg_bcast_last(x, pos, seg):
    n = x.shape[1]
    s = 1
    while s < seg:
        x = jnp.where(pos + s <= seg - 1, pltpu.roll(x, n - s, 1), x)
        s *= 2
    return x


def _mlstm_kernel(has_state, has_alias, final, single_tile, bt, tt, nsub, *refs):
    refs = list(refs)
    x_ref, msc_ref, nw_ref = refs[:3]
    wm_refs = refs[3:8]
    wif_ref, bias_ref, wgm_ref, hnw_ref, wpm_ref, wout_ref, fnw_ref = refs[8:15]
    refs = refs[15:]
    if has_state:
        c0_ref, n0_ref, m0_ref = refs[:3]
        refs = refs[3:]
    if has_alias:
        refs = refs[1:]
    y_ref, c_ref, n_ref, m_ref = refs[:4]
    h_scr, p_scr, hm_scr, sg_scr, qkv_scr = refs[4:9]
    if nsub > 1:
        zt_scr, gd_scr = refs[9:]

    r = bt * tt
    ro = r * nsub
    rp = max(r, 128)
    step = pl.program_id(1)

    def init_state():
        if has_state:
            c_ref[...] = c0_ref[...]
            n_ref[...] = n0_ref[...]
            m_ref[...] = m0_ref[...]
        else:
            c_ref[...] = jnp.zeros_like(c_ref)
            n_ref[...] = jnp.zeros_like(n_ref)
            m_ref[...] = jnp.zeros_like(m_ref)

    def normalise():
        hh = _rmsnorm(x_ref[...].reshape(ro, D), nw_ref[...]).astype(BF16)
        h_scr[...] = hh
        return hh

    def project(hh):
        for k, w_ref in enumerate(wm_refs):
            res = _dot(hh, w_ref[...])
            if tt > SHORT_SEQ and k < 3:
                qkv_scr[:, k * D:(k + 1) * D] = (res * K_SCALE if k == 1 else res).astype(BF16)
            else:
                p_scr[:, k * D:(k + 1) * D] = res

    def merge_gate(piece):
        cols = slice(piece * (D // GATE_PIECES), (piece + 1) * (D // GATE_PIECES))
        sg_scr[:, cols] = _sigmoid(_dot(h_scr[...], wgm_ref[:, cols]))

    def gate_chain(hrows, mprev):
        n = hrows.shape[0]
        gt = _dot_nt(wif_ref[...], hrows) + bias_ref[...]
        pos = jnp.bitwise_and(lax.broadcasted_iota(jnp.int32, (8, n), 1), tt - 1)
        bcum = _seg_scan(_log_sigmoid(gt[8:16, :]), pos, tt, jnp.add, 0.0)
        g = gt[0:8, :] - bcum
        mrun = jnp.maximum(mprev, _seg_scan(g, pos, tt, jnp.maximum, NEG))
        mlast = _seg_bcast_last(mrun, pos, tt)
        winter = jnp.exp(mprev - mrun)
        efloor = jnp.exp(-(bcum + mrun))
        wrow = jnp.exp(g - mlast)
        decay = jnp.exp(mprev - mlast)
        mnew = _seg_bcast_last(bcum, pos, tt) + mlast
        zt = jnp.concatenate([mrun, winter, efloor, wrow], axis=0).T
        return g, decay, zt, mnew

    if not single_tile:
        pl.when(step == 0)(init_state)
        c_prev, n_prev, m_start = c_ref, n_ref, (lambda: m_ref[...])
    elif has_state:
        c_prev, n_prev, m_start = c0_ref, n0_ref, (lambda: m0_ref[...])
    else:
        c_ref[...] = jnp.zeros_like(c_ref)
        n_ref[...] = jnp.zeros_like(n_ref)
        c_prev, n_prev, m_start = c_ref, n_ref, (lambda: jnp.zeros(m_ref.shape, F32))

    if nsub == 1:
        base = 0
        h = normalise()
        hp = h if rp == r else jnp.concatenate([h, jnp.zeros((rp - r, D), BF16)], axis=0)
        g, decay, zt, mnew = gate_chain(hp, m_start())
        m_ref[...] = mnew
        project(h)
    else:
        base = pl.multiple_of(step * r, r)

        @pl.when(step == 0)
        def _block_prologue():
            hh = normalise()
            ga, da, za, mnew_all = gate_chain(hh, m_start())
            project(hh)
            m_ref[...] = mnew_all
            zt_scr[:, 0:32] = za
            for s in range(nsub):
                gd_scr[16 * s:16 * s + 8, 0:r] = ga[:, s * r:(s + 1) * r]
                gd_scr[16 * s + 8:16 * s + 16, 0:r] = da[:, s * r:(s + 1) * r]
            for piece in range(GATE_PIECES):
                merge_gate(piece)

        zt = zt_scr[pl.ds(base, r), 0:32]
        gd = gd_scr[pl.ds(pl.multiple_of(step * 16, 16), 16), :]
        g = gd[0:8, 0:r]
        decay = gd[8:16, 0:r]

    if tt <= SHORT_SEQ:
        nst = NH * r
        prow = slice(0, r) if nsub == 1 else pl.ds(base, r)
        stack = lambda off: jnp.concatenate(
            [p_scr[prow, off + hd * DH:off + (hd + 1) * DH] for hd in range(NH)], axis=0)
        qs = stack(0)
        ks = stack(D) * K_SCALE
        vs = stack(2 * D)
        col = lambda c0: jnp.concatenate([zt[0:r, c0 + hd:c0 + hd + 1] for hd in range(NH)], axis=0)
        mcol, wi, ef, wc = col(0), col(8), col(16), col(24)
        grow = jnp.concatenate([g[hd:hd + 1, 0:r] for hd in range(NH)], axis=1)
        row_i = lax.broadcasted_iota(jnp.int32, (nst, nst), 0)
        col_i = lax.broadcasted_iota(jnp.int32, (nst, nst), 1)
        shift = tt.bit_length() - 1
        same = lax.shift_right_logical(row_i, shift) == lax.shift_right_logical(col_i, shift)
        arg = jnp.where(col_i <= row_i, grow - mcol, NEG)
        dm = jnp.exp(jnp.where(same, arg, NEG))
        sc = _dot_nt(qs.astype(BF16), ks.astype(BF16)) * dm
        intra = _dot(sc.astype(BF16), vs.astype(BF16))
        pairs = [(hd, b) for hd in range(NH) for b in range(bt)]
        blk = lambda hd, b: slice(hd * r + b * tt, hd * r + (b + 1) * tt)
        inter = jnp.concatenate([_dot(qs[blk(hd, b)].astype(BF16), c_prev[b, hd].astype(BF16))
                                 for hd, b in pairs], axis=0)
        nfull = jnp.concatenate([jnp.broadcast_to(n_prev[b, hd:hd + 1, :], (tt, DH)) for hd, b in pairs], axis=0)
        num = wi * inter + intra
        den = wi * jnp.sum(qs * nfull, axis=-1, keepdims=True) + jnp.sum(sc, axis=-1, keepdims=True)
        hh = num * (1.0 / jnp.maximum(jnp.abs(den), ef))
        for hd in range(NH):
            hm_scr[prow, hd * DH:(hd + 1) * DH] = hh[hd * r:(hd + 1) * r]
        wk = ks * wc
        for hd, b in pairs:
            dec = decay[hd:hd + 1, b * tt:b * tt + 1]
            upd = _dot_tn(wk[blk(hd, b)].astype(BF16), vs[blk(hd, b)].astype(BF16))
            c_ref[b, hd] = dec * c_prev[b, hd] + upd
            n_ref[b, hd:hd + 1, :] = (dec * n_prev[b, hd:hd + 1, :]
                                      + jnp.sum(wk[blk(hd, b)], axis=0, keepdims=True))
        if nsub == 1:
            for piece in range(GATE_PIECES):
                merge_gate(piece)

    row_i = lax.broadcasted_iota(jnp.int32, (tt, tt), 0)
    col_i = lax.broadcasted_iota(jnp.int32, (tt, tt), 1)
    causal = row_i >= col_i

    for b in range(bt if tt > SHORT_SEQ else 0):
        rows = slice(b * tt, (b + 1) * tt)
        prow = rows if nsub == 1 else pl.ds(base + b * tt, tt)
        for hd in range(NH):
            cs = slice(hd * DH, (hd + 1) * DH)
            qb = qkv_scr[prow, cs]
            kb = qkv_scr[prow, D + hd * DH:D + (hd + 1) * DH]
            vb = qkv_scr[prow, 2 * D + hd * DH:2 * D + (hd + 1) * DH]
            qf = qb.astype(F32)
            kf = kb.astype(F32)
            mcol = zt[rows, hd:hd + 1]
            wi = zt[rows, 8 + hd:9 + hd]
            ef = zt[rows, 16 + hd:17 + hd]
            wc = zt[rows, 24 + hd:25 + hd]
            grow = g[hd:hd + 1, b * tt:(b + 1) * tt]
            dm = jnp.exp(jnp.where(causal, grow - mcol, NEG))
            s_raw = _dot_nt(qb, kb)
            cf = c_prev[b, hd]
            nrow = n_prev[b, hd:hd + 1, :]
            inter = _dot(qb, cf.astype(BF16))
            dec = decay[hd:hd + 1, b * tt:b * tt + 1]
            wk = kf * wc
            c_ref[b, hd] = dec * cf + _dot_tn(wk.astype(BF16), vb)
            n_ref[b, hd:hd + 1, :] = dec * nrow + jnp.sum(wk, axis=0, keepdims=True)
            sc = s_raw * dm
            num = wi * inter + _dot(sc.astype(BF16), vb)
            den = wi * jnp.sum(qf * nrow, axis=-1, keepdims=True) + jnp.sum(sc, axis=-1, keepdims=True)
            hm_scr[prow, cs] = num * (1.0 / jnp.maximum(jnp.abs(den), ef))
            done = b * NH + hd + 1
            if nsub == 1 and (done * GATE_PIECES) % (bt * NH) == 0:
                merge_gate(done * GATE_PIECES // (bt * NH) - 1)

    def epilogue():
        rc = min(MLSTM_ROW_CHUNK, ro)
        hnw = hnw_ref[...]
        for ci in range(ro // rc):
            rows = slice(ci * rc, (ci + 1) * rc)
            hmv = hm_scr[rows, :] * _sigmoid(p_scr[rows, 3 * D:4 * D])
            parts = []
            for hd in range(NH):
                hh = hmv[:, hd * DH:(hd + 1) * DH]
                mu = jnp.mean(hh, axis=-1, keepdims=True)
                dv = hh - mu
                var = jnp.mean(dv * dv, axis=-1, keepdims=True)
                parts.append(dv * lax.rsqrt(var + EPS) * hnw[:, hd * DH:(hd + 1) * DH])
            hn = jnp.concatenate(parts, axis=1)
            tm = hn * _silu(p_scr[rows, 4 * D:5 * D])
            ym = _dot(tm.astype(BF16), wpm_ref[...])
            nb = rc // tt if rc >= tt else 0
            if nb:
                msc = msc_ref[ci * nb:(ci + 1) * nb].reshape(rc, D)
                xr = x_ref[ci * nb:(ci + 1) * nb].reshape(rc, D)
            else:
                per = tt // rc
                msc = msc_ref[ci // per, (ci % per) * rc:(ci % per + 1) * rc, :]
                xr = x_ref[ci // per, (ci % per) * rc:(ci % per + 1) * rc, :]
            merged = msc + sg_scr[rows, :] * ym
            y = xr + _dot(merged.astype(BF16), wout_ref[...])
            if final:
                y = _rmsnorm(y, fnw_ref[...])
            if nb:
                y_ref[ci * nb:(ci + 1) * nb] = y.reshape(nb, tt, D)
            else:
                y_ref[ci // per, (ci % per) * rc:(ci % per + 1) * rc, :] = y

    if nsub == 1:
        epilogue()
    else:
        pl.when(step == nsub - 1)(epilogue)


def _mlstm_call(layer, depth, x, msc, weights, state, c_stack, bt, tt):
    b, t, _ = x.shape
    r = bt * tt
    rp = max(r, 128)
    nbt = b // bt
    has_state = state is not None
    has_alias = c_stack is not None
    final = layer == depth - 1
    nsub = max(1, min(nbt, PROJ_ROWS // r)) if t == tt else 1
    assert nbt % nsub == 0
    ro = r * nsub
    rpo = max(ro, 128)
    if nsub == 1:
        grid = (nbt, t // tt)
        tile_idx = lambda bi, si: (bi, si, 0)
        sub_idx = lambda bi, si: bi
    else:
        grid = (nbt // nsub, nsub)
        tile_idx = lambda bi, si: (bi, 0, 0)
        sub_idx = lambda bi, si: bi * nsub + si
    wspec = lambda shape: pl.BlockSpec((None,) + shape, lambda bi, si: (layer,) + (0,) * len(shape),
                                       pipeline_mode=pl.Buffered(1))
    tile = pl.BlockSpec((bt * nsub, tt, D), tile_idx)
    wcol = lambda k: pl.BlockSpec((None, D, D), lambda bi, si: (layer, 0, k), pipeline_mode=pl.Buffered(1))
    nw, w_main, w_if, bias, w_gates, hnw, wpm, wout, fnw = weights
    in_specs = ([tile, tile, wspec((1, D))] + [wcol(OFF_M // D + k) for k in range(5)]
                + [wspec((16, D)), wspec((16, 1)), wcol(2), wspec((1, D)), wspec((D, D)), wspec((D, D)),
                   pl.BlockSpec((1, D), lambda bi, si: (0, 0), pipeline_mode=pl.Buffered(1))])
    args = [x, msc, nw] + [w_main] * 5 + [w_if, bias, w_gates, hnw, wpm, wout, fnw]
    if has_state:
        in_specs += [pl.BlockSpec((None, bt, NH, DH, DH), lambda bi, si: (layer, sub_idx(bi, si), 0, 0, 0)),
                     pl.BlockSpec((None, bt, NH, DH), lambda bi, si: (layer, sub_idx(bi, si), 0, 0)),
                     pl.BlockSpec((None, None, 8, rpo), lambda bi, si: (layer, bi, 0, 0))]
        c0, n0, m0 = state
        args += [c0, n0, _expand_m(m0, bt * nsub, tt)]
    aliases = {}
    if has_alias:
        aliases = {len(args): 1}
        in_specs.append(pl.BlockSpec(memory_space=pl.ANY))
        args.append(c_stack)
    out_specs = [tile,
                 pl.BlockSpec((None, bt, NH, DH, DH), lambda bi, si: (layer, sub_idx(bi, si), 0, 0, 0)),
                 pl.BlockSpec((bt, NH, DH), lambda bi, si: (sub_idx(bi, si), 0, 0)),
                 pl.BlockSpec((None, 8, rpo), lambda bi, si: (bi, 0, 0))]
    out_shape = [jax.ShapeDtypeStruct((b, t, D), F32), jax.ShapeDtypeStruct((depth, b, NH, DH, DH), F32),
                 jax.ShapeDtypeStruct((b, NH, DH), F32), jax.ShapeDtypeStruct((nbt // nsub, 8, rpo), F32)]
    scratch = [pltpu.VMEM((ro, D), BF16),
               pltpu.VMEM((ro, 5 * D), F32),
               pltpu.VMEM((ro, D), F32),
               pltpu.VMEM((ro, D), F32),
               pltpu.VMEM((ro if tt > SHORT_SEQ else 16, 3 * D), BF16)]
    if nsub > 1:
        scratch += [pltpu.VMEM((ro, 128), F32),
                    pltpu.VMEM((16 * nsub, 128), F32)]
    y, c_new, n_new, m_rows = pl.pallas_call(
        functools.partial(_mlstm_kernel, has_state, has_alias, final, t == tt, bt, tt, nsub),
        grid=grid,
        in_specs=in_specs, out_specs=out_specs, out_shape=out_shape, scratch_shapes=scratch,
        input_output_aliases=aliases,
        compiler_params=pltpu.CompilerParams(dimension_semantics=("arbitrary", "arbitrary"),
                                             vmem_limit_bytes=VMEM_LIMIT),
        name="mlstm_state" if has_state else "mlstm",
    )(*args)
    return y, c_new, n_new, _collapse_m(m_rows, bt * nsub, tt)


def _expand_m(m, bt, tt):
    depth, b, _ = m.shape
    r = bt * tt
    rp = max(r, 128)
    v = jnp.repeat(m.reshape(depth, b // bt, bt, NH).transpose(0, 1, 3, 2), tt, axis=-1)
    return jnp.pad(v, ((0, 0), (0, 0), (0, 8 - NH), (0, rp - r)))


def _collapse_m(mrows, bt, tt):
    nbt = mrows.shape[0]
    return mrows[:, :NH, 0:bt * tt:tt].transpose(0, 2, 1).reshape(nbt * bt, NH)


def kernel(x_prompt, x_sample, state_ssm_re, state_ssm_im, state_conv, state_mlstm_c, state_mlstm_n, state_mlstm_m, norm_w, w_in, i_bias, f_bias, lam_re, lam_im, log_dt, b_re, b_im, c_re, c_im, d_skip, w_glu, w_proj_s, conv_w, w_proj_c, mlstm_norm_w, w_proj_m, w_out, final_norm_w):
    depth = norm_w.shape[0]
    bp, tp, _ = x_prompt.shape
    bs, ts, _ = x_sample.shape
    assert tp % PROMPT_CHUNK == 0 and ts % 8 == 0 and ts & (ts - 1) == 0 and bp % 8 == 0 and bs % 64 == 0

    a_re, a_im, bb_re, bb_im, cp_re, cp_im = _s5_discretise(lam_re, lam_im, log_dt, b_re, b_im, c_re, c_im)
    row = lambda a: a.reshape(depth, 1, a.shape[-1])
    nw = row(norm_w)
    w_t = jnp.swapaxes(w_in, 1, 2)
    w_main = _cast_columns(w_t, 0, OFF_IF // D)
    w_gates = _cast_columns(w_t, OFF_G, 3)
    gate_rows = lambda a: jnp.pad(a.reshape(depth, 2, NH, -1), ((0, 0), (0, 0), (0, 8 - NH), (0, 0))).reshape(depth, 16, -1)
    w_if = gate_rows(w_t[:, OFF_IF:OFF_G, :]).astype(BF16)
    bias = gate_rows(jnp.concatenate([i_bias, f_bias], axis=-1)[:, :, None])
    wa = [nw, w_main, w_gates, a_re, a_im, bb_re, bb_im, cp_re, cp_im, row(d_skip),
          w_glu.astype(BF16), w_proj_s.astype(BF16), conv_w, w_proj_c.astype(BF16)]
    wb = [nw, w_main, w_if, bias, w_gates, row(mlstm_norm_w), w_proj_m.astype(BF16), w_out.astype(BF16),
          final_norm_w.reshape(1, D)]

    sb_a, sb_b = 64, 4
    pb_a, pb_b = 8, 2
    st_a = (state_ssm_re.reshape(depth, bs, S5N), state_ssm_im.reshape(depth, bs, S5N), state_conv)
    st_b = (state_mlstm_c, state_mlstm_n, state_mlstm_m)

    yp, ys = x_prompt, x_sample
    cp = cs = None
    outs_p = [[] for _ in range(5)]
    outs_s = [[] for _ in range(5)]
    for l in range(depth):
        msc, sre, sim, cv = _s5conv_call(l, yp, wa, None, pb_a, S5_TILE_T)
        yp, cp, nn, mm = _mlstm_call(l, depth, yp, msc, wb, None, cp, pb_b, PROMPT_CHUNK)
        for i, v in enumerate((sre.reshape(bp, S5G, S5P), sim.reshape(bp, S5G, S5P), cv, nn, mm)):
            outs_p[i].append(v)
        msc, sre, sim, cv = _s5conv_call(l, ys, wa, st_a, sb_a, ts)
        ys, cs, nn, mm = _mlstm_call(l, depth, ys, msc, wb, st_b, cs, sb_b, ts)
        for i, v in enumerate((sre.reshape(bs, S5G, S5P), sim.reshape(bs, S5G, S5P), cv, nn, mm)):
            outs_s[i].append(v)
    sp = [jnp.stack(o) for o in outs_p]
    ss = [jnp.stack(o) for o in outs_s]
    return (yp, ys, sp[0], ss[0], sp[1], ss[1], sp[2], ss[2], cp, cs, sp[3], ss[3], sp[4], ss[4])
```

```python
import functools
import math

import jax
import jax.numpy as jnp
from jax import lax
from jax.experimental import pallas as pl
from jax.experimental.pallas import tpu as pltpu

D = 1024
S5W = 512
S5G = 32
S5P = 64
S5C = 16
S5N = S5G * S5P
CW = 512
CK = 3
NH = 4
DH = 256
PROMPT_CHUNK = 256
S5_TILE_T = 64
SHORT_SEQ = 32
EPS = 1e-6
NEG = -1e30
K_SCALE = DH ** -0.5

OFF_S = 0
OFF_M = 3072
OFF_IF = 8192
OFF_G = 8200
IN_W = 11272

F32 = jnp.float32
BF16 = jnp.bfloat16
VMEM_LIMIT = 58 * 1024 * 1024
ROW_CHUNK = 256
GATE_PIECES = 4
MLSTM_ROW_CHUNK = 512
PROJ_ROWS = 256


def _sigmoid(x):
    return 1.0 / (1.0 + jnp.exp(-x))


def _silu(x):
    return x * _sigmoid(x)


def _gelu_tanh(x):
    return 0.5 * x * (1.0 + jnp.tanh(math.sqrt(2.0 / math.pi) * (x + 0.044715 * (x * x * x))))


def _log_sigmoid(x):
    return jnp.minimum(x, 0.0) - jnp.log1p(jnp.exp(-jnp.abs(x)))


def _rmsnorm(x, w):
    return x * lax.rsqrt(jnp.mean(x * x, axis=-1, keepdims=True) + EPS) * w


def _dot(a, b):
    return jnp.dot(a, b, preferred_element_type=F32)


def _dot_nt(a, b):
    return lax.dot_general(a, b, (((1,), (1,)), ((), ())), preferred_element_type=F32)


def _dot_tn(a, b):
    return lax.dot_general(a, b, (((0,), (0,)), ((), ())), preferred_element_type=F32)


def _pitch(n):
    p = n + 8
    return p if (p // 8) % 2 == 1 else p + 8


def _s5_disc_kernel(lre_ref, lim_ref, ldt_ref, bre_ref, bim_ref, cre_ref, cim_ref,
                    are_ref, aim_ref, bbre_ref, bbim_ref, cpre_ref, cpim_ref):
    lr = lre_ref[...]
    li = lim_ref[...]
    dt = jnp.exp(ldt_ref[...])
    ea = jnp.exp(lr * dt)
    ar = ea * jnp.cos(li * dt)
    ai = ea * jnp.sin(li * dt)
    are_ref[...] = ar
    aim_ref[...] = ai
    nr = ar - 1.0
    inv = 1.0 / (lr * lr + li * li)
    cr = (nr * lr + ai * li) * inv
    ci = (ai * lr - nr * li) * inv
    half = S5N // 2
    shr = lax.shift_right_logical
    bdiag = (shr(lax.broadcasted_iota(jnp.int32, (256, half), 0), 4)
             == shr(lax.broadcasted_iota(jnp.int32, (256, half), 1), 6))
    cdiag = (shr(lax.broadcasted_iota(jnp.int32, (half, 256), 0), 6)
             == shr(lax.broadcasted_iota(jnp.int32, (half, 256), 1), 4))
    rep = lambda x: jnp.concatenate([x] * 16, axis=0)
    for h in range(2):
        crh = cr[:, h * half:(h + 1) * half]
        cih = ci[:, h * half:(h + 1) * half]
        br = bre_ref[h]
        bi = bim_ref[h]
        bbre_ref[h] = jnp.where(bdiag, rep(crh * br - cih * bi), 0.0).astype(BF16)
        bbim_ref[h] = jnp.where(bdiag, rep(crh * bi + cih * br), 0.0).astype(BF16)
        cpre_ref[h] = jnp.where(cdiag, rep(cre_ref[h]), 0.0).astype(BF16)
        cpim_ref[h] = jnp.where(cdiag, rep(cim_ref[h]), 0.0).astype(BF16)


def _s5_discretise(lam_re, lam_im, log_dt, b_re, b_im, c_re, c_im):
    depth = lam_re.shape[0]

    def rows_b(b):
        return jnp.swapaxes(b.reshape(depth, 2, 16 * S5P, S5C), 2, 3)

    def rows_c(c):
        return jnp.swapaxes(c.reshape(depth, 2, 16 * S5C, S5P), 2, 3)

    row = lambda a: a.reshape(depth, 1, S5N)
    ldt = jnp.broadcast_to(log_dt[:, :, None], (depth, S5G, S5P))
    vec = pl.BlockSpec((None, 1, S5N), lambda l: (l, 0, 0))
    bmat = pl.BlockSpec((None, 2, 256, 1024), lambda l: (l, 0, 0, 0))
    cmat = pl.BlockSpec((None, 2, 1024, 256), lambda l: (l, 0, 0, 0))
    brow = pl.BlockSpec((None, 2, S5C, 1024), lambda l: (l, 0, 0, 0))
    crow = pl.BlockSpec((None, 2, S5P, 256), lambda l: (l, 0, 0, 0))
    return pl.pallas_call(
        _s5_disc_kernel,
        grid=(depth,),
        in_specs=[vec, vec, vec, brow, brow, crow, crow],
        out_specs=[vec, vec, bmat, bmat, cmat, cmat],
        out_shape=[jax.ShapeDtypeStruct((depth, 1, S5N), F32), jax.ShapeDtypeStruct((depth, 1, S5N), F32),
                   jax.ShapeDtypeStruct((depth, 2, 256, 1024), BF16),
                   jax.ShapeDtypeStruct((depth, 2, 256, 1024), BF16),
                   jax.ShapeDtypeStruct((depth, 2, 1024, 256), BF16),
                   jax.ShapeDtypeStruct((depth, 2, 1024, 256), BF16)],
        name="s5_discretise",
    )(row(lam_re), row(lam_im), row(ldt), rows_b(b_re), rows_b(b_im), rows_c(c_re), rows_c(c_im))


def _cast_kernel(shift, *refs):
    if shift:
        wa_ref, wb_ref, o_ref = refs
        w = jnp.concatenate([wa_ref[shift:, :], wb_ref[...]], axis=0)
    else:
        w_ref, o_ref = refs
        w = w_ref[...]
    o_ref[...] = w.T.astype(BF16)


def _cast_columns(w_t, first_col, n_blocks):
    depth = w_t.shape[0]
    k0, shift = divmod(first_col, D)
    assert shift % 8 == 0
    in_specs = [pl.BlockSpec((None, D, D), lambda l, k: (l, k0 + k, 0))]
    args = [w_t]
    if shift:
        assert D % shift == 0
        in_specs.append(pl.BlockSpec((None, shift, D), lambda l, k: (l, (k0 + k + 1) * (D // shift), 0)))
        args.append(w_t)
    return pl.pallas_call(
        functools.partial(_cast_kernel, shift),
        grid=(depth, n_blocks),
        in_specs=in_specs,
        out_specs=pl.BlockSpec((None, D, D), lambda l, k: (l, 0, k)),
        out_shape=jax.ShapeDtypeStruct((depth, D, n_blocks * D), BF16),
        name="cast_weights",
    )(*args)


def _s5conv_kernel(has_state, bt, tt, *refs):
    refs = list(refs)
    (x_ref, nw_ref, ws_ref, wg_ref, are_ref, aim_ref, bbre_ref, bbim_ref, cre_ref, cim_ref,
     dsk_ref, wglu_ref, wps_ref, cw_ref, wpc_ref) = refs[:15]
    refs = refs[15:]
    if has_state:
        s0re_ref, s0im_ref, cv0_ref = refs[:3]
        refs = refs[3:]
    msc_ref, sre_ref, sim_ref, cvo_ref = refs[:4]
    h_scr, ps_scr, g_scr, uslab, utb, bu_scr, yslab, y_scr, vhalo = refs[4:]

    r = bt * tt
    pt = _pitch(tt)
    pb = _pitch(bt)
    rc = min(ROW_CHUNK, r)
    nchunk = r // rc
    ti = pl.program_id(1)

    def merge_gates(ci):
        rows = slice(ci * rc, (ci + 1) * rc)
        g_scr[rows, :] = _sigmoid(_dot(h_scr[rows, :], wg_ref[...]))

    @pl.when(ti == 0)
    def _init():
        if has_state:
            sre_ref[...] = s0re_ref[...]
            sim_ref[...] = s0im_ref[...]
            vhalo[:, 6:8, :] = cv0_ref[...]
        else:
            sre_ref[...] = jnp.zeros_like(sre_ref)
            sim_ref[...] = jnp.zeros_like(sim_ref)
            vhalo[:, 6:8, :] = jnp.zeros((bt, 2, CW), F32)

    x = x_ref[...].reshape(r, D)
    h = _rmsnorm(x, nw_ref[...]).astype(BF16)
    h_scr[...] = h
    ps_scr[...] = _dot(h, ws_ref[...])


    for b in range(bt):
        for j in range(4):
            uslab[j, b * pt:b * pt + tt, :] = ps_scr[b * tt:(b + 1) * tt, j * 128:(j + 1) * 128]
    for t in range(tt):
        for bs in range(bt // 8):
            dst = t * bt + bs * 8
            for j in range(4):
                utb[dst:dst + 8, j * 128:(j + 1) * 128] = uslab[j, pl.ds(bs * 8 * pt + t, 8, stride=pt), :]
    for hf in range(2):
        uh = utb[:, hf * 256:(hf + 1) * 256].astype(BF16)
        bu_scr[:, hf * 1024:(hf + 1) * 1024] = _dot(uh, bbre_ref[hf])
        bu_scr[:, S5N + hf * 1024:S5N + (hf + 1) * 1024] = _dot(uh, bbim_ref[hf])
        for ci in range(hf * nchunk // 2, (hf + 1) * nchunk // 2):
            merge_gates(ci)

    v = ps_scr[:, 1536:2048] * ps_scr[:, 2048:2560]
    vhalo[:, 8:8 + tt, :] = v.reshape(bt, tt, CW)
    cw = cw_ref[...]
    yc = (cw[0:1, :] * vhalo[:, 6:6 + tt, :] + cw[1:2, :] * vhalo[:, 7:7 + tt, :]
          + cw[2:3, :] * vhalo[:, 8:8 + tt, :])
    ps_scr[:, 2048:2560] = yc.reshape(r, CW)
    new_halo = vhalo[:, tt + 6:tt + 8, :]
    vhalo[:, 6:8, :] = new_halo
    cvo_ref[...] = new_halo

    def conv_project(ci):
        rows = slice(ci * rc, (ci + 1) * rc)
        tc_ = ps_scr[rows, 1024:1536] * ps_scr[rows, 2048:2560] * _silu(ps_scr[rows, 2560:3072])
        ps_scr[rows, 1024:2048] = g_scr[rows, D:] * _dot(tc_.astype(BF16), wpc_ref[...])

    for ci in range(nchunk // 2):
        conv_project(ci)

    def scan_quarter(q):
        c_re = slice(q * 512, (q + 1) * 512)
        c_im = slice(S5N + q * 512, S5N + (q + 1) * 512)
        ar = jnp.broadcast_to(are_ref[:, c_re], (8, 512))
        ai = jnp.broadcast_to(aim_ref[:, c_re], (8, 512))
        for bs in range(bt // 8):
            r0 = bs * 8
            sr = sre_ref[r0:r0 + 8, c_re]
            si = sim_ref[r0:r0 + 8, c_re]
            for t in range(tt):
                row = t * bt + r0
                nr = ar * sr - ai * si + bu_scr[row:row + 8, c_re]
                ni = ar * si + ai * sr + bu_scr[row:row + 8, c_im]
                bu_scr[row:row + 8, c_re] = nr
                bu_scr[row:row + 8, c_im] = ni
                sr, si = nr, ni
            sre_ref[r0:r0 + 8, c_re] = sr
            sim_ref[r0:r0 + 8, c_re] = si

    for m in range(2):
        scan_quarter(2 * m)
        scan_quarter(2 * m + 1)
        sre = bu_scr[:, m * 1024:(m + 1) * 1024].astype(BF16)
        sim = bu_scr[:, S5N + m * 1024:S5N + (m + 1) * 1024].astype(BF16)
        ytb = _dot(sre, cre_ref[m]) - _dot(sim, cim_ref[m])
        for t in range(tt):
            for jj in range(2):
                yslab[2 * m + jj, t * pb:t * pb + bt, :] = ytb[t * bt:(t + 1) * bt, jj * 128:(jj + 1) * 128]
    for ci in range(nchunk // 2, nchunk):
        conv_project(ci)
    for b in range(bt):
        for ts in range(tt // 8):
            dst = b * tt + ts * 8
            for j in range(4):
                y_scr[dst:dst + 8, j * 128:(j + 1) * 128] = yslab[j, pl.ds(ts * 8 * pb + b, 8, stride=pb), :]

    nb = rc // tt if rc >= tt else 0
    dsk = dsk_ref[...]
    for ci in range(nchunk):
        rows = slice(ci * rc, (ci + 1) * rc)
        y = y_scr[rows, :] + dsk * ps_scr[rows, 0:512]
        gl = _gelu_tanh(y)
        glu = gl * _sigmoid(_dot(gl.astype(BF16), wglu_ref[...]))
        ts_ = glu * _silu(ps_scr[rows, 512:1024])
        ys = _dot(ts_.astype(BF16), wps_ref[...])
        out = g_scr[rows, :D] * ys + ps_scr[rows, 1024:2048]
        if nb:
            msc_ref[ci * nb:(ci + 1) * nb] = out.reshape(nb, tt, D)
        else:
            per = tt // rc
            msc_ref[ci // per, (ci % per) * rc:(ci % per + 1) * rc, :] = out


def _s5conv_call(layer, x, weights, state, bt, tt):
    b, t, _ = x.shape
    r = bt * tt
    has_state = state is not None
    wspec = lambda shape: pl.BlockSpec((None,) + shape, lambda bi, ti: (layer,) + (0,) * len(shape),
                                       pipeline_mode=pl.Buffered(1))
    tile = pl.BlockSpec((bt, tt, D), lambda bi, ti: (bi, ti, 0))
    in_specs = [tile, wspec((1, D)), wspec((D, 3072)), wspec((D, 2048)), wspec((1, S5N)), wspec((1, S5N)),
                wspec((2, 256, 1024)), wspec((2, 256, 1024)), wspec((2, 1024, 256)), wspec((2, 1024, 256)),
                wspec((1, S5W)), wspec((S5W, S5W)), wspec((S5W, D)), wspec((CK, CW)), wspec((CW, D))]
    args = [x] + list(weights)
    if has_state:
        in_specs += [pl.BlockSpec((None, bt, S5N), lambda bi, ti: (layer, bi, 0)),
                     pl.BlockSpec((None, bt, S5N), lambda bi, ti: (layer, bi, 0)),
                     pl.BlockSpec((None, bt, CK - 1, CW), lambda bi, ti: (layer, bi, 0, 0))]
        args += list(state)
    out_specs = [tile,
                 pl.BlockSpec((bt, S5N), lambda bi, ti: (bi, 0)),
                 pl.BlockSpec((bt, S5N), lambda bi, ti: (bi, 0)),
                 pl.BlockSpec((bt, CK - 1, CW), lambda bi, ti: (bi, 0, 0))]
    out_shape = [jax.ShapeDtypeStruct((b, t, D), F32), jax.ShapeDtypeStruct((b, S5N), F32),
                 jax.ShapeDtypeStruct((b, S5N), F32), jax.ShapeDtypeStruct((b, CK - 1, CW), F32)]
    scratch = [pltpu.VMEM((r, D), BF16),
               pltpu.VMEM((r, 3072), F32),
               pltpu.VMEM((r, 2 * D), F32),
               pltpu.VMEM((4, bt * _pitch(tt), 128), F32),
               pltpu.VMEM((r, S5W), F32),
               pltpu.VMEM((r, 2 * S5N), F32),
               pltpu.VMEM((4, tt * _pitch(bt), 128), F32),
               pltpu.VMEM((r, S5W), F32),
               pltpu.VMEM((bt, tt + 8, CW), F32)]
    return pl.pallas_call(
        functools.partial(_s5conv_kernel, has_state, bt, tt),
        grid=(b // bt, t // tt),
        in_specs=in_specs, out_specs=out_specs, out_shape=out_shape, scratch_shapes=scratch,
        compiler_params=pltpu.CompilerParams(dimension_semantics=("arbitrary", "arbitrary"),
                                             vmem_limit_bytes=VMEM_LIMIT),
        name="s5conv_state" if has_state else "s5conv",
    )(*args)


def _seg_scan(x, pos, seg, op, fill):
    s = 1
    while s < seg:
        x = op(x, jnp.where(pos >= s, pltpu.roll(x, s, 1), fill))
        s *= 2
    return x


def _seg_bcast_last(x, pos, seg):
    n = x.shape[1]
    s = 1
    while s < seg:
        x = jnp.where(pos + s <= seg - 1, pltpu.roll(x, n - s, 1), x)
        s *= 2
    return x


def _mlstm_kernel(has_state, has_alias, final, single_tile, bt, tt, nsub, *refs):
    refs = list(refs)
    x_ref, msc_ref, nw_ref = refs[:3]
    wm_refs = refs[3:8]
    wif_ref, bias_ref, wgm_ref, hnw_ref, wpm_ref, wout_ref, fnw_ref = refs[8:15]
    refs = refs[15:]
    if has_state:
        c0_ref, n0_ref, m0_ref = refs[:3]
        refs = refs[3:]
    if has_alias:
        refs = refs[1:]
    y_ref, c_ref, n_ref, m_ref = refs[:4]
    h_scr, p_scr, hm_scr, sg_scr, qkv_scr = refs[4:9]
    if nsub > 1:
        zt_scr, gd_scr = refs[9:]

    r = bt * tt
    ro = r * nsub
    rp = max(r, 128)
    step = pl.program_id(1)

    def init_state():
        if has_state:
            c_ref[...] = c0_ref[...]
            n_ref[...] = n0_ref[...]
            m_ref[...] = m0_ref[...]
        else:
            c_ref[...] = jnp.zeros_like(c_ref)
            n_ref[...] = jnp.zeros_like(n_ref)
            m_ref[...] = jnp.zeros_like(m_ref)

    def normalise():
        hh = _rmsnorm(x_ref[...].reshape(ro, D), nw_ref[...]).astype(BF16)
        h_scr[...] = hh
        return hh

    def project(hh, blocks=range(5)):
        for k in blocks:
            res = _dot(hh, wm_refs[k][...])
            if tt > SHORT_SEQ and k < 3:
                qkv_scr[:, k * D:(k + 1) * D] = (res * K_SCALE if k == 1 else res).astype(BF16)
            else:
                p_scr[:, k * D:(k + 1) * D] = res

    def merge_gate(piece):
        cols = slice(piece * (D // GATE_PIECES), (piece + 1) * (D // GATE_PIECES))
        sg_scr[:, cols] = _sigmoid(_dot(h_scr[...], wgm_ref[:, cols]))

    def oz_piece(piece):
        k, j = 3 + piece // 2, piece % 2
        w = D // 2
        p_scr[:, k * D + j * w:k * D + (j + 1) * w] = _dot(h_scr[...], wm_refs[k][:, j * w:(j + 1) * w])

    def epilogue_head(ci, rc):
        rows = slice(ci * rc, (ci + 1) * rc)
        hnw = hnw_ref[...]
        hmv = hm_scr[rows, :] * _sigmoid(p_scr[rows, 3 * D:4 * D])
        parts = []
        for hd in range(NH):
            hh = hmv[:, hd * DH:(hd + 1) * DH]
            mu = jnp.mean(hh, axis=-1, keepdims=True)
            dv = hh - mu
            var = jnp.mean(dv * dv, axis=-1, keepdims=True)
            parts.append(dv * lax.rsqrt(var + EPS) * hnw[:, hd * DH:(hd + 1) * DH])
        hn = jnp.concatenate(parts, axis=1)
        return (hn * _silu(p_scr[rows, 4 * D:5 * D])).astype(BF16)

    def epilogue_tail(ci, rc, tm):
        rows = slice(ci * rc, (ci + 1) * rc)
        ym = _dot(tm, wpm_ref[...])
        nb = rc // tt if rc >= tt else 0
        if nb:
            msc = msc_ref[ci * nb:(ci + 1) * nb].reshape(rc, D)
            xr = x_ref[ci * nb:(ci + 1) * nb].reshape(rc, D)
        else:
            per = tt // rc
            msc = msc_ref[ci // per, (ci % per) * rc:(ci % per + 1) * rc, :]
            xr = x_ref[ci // per, (ci % per) * rc:(ci % per + 1) * rc, :]
        merged = msc + sg_scr[rows, :] * ym
        y = xr + _dot(merged.astype(BF16), wout_ref[...])
        if final:
            y = _rmsnorm(y, fnw_ref[...])
        if nb:
            y_ref[ci * nb:(ci + 1) * nb] = y.reshape(nb, tt, D)
        else:
            y_ref[ci // per, (ci % per) * rc:(ci % per + 1) * rc, :] = y

    deferred = nsub == 1 and tt > SHORT_SEQ
    fillers = [f for p in range(4) for f in (functools.partial(oz_piece, p), functools.partial(merge_gate, p))
               ] if deferred else []

    def gate_chain(hrows, mprev):
        n = hrows.shape[0]
        gt = _dot_nt(wif_ref[...], hrows) + bias_ref[...]
        pos = jnp.bitwise_and(lax.broadcasted_iota(jnp.int32, (8, n), 1), tt - 1)
        bcum = _seg_scan(_log_sigmoid(gt[8:16, :]), pos, tt, jnp.add, 0.0)
        g = gt[0:8, :] - bcum
        mrun = jnp.maximum(mprev, _seg_scan(g, pos, tt, jnp.maximum, NEG))
        mlast = _seg_bcast_last(mrun, pos, tt)
        winter = jnp.exp(mprev - mrun)
        efloor = jnp.exp(-(bcum + mrun))
        wrow = jnp.exp(g - mlast)
        decay = jnp.exp(mprev - mlast)
        mnew = _seg_bcast_last(bcum, pos, tt) + mlast
        zt = jnp.concatenate([mrun, winter, efloor, wrow], axis=0).T
        return g, decay, zt, mnew

    if not single_tile:
        pl.when(step == 0)(init_state)
        c_prev, n_prev, m_start = c_ref, n_ref, (lambda: m_ref[...])
    elif has_state:
        c_prev, n_prev, m_start = c0_ref, n0_ref, (lambda: m0_ref[...])
    else:
        c_ref[...] = jnp.zeros_like(c_ref)
        n_ref[...] = jnp.zeros_like(n_ref)
        c_prev, n_prev, m_start = c_ref, n_ref, (lambda: jnp.zeros(m_ref.shape, F32))

    if nsub == 1:
        base = 0
        h = normalise()
        hp = h if rp == r else jnp.concatenate([h, jnp.zeros((rp - r, D), BF16)], axis=0)
        g, decay, zt, mnew = gate_chain(hp, m_start())
        m_ref[...] = mnew
        project(h, range(3) if deferred else range(5))
    else:
        base = pl.multiple_of(step * r, r)

        @pl.when(step == 0)
        def _block_prologue():
            hh = normalise()
            ga, da, za, mnew_all = gate_chain(hh, m_start())
            project(hh)
            m_ref[...] = mnew_all
            zt_scr[:, 0:32] = za
            for s in range(nsub):
                gd_scr[16 * s:16 * s + 8, 0:r] = ga[:, s * r:(s + 1) * r]
                gd_scr[16 * s + 8:16 * s + 16, 0:r] = da[:, s * r:(s + 1) * r]
            for piece in range(GATE_PIECES):
                merge_gate(piece)

        zt = zt_scr[pl.ds(base, r), 0:32]
        gd = gd_scr[pl.ds(pl.multiple_of(step * 16, 16), 16), :]
        g = gd[0:8, 0:r]
        decay = gd[8:16, 0:r]

    if tt <= SHORT_SEQ:
        nst = NH * r
        prow = slice(0, r) if nsub == 1 else pl.ds(base, r)
        stack = lambda off: jnp.concatenate(
            [p_scr[prow, off + hd * DH:off + (hd + 1) * DH] for hd in range(NH)], axis=0)
        qs = stack(0)
        ks = stack(D) * K_SCALE
        vs = stack(2 * D)
        col = lambda c0: jnp.concatenate([zt[0:r, c0 + hd:c0 + hd + 1] for hd in range(NH)], axis=0)
        mcol, wi, ef, wc = col(0), col(8), col(16), col(24)
        grow = jnp.concatenate([g[hd:hd + 1, 0:r] for hd in range(NH)], axis=1)
        row_i = lax.broadcasted_iota(jnp.int32, (nst, nst), 0)
        col_i = lax.broadcasted_iota(jnp.int32, (nst, nst), 1)
        shift = tt.bit_length() - 1
        same = lax.shift_right_logical(row_i, shift) == lax.shift_right_logical(col_i, shift)
        arg = jnp.where(col_i <= row_i, grow - mcol, NEG)
        dm = jnp.exp(jnp.where(same, arg, NEG))
        sc = _dot_nt(qs.astype(BF16), ks.astype(BF16)) * dm
        intra = _dot(sc.astype(BF16), vs.astype(BF16))
        pairs = [(hd, b) for hd in range(NH) for b in range(bt)]
        blk = lambda hd, b: slice(hd * r + b * tt, hd * r + (b + 1) * tt)
        inter = jnp.concatenate([_dot(qs[blk(hd, b)].astype(BF16), c_prev[b, hd].astype(BF16))
                                 for hd, b in pairs], axis=0)
        nfull = jnp.concatenate([jnp.broadcast_to(n_prev[b, hd:hd + 1, :], (tt, DH)) for hd, b in pairs], axis=0)
        num = wi * inter + intra
        den = wi * jnp.sum(qs * nfull, axis=-1, keepdims=True) + jnp.sum(sc, axis=-1, keepdims=True)
        hh = num * (1.0 / jnp.maximum(jnp.abs(den), ef))
        for hd in range(NH):
            hm_scr[prow, hd * DH:(hd + 1) * DH] = hh[hd * r:(hd + 1) * r]
        wk = ks * wc
        for hd, b in pairs:
            dec = decay[hd:hd + 1, b * tt:b * tt + 1]
            upd = _dot_tn(wk[blk(hd, b)].astype(BF16), vs[blk(hd, b)].astype(BF16))
            c_ref[b, hd] = dec * c_prev[b, hd] + upd
            n_ref[b, hd:hd + 1, :] = (dec * n_prev[b, hd:hd + 1, :]
                                      + jnp.sum(wk[blk(hd, b)], axis=0, keepdims=True))
        if nsub == 1:
            for piece in range(GATE_PIECES):
                merge_gate(piece)

    row_i = lax.broadcasted_iota(jnp.int32, (tt, tt), 0)
    col_i = lax.broadcasted_iota(jnp.int32, (tt, tt), 1)
    causal = row_i >= col_i

    for b in range(bt if tt > SHORT_SEQ else 0):
        rows = slice(b * tt, (b + 1) * tt)
        prow = rows if nsub == 1 else pl.ds(base + b * tt, tt)
        for hd in range(NH):
            cs = slice(hd * DH, (hd + 1) * DH)
            qb = qkv_scr[prow, cs]
            kb = qkv_scr[prow, D + hd * DH:D + (hd + 1) * DH]
            vb = qkv_scr[prow, 2 * D + hd * DH:2 * D + (hd + 1) * DH]
            qf = qb.astype(F32)
            kf = kb.astype(F32)
            mcol = zt[rows, hd:hd + 1]
            wi = zt[rows, 8 + hd:9 + hd]
            ef = zt[rows, 16 + hd:17 + hd]
            wc = zt[rows, 24 + hd:25 + hd]
            grow = g[hd:hd + 1, b * tt:(b + 1) * tt]
            dm = jnp.exp(jnp.where(causal, grow - mcol, NEG))
            s_raw = _dot_nt(qb, kb)
            cf = c_prev[b, hd]
            nrow = n_prev[b, hd:hd + 1, :]
            inter = _dot(qb, cf.astype(BF16))
            dec = decay[hd:hd + 1, b * tt:b * tt + 1]
            wk = kf * wc
            c_ref[b, hd] = dec * cf + _dot_tn(wk.astype(BF16), vb)
            n_ref[b, hd:hd + 1, :] = dec * nrow + jnp.sum(wk, axis=0, keepdims=True)
            sc = s_raw * dm
            num = wi * inter + _dot(sc.astype(BF16), vb)
            den = wi * jnp.sum(qf * nrow, axis=-1, keepdims=True) + jnp.sum(sc, axis=-1, keepdims=True)
            hm_scr[prow, cs] = num * (1.0 / jnp.maximum(jnp.abs(den), ef))
            done = b * NH + hd + 1
            for f in fillers[(done - 1) * len(fillers) // (bt * NH):done * len(fillers) // (bt * NH)]:
                f()

    def epilogue():
        rc = min(MLSTM_ROW_CHUNK, ro)
        for ci in range(ro // rc):
            epilogue_tail(ci, rc, epilogue_head(ci, rc))

    if nsub == 1:
        epilogue()
    else:
        pl.when(step == nsub - 1)(epilogue)


def _mlstm_call(layer, depth, x, msc, weights, state, c_stack, bt, tt):
    b, t, _ = x.shape
    r = bt * tt
    rp = max(r, 128)
    nbt = b // bt
    has_state = state is not None
    has_alias = c_stack is not None
    final = layer == depth - 1
    nsub = max(1, min(nbt, PROJ_ROWS // r)) if t == tt else 1
    assert nbt % nsub == 0
    ro = r * nsub
    rpo = max(ro, 128)
    if nsub == 1:
        grid = (nbt, t // tt)
        tile_idx = lambda bi, si: (bi, si, 0)
        sub_idx = lambda bi, si: bi
    else:
        grid = (nbt // nsub, nsub)
        tile_idx = lambda bi, si: (bi, 0, 0)
        sub_idx = lambda bi, si: bi * nsub + si
    wspec = lambda shape: pl.BlockSpec((None,) + shape, lambda bi, si: (layer,) + (0,) * len(shape),
                                       pipeline_mode=pl.Buffered(1))
    tile = pl.BlockSpec((bt * nsub, tt, D), tile_idx)
    wcol = lambda k: pl.BlockSpec((None, D, D), lambda bi, si: (layer, 0, k), pipeline_mode=pl.Buffered(1))
    nw, w_main, w_if, bias, w_gates, hnw, wpm, wout, fnw = weights
    in_specs = ([tile, tile, wspec((1, D))] + [wcol(OFF_M // D + k) for k in range(5)]
                + [wspec((16, D)), wspec((16, 1)), wcol(2), wspec((1, D)), wspec((D, D)), wspec((D, D)),
                   pl.BlockSpec((1, D), lambda bi, si: (0, 0), pipeline_mode=pl.Buffered(1))])
    args = [x, msc, nw] + [w_main] * 5 + [w_if, bias, w_gates, hnw, wpm, wout, fnw]
    if has_state:
        in_specs += [pl.BlockSpec((None, bt, NH, DH, DH), lambda bi, si: (layer, sub_idx(bi, si), 0, 0, 0)),
                     pl.BlockSpec((None, bt, NH, DH), lambda bi, si: (layer, sub_idx(bi, si), 0, 0)),
                     pl.BlockSpec((None, None, 8, rpo), lambda bi, si: (layer, bi, 0, 0))]
        c0, n0, m0 = state
        args += [c0, n0, _expand_m(m0, bt * nsub, tt)]
    aliases = {}
    if has_alias:
        aliases = {len(args): 1}
        in_specs.append(pl.BlockSpec(memory_space=pl.ANY))
        args.append(c_stack)
    out_specs = [tile,
                 pl.BlockSpec((None, bt, NH, DH, DH), lambda bi, si: (layer, sub_idx(bi, si), 0, 0, 0)),
                 pl.BlockSpec((bt, NH, DH), lambda bi, si: (sub_idx(bi, si), 0, 0)),
                 pl.BlockSpec((None, 8, rpo), lambda bi, si: (bi, 0, 0))]
    out_shape = [jax.ShapeDtypeStruct((b, t, D), F32), jax.ShapeDtypeStruct((depth, b, NH, DH, DH), F32),
                 jax.ShapeDtypeStruct((b, NH, DH), F32), jax.ShapeDtypeStruct((nbt // nsub, 8, rpo), F32)]
    scratch = [pltpu.VMEM((ro, D), BF16),
               pltpu.VMEM((ro, 5 * D), F32),
               pltpu.VMEM((ro, D), F32),
               pltpu.VMEM((ro, D), F32),
               pltpu.VMEM((ro if tt > SHORT_SEQ else 16, 3 * D), BF16)]
    if nsub > 1:
        scratch += [pltpu.VMEM((ro, 128), F32),
                    pltpu.VMEM((16 * nsub, 128), F32)]
    y, c_new, n_new, m_rows = pl.pallas_call(
        functools.partial(_mlstm_kernel, has_state, has_alias, final, t == tt, bt, tt, nsub),
        grid=grid,
        in_specs=in_specs, out_specs=out_specs, out_shape=out_shape, scratch_shapes=scratch,
        input_output_aliases=aliases,
        compiler_params=pltpu.CompilerParams(dimension_semantics=("arbitrary", "arbitrary"),
                                             vmem_limit_bytes=VMEM_LIMIT),
        name="mlstm_state" if has_state else "mlstm",
    )(*args)
    return y, c_new, n_new, _collapse_m(m_rows, bt * nsub, tt)


def _expand_m(m, bt, tt):
    depth, b, _ = m.shape
    r = bt * tt
    rp = max(r, 128)
    v = jnp.repeat(m.reshape(depth, b // bt, bt, NH).transpose(0, 1, 3, 2), tt, axis=-1)
    return jnp.pad(v, ((0, 0), (0, 0), (0, 8 - NH), (0, rp - r)))


def _collapse_m(mrows, bt, tt):
    nbt = mrows.shape[0]
    return mrows[:, :NH, 0:bt * tt:tt].transpose(0, 2, 1).reshape(nbt * bt, NH)


def kernel(x_prompt, x_sample, state_ssm_re, state_ssm_im, state_conv, state_mlstm_c, state_mlstm_n, state_mlstm_m, norm_w, w_in, i_bias, f_bias, lam_re, lam_im, log_dt, b_re, b_im, c_re, c_im, d_skip, w_glu, w_proj_s, conv_w, w_proj_c, mlstm_norm_w, w_proj_m, w_out, final_norm_w):
    depth = norm_w.shape[0]
    bp, tp, _ = x_prompt.shape
    bs, ts, _ = x_sample.shape
    assert tp % PROMPT_CHUNK == 0 and ts % 8 == 0 and ts & (ts - 1) == 0 and bp % 8 == 0 and bs % 64 == 0

    a_re, a_im, bb_re, bb_im, cp_re, cp_im = _s5_discretise(lam_re, lam_im, log_dt, b_re, b_im, c_re, c_im)
    row = lambda a: a.reshape(depth, 1, a.shape[-1])
    nw = row(norm_w)
    w_t = jnp.swapaxes(w_in, 1, 2)
    w_main = _cast_columns(w_t, 0, OFF_IF // D)
    w_gates = _cast_columns(w_t, OFF_G, 3)
    gate_rows = lambda a: jnp.pad(a.reshape(depth, 2, NH, -1), ((0, 0), (0, 0), (0, 8 - NH), (0, 0))).reshape(depth, 16, -1)
    w_if = gate_rows(w_t[:, OFF_IF:OFF_G, :]).astype(BF16)
    bias = gate_rows(jnp.concatenate([i_bias, f_bias], axis=-1)[:, :, None])
    wa = [nw, w_main, w_gates, a_re, a_im, bb_re, bb_im, cp_re, cp_im, row(d_skip),
          w_glu.astype(BF16), w_proj_s.astype(BF16), conv_w, w_proj_c.astype(BF16)]
    wb = [nw, w_main, w_if, bias, w_gates, row(mlstm_norm_w), w_proj_m.astype(BF16), w_out.astype(BF16),
          final_norm_w.reshape(1, D)]

    sb_a, sb_b = 64, 4
    pb_a, pb_b = 8, 2
    st_a = (state_ssm_re.reshape(depth, bs, S5N), state_ssm_im.reshape(depth, bs, S5N), state_conv)
    st_b = (state_mlstm_c, state_mlstm_n, state_mlstm_m)

    yp, ys = x_prompt, x_sample
    cp = cs = None
    outs_p = [[] for _ in range(5)]
    outs_s = [[] for _ in range(5)]
    for l in range(depth):
        msc, sre, sim, cv = _s5conv_call(l, yp, wa, None, pb_a, S5_TILE_T)
        yp, cp, nn, mm = _mlstm_call(l, depth, yp, msc, wb, None, cp, pb_b, PROMPT_CHUNK)
        for i, v in enumerate((sre.reshape(bp, S5G, S5P), sim.reshape(bp, S5G, S5P), cv, nn, mm)):
            outs_p[i].append(v)
        msc, sre, sim, cv = _s5conv_call(l, ys, wa, st_a, sb_a, ts)
        ys, cs, nn, mm = _mlstm_call(l, depth, ys, msc, wb, st_b, cs, sb_b, ts)
        for i, v in enumerate((sre.reshape(bs, S5G, S5P), sim.reshape(bs, S5G, S5P), cv, nn, mm)):
            outs_s[i].append(v)
    sp = [jnp.stack(o) for o in outs_p]
    ss = [jnp.stack(o) for o in outs_s]
    return (yp, ys, sp[0], ss[0], sp[1], ss[1], sp[2], ss[2], cp, cs, sp[3], ss[3], sp[4], ss[4])
```

```python
import functools
import math

import jax
import jax.numpy as jnp
from jax import lax
from jax.experimental import pallas as pl
from jax.experimental.pallas import tpu as pltpu

D = 1024
S5W = 512
S5G = 32
S5P = 64
S5C = 16
S5N = S5G * S5P
CW = 512
CK = 3
NH = 4
DH = 256
PROMPT_CHUNK = 256
S5_TILE_T = 64
SHORT_SEQ = 32
EPS = 1e-6
NEG = -1e30
K_SCALE = DH ** -0.5

OFF_S = 0
OFF_M = 3072
OFF_IF = 8192
OFF_G = 8200
IN_W = 11272

F32 = jnp.float32
BF16 = jnp.bfloat16
VMEM_LIMIT = 58 * 1024 * 1024
ROW_CHUNK = 256
GATE_PIECES = 4
MLSTM_ROW_CHUNK = 512
PROJ_ROWS = 256


def _sigmoid(x):
    return 1.0 / (1.0 + jnp.exp(-x))


def _silu(x):
    return x * _sigmoid(x)


def _gelu_tanh(x):
    return 0.5 * x * (1.0 + jnp.tanh(math.sqrt(2.0 / math.pi) * (x + 0.044715 * (x * x * x))))


def _log_sigmoid(x):
    return jnp.minimum(x, 0.0) - jnp.log1p(jnp.exp(-jnp.abs(x)))


def _rmsnorm(x, w):
    return x * lax.rsqrt(jnp.mean(x * x, axis=-1, keepdims=True) + EPS) * w


def _dot(a, b):
    return jnp.dot(a, b, preferred_element_type=F32)


def _dot_nt(a, b):
    return lax.dot_general(a, b, (((1,), (1,)), ((), ())), preferred_element_type=F32)


def _dot_tn(a, b):
    return lax.dot_general(a, b, (((0,), (0,)), ((), ())), preferred_element_type=F32)


def _pitch(n):
    p = n + 8
    return p if (p // 8) % 2 == 1 else p + 8


def _s5_disc_kernel(lre_ref, lim_ref, ldt_ref, bre_ref, bim_ref, cre_ref, cim_ref,
                    are_ref, aim_ref, bbre_ref, bbim_ref, cpre_ref, cpim_ref):
    lr = lre_ref[...]
    li = lim_ref[...]
    dt = jnp.exp(ldt_ref[...])
    ea = jnp.exp(lr * dt)
    ar = ea * jnp.cos(li * dt)
    ai = ea * jnp.sin(li * dt)
    are_ref[...] = ar
    aim_ref[...] = ai
    nr = ar - 1.0
    inv = 1.0 / (lr * lr + li * li)
    cr = (nr * lr + ai * li) * inv
    ci = (ai * lr - nr * li) * inv
    half = S5N // 2
    shr = lax.shift_right_logical
    bdiag = (shr(lax.broadcasted_iota(jnp.int32, (256, half), 0), 4)
             == shr(lax.broadcasted_iota(jnp.int32, (256, half), 1), 6))
    cdiag = (shr(lax.broadcasted_iota(jnp.int32, (half, 256), 0), 6)
             == shr(lax.broadcasted_iota(jnp.int32, (half, 256), 1), 4))
    rep = lambda x: jnp.concatenate([x] * 16, axis=0)
    for h in range(2):
        crh = cr[:, h * half:(h + 1) * half]
        cih = ci[:, h * half:(h + 1) * half]
        br = bre_ref[h]
        bi = bim_ref[h]
        bbre_ref[h] = jnp.where(bdiag, rep(crh * br - cih * bi), 0.0).astype(BF16)
        bbim_ref[h] = jnp.where(bdiag, rep(crh * bi + cih * br), 0.0).astype(BF16)
        cpre_ref[h] = jnp.where(cdiag, rep(cre_ref[h]), 0.0).astype(BF16)
        cpim_ref[h] = jnp.where(cdiag, rep(cim_ref[h]), 0.0).astype(BF16)


def _s5_discretise(lam_re, lam_im, log_dt, b_re, b_im, c_re, c_im):
    depth = lam_re.shape[0]

    def rows_b(b):
        return jnp.swapaxes(b.reshape(depth, 2, 16 * S5P, S5C), 2, 3)

    def rows_c(c):
        return jnp.swapaxes(c.reshape(depth, 2, 16 * S5C, S5P), 2, 3)

    row = lambda a: a.reshape(depth, 1, S5N)
    ldt = jnp.broadcast_to(log_dt[:, :, None], (depth, S5G, S5P))
    vec = pl.BlockSpec((None, 1, S5N), lambda l: (l, 0, 0))
    bmat = pl.BlockSpec((None, 2, 256, 1024), lambda l: (l, 0, 0, 0))
    cmat = pl.BlockSpec((None, 2, 1024, 256), lambda l: (l, 0, 0, 0))
    brow = pl.BlockSpec((None, 2, S5C, 1024), lambda l: (l, 0, 0, 0))
    crow = pl.BlockSpec((None, 2, S5P, 256), lambda l: (l, 0, 0, 0))
    return pl.pallas_call(
        _s5_disc_kernel,
        grid=(depth,),
        in_specs=[vec, vec, vec, brow, brow, crow, crow],
        out_specs=[vec, vec, bmat, bmat, cmat, cmat],
        out_shape=[jax.ShapeDtypeStruct((depth, 1, S5N), F32), jax.ShapeDtypeStruct((depth, 1, S5N), F32),
                   jax.ShapeDtypeStruct((depth, 2, 256, 1024), BF16),
                   jax.ShapeDtypeStruct((depth, 2, 256, 1024), BF16),
                   jax.ShapeDtypeStruct((depth, 2, 1024, 256), BF16),
                   jax.ShapeDtypeStruct((depth, 2, 1024, 256), BF16)],
        name="s5_discretise",
    )(row(lam_re), row(lam_im), row(ldt), rows_b(b_re), rows_b(b_im), rows_c(c_re), rows_c(c_im))


def _cast_kernel(shift, *refs):
    if shift:
        wa_ref, wb_ref, o_ref = refs
        w = jnp.concatenate([wa_ref[shift:, :], wb_ref[...]], axis=0)
    else:
        w_ref, o_ref = refs
        w = w_ref[...]
    o_ref[...] = w.T.astype(BF16)


def _cast_columns(w_t, first_col, n_blocks):
    depth = w_t.shape[0]
    k0, shift = divmod(first_col, D)
    assert shift % 8 == 0
    in_specs = [pl.BlockSpec((None, D, D), lambda l, k: (l, k0 + k, 0))]
    args = [w_t]
    if shift:
        assert D % shift == 0
        in_specs.append(pl.BlockSpec((None, shift, D), lambda l, k: (l, (k0 + k + 1) * (D // shift), 0)))
        args.append(w_t)
    return pl.pallas_call(
        functools.partial(_cast_kernel, shift),
        grid=(depth, n_blocks),
        in_specs=in_specs,
        out_specs=pl.BlockSpec((None, D, D), lambda l, k: (l, 0, k)),
        out_shape=jax.ShapeDtypeStruct((depth, D, n_blocks * D), BF16),
        name="cast_weights",
    )(*args)


def _s5conv_kernel(has_state, bt, tt, *refs):
    refs = list(refs)
    (x_ref, nw_ref, ws_ref, wg_ref, are_ref, aim_ref, bbre_ref, bbim_ref, cre_ref, cim_ref,
     dsk_ref, wglu_ref, wps_ref, cw_ref, wpc_ref) = refs[:15]
    refs = refs[15:]
    if has_state:
        s0re_ref, s0im_ref, cv0_ref = refs[:3]
        refs = refs[3:]
    msc_ref, sre_ref, sim_ref, cvo_ref = refs[:4]
    h_scr, ps_scr, g_scr, uslab, utb, bu_scr, yslab, y_scr, vhalo = refs[4:]

    r = bt * tt
    pt = _pitch(tt)
    pb = _pitch(bt)
    rc = min(ROW_CHUNK, r)
    nchunk = r // rc
    ti = pl.program_id(1)

    def merge_gates(ci):
        rows = slice(ci * rc, (ci + 1) * rc)
        g_scr[rows, :] = _sigmoid(_dot(h_scr[rows, :], wg_ref[...]))

    @pl.when(ti == 0)
    def _init():
        if has_state:
            sre_ref[...] = s0re_ref[...]
            sim_ref[...] = s0im_ref[...]
            vhalo[:, 6:8, :] = cv0_ref[...]
        else:
            sre_ref[...] = jnp.zeros_like(sre_ref)
            sim_ref[...] = jnp.zeros_like(sim_ref)
            vhalo[:, 6:8, :] = jnp.zeros((bt, 2, CW), F32)

    x = x_ref[...].reshape(r, D)
    h = _rmsnorm(x, nw_ref[...]).astype(BF16)
    h_scr[...] = h
    ps_scr[...] = _dot(h, ws_ref[...])


    for b in range(bt):
        for j in range(4):
            uslab[j, b * pt:b * pt + tt, :] = ps_scr[b * tt:(b + 1) * tt, j * 128:(j + 1) * 128]
    for t in range(tt):
        for bs in range(bt // 8):
            dst = t * bt + bs * 8
            for j in range(4):
                utb[dst:dst + 8, j * 128:(j + 1) * 128] = uslab[j, pl.ds(bs * 8 * pt + t, 8, stride=pt), :]
    for hf in range(2):
        uh = utb[:, hf * 256:(hf + 1) * 256].astype(BF16)
        bu_scr[:, hf * 1024:(hf + 1) * 1024] = _dot(uh, bbre_ref[hf])
        bu_scr[:, S5N + hf * 1024:S5N + (hf + 1) * 1024] = _dot(uh, bbim_ref[hf])
        for ci in range(hf * nchunk // 2, (hf + 1) * nchunk // 2):
            merge_gates(ci)

    v = ps_scr[:, 1536:2048] * ps_scr[:, 2048:2560]
    vhalo[:, 8:8 + tt, :] = v.reshape(bt, tt, CW)
    cw = cw_ref[...]
    yc = (cw[0:1, :] * vhalo[:, 6:6 + tt, :] + cw[1:2, :] * vhalo[:, 7:7 + tt, :]
          + cw[2:3, :] * vhalo[:, 8:8 + tt, :])
    ps_scr[:, 2048:2560] = yc.reshape(r, CW)
    new_halo = vhalo[:, tt + 6:tt + 8, :]
    vhalo[:, 6:8, :] = new_halo
    cvo_ref[...] = new_halo

    def conv_project(ci):
        rows = slice(ci * rc, (ci + 1) * rc)
        tc_ = ps_scr[rows, 1024:1536] * ps_scr[rows, 2048:2560] * _silu(ps_scr[rows, 2560:3072])
        ps_scr[rows, 1024:2048] = g_scr[rows, D:] * _dot(tc_.astype(BF16), wpc_ref[...])

    for ci in range(nchunk // 2):
        conv_project(ci)

    def scan_quarter(q):
        c_re = slice(q * 512, (q + 1) * 512)
        c_im = slice(S5N + q * 512, S5N + (q + 1) * 512)
        ar = jnp.broadcast_to(are_ref[:, c_re], (8, 512))
        ai = jnp.broadcast_to(aim_ref[:, c_re], (8, 512))
        for bs in range(bt // 8):
            r0 = bs * 8
            sr = sre_ref[r0:r0 + 8, c_re]
            si = sim_ref[r0:r0 + 8, c_re]
            for t in range(tt):
                row = t * bt + r0
                nr = ar * sr - ai * si + bu_scr[row:row + 8, c_re]
                ni = ar * si + ai * sr + bu_scr[row:row + 8, c_im]
                bu_scr[row:row + 8, c_re] = nr
                bu_scr[row:row + 8, c_im] = ni
                sr, si = nr, ni
            sre_ref[r0:r0 + 8, c_re] = sr
            sim_ref[r0:r0 + 8, c_re] = si

    for m in range(2):
        scan_quarter(2 * m)
        scan_quarter(2 * m + 1)
        sre = bu_scr[:, m * 1024:(m + 1) * 1024].astype(BF16)
        sim = bu_scr[:, S5N + m * 1024:S5N + (m + 1) * 1024].astype(BF16)
        ytb = _dot(sre, cre_ref[m]) - _dot(sim, cim_ref[m])
        for t in range(tt):
            for jj in range(2):
                yslab[2 * m + jj, t * pb:t * pb + bt, :] = ytb[t * bt:(t + 1) * bt, jj * 128:(jj + 1) * 128]
    for ci in range(nchunk // 2, nchunk):
        conv_project(ci)
    for b in range(bt):
        for ts in range(tt // 8):
            dst = b * tt + ts * 8
            for j in range(4):
                y_scr[dst:dst + 8, j * 128:(j + 1) * 128] = yslab[j, pl.ds(ts * 8 * pb + b, 8, stride=pb), :]

    nb = rc // tt if rc >= tt else 0
    dsk = dsk_ref[...]
    gls, gates = [], []
    for ci in range(nchunk):
        rows = slice(ci * rc, (ci + 1) * rc)
        gl = _gelu_tanh(y_scr[rows, :] + dsk * ps_scr[rows, 0:512])
        gls.append(gl)
        gates.append(_dot(gl.astype(BF16), wglu_ref[...]))
    for ci in range(nchunk):
        rows = slice(ci * rc, (ci + 1) * rc)
        glu = gls[ci] * _sigmoid(gates[ci])
        ts_ = glu * _silu(ps_scr[rows, 512:1024])
        ys = _dot(ts_.astype(BF16), wps_ref[...])
        out = g_scr[rows, :D] * ys + ps_scr[rows, 1024:2048]
        if nb:
            msc_ref[ci * nb:(ci + 1) * nb] = out.reshape(nb, tt, D)
        else:
            per = tt // rc
            msc_ref[ci // per, (ci % per) * rc:(ci % per + 1) * rc, :] = out


def _s5conv_call(layer, x, weights, state, bt, tt):
    b, t, _ = x.shape
    r = bt * tt
    has_state = state is not None
    wspec = lambda shape: pl.BlockSpec((None,) + shape, lambda bi, ti: (layer,) + (0,) * len(shape),
                                       pipeline_mode=pl.Buffered(1))
    tile = pl.BlockSpec((bt, tt, D), lambda bi, ti: (bi, ti, 0))
    in_specs = [tile, wspec((1, D)), wspec((D, 3072)), wspec((D, 2048)), wspec((1, S5N)), wspec((1, S5N)),
                wspec((2, 256, 1024)), wspec((2, 256, 1024)), wspec((2, 1024, 256)), wspec((2, 1024, 256)),
                wspec((1, S5W)), wspec((S5W, S5W)), wspec((S5W, D)), wspec((CK, CW)), wspec((CW, D))]
    args = [x] + list(weights)
    if has_state:
        in_specs += [pl.BlockSpec((None, bt, S5N), lambda bi, ti: (layer, bi, 0)),
                     pl.BlockSpec((None, bt, S5N), lambda bi, ti: (layer, bi, 0)),
                     pl.BlockSpec((None, bt, CK - 1, CW), lambda bi, ti: (layer, bi, 0, 0))]
        args += list(state)
    out_specs = [tile,
                 pl.BlockSpec((bt, S5N), lambda bi, ti: (bi, 0)),
                 pl.BlockSpec((bt, S5N), lambda bi, ti: (bi, 0)),
                 pl.BlockSpec((bt, CK - 1, CW), lambda bi, ti: (bi, 0, 0))]
    out_shape = [jax.ShapeDtypeStruct((b, t, D), F32), jax.ShapeDtypeStruct((b, S5N), F32),
                 jax.ShapeDtypeStruct((b, S5N), F32), jax.ShapeDtypeStruct((b, CK - 1, CW), F32)]
    scratch = [pltpu.VMEM((r, D), BF16),
               pltpu.VMEM((r, 3072), F32),
               pltpu.VMEM((r, 2 * D), F32),
               pltpu.VMEM((4, bt * _pitch(tt), 128), F32),
               pltpu.VMEM((r, S5W), F32),
               pltpu.VMEM((r, 2 * S5N), F32),
               pltpu.VMEM((4, tt * _pitch(bt), 128), F32),
               pltpu.VMEM((r, S5W), F32),
               pltpu.VMEM((bt, tt + 8, CW), F32)]
    return pl.pallas_call(
        functools.partial(_s5conv_kernel, has_state, bt, tt),
        grid=(b // bt, t // tt),
        in_specs=in_specs, out_specs=out_specs, out_shape=out_shape, scratch_shapes=scratch,
        compiler_params=pltpu.CompilerParams(dimension_semantics=("arbitrary", "arbitrary"),
                                             vmem_limit_bytes=VMEM_LIMIT),
        name="s5conv_state" if has_state else "s5conv",
    )(*args)


def _seg_scan(x, pos, seg, op, fill):
    s = 1
    while s < seg:
        x = op(x, jnp.where(pos >= s, pltpu.roll(x, s, 1), fill))
        s *= 2
    return x


def _seg_bcast_last(x, pos, seg):
    n = x.shape[1]
    s = 1
    while s < seg:
        x = jnp.where(pos + s <= seg - 1, pltpu.roll(x, n - s, 1), x)
        s *= 2
    return x


def _mlstm_kernel(has_state, has_alias, final, single_tile, bt, tt, nsub, *refs):
    refs = list(refs)
    x_ref, msc_ref, nw_ref = refs[:3]
    wm_refs = refs[3:8]
    wif_ref, bias_ref, wgm_ref, hnw_ref, wpm_ref, wout_ref, fnw_ref = refs[8:15]
    refs = refs[15:]
    if has_state:
        c0_ref, n0_ref, m0_ref = refs[:3]
        refs = refs[3:]
    if has_alias:
        refs = refs[1:]
    y_ref, c_ref, n_ref, m_ref = refs[:4]
    h_scr, p_scr, hm_scr, sg_scr, qkv_scr = refs[4:9]
    if nsub > 1:
        zt_scr, gd_scr = refs[9:]

    r = bt * tt
    ro = r * nsub
    rp = max(r, 128)
    step = pl.program_id(1)

    def init_state():
        if has_state:
            c_ref[...] = c0_ref[...]
            n_ref[...] = n0_ref[...]
            m_ref[...] = m0_ref[...]
        else:
            c_ref[...] = jnp.zeros_like(c_ref)
            n_ref[...] = jnp.zeros_like(n_ref)
            m_ref[...] = jnp.zeros_like(m_ref)

    def normalise():
        hh = _rmsnorm(x_ref[...].reshape(ro, D), nw_ref[...]).astype(BF16)
        h_scr[...] = hh
        return hh

    def project(hh, blocks=range(5)):
        for k in blocks:
            res = _dot(hh, wm_refs[k][...])
            if tt > SHORT_SEQ and k < 3:
                qkv_scr[:, k * D:(k + 1) * D] = (res * K_SCALE if k == 1 else res).astype(BF16)
            else:
                p_scr[:, k * D:(k + 1) * D] = res

    def merge_gate(piece):
        cols = slice(piece * (D // GATE_PIECES), (piece + 1) * (D // GATE_PIECES))
        sg_scr[:, cols] = _sigmoid(_dot(h_scr[...], wgm_ref[:, cols]))

    def oz_piece(piece):
        k, j = 3 + piece // 2, piece % 2
        w = D // 2
        p_scr[:, k * D + j * w:k * D + (j + 1) * w] = _dot(h_scr[...], wm_refs[k][:, j * w:(j + 1) * w])

    def epilogue_head(ci, rc):
        rows = slice(ci * rc, (ci + 1) * rc)
        hnw = hnw_ref[...]
        hmv = hm_scr[rows, :] * _sigmoid(p_scr[rows, 3 * D:4 * D])
        parts = []
        for hd in range(NH):
            hh = hmv[:, hd * DH:(hd + 1) * DH]
            mu = jnp.mean(hh, axis=-1, keepdims=True)
            dv = hh - mu
            var = jnp.mean(dv * dv, axis=-1, keepdims=True)
            parts.append(dv * lax.rsqrt(var + EPS) * hnw[:, hd * DH:(hd + 1) * DH])
        hn = jnp.concatenate(parts, axis=1)
        return (hn * _silu(p_scr[rows, 4 * D:5 * D])).astype(BF16)

    def epilogue_mid(ci, rc, tm):
        rows = slice(ci * rc, (ci + 1) * rc)
        ym = _dot(tm, wpm_ref[...])
        nb = rc // tt if rc >= tt else 0
        if nb:
            msc = msc_ref[ci * nb:(ci + 1) * nb].reshape(rc, D)
        else:
            per = tt // rc
            msc = msc_ref[ci // per, (ci % per) * rc:(ci % per + 1) * rc, :]
        return (msc + sg_scr[rows, :] * ym).astype(BF16)

    def epilogue_tail(ci, rc, merged):
        nb = rc // tt if rc >= tt else 0
        if nb:
            xr = x_ref[ci * nb:(ci + 1) * nb].reshape(rc, D)
        else:
            per = tt // rc
            xr = x_ref[ci // per, (ci % per) * rc:(ci % per + 1) * rc, :]
        y = xr + _dot(merged, wout_ref[...])
        if final:
            y = _rmsnorm(y, fnw_ref[...])
        if nb:
            y_ref[ci * nb:(ci + 1) * nb] = y.reshape(nb, tt, D)
        else:
            y_ref[ci // per, (ci % per) * rc:(ci % per + 1) * rc, :] = y

    deferred = nsub == 1 and tt > SHORT_SEQ
    fillers = [f for p in range(4) for f in (functools.partial(oz_piece, p), functools.partial(merge_gate, p))
               ] if deferred else []

    def gate_chain(hrows, mprev):
        n = hrows.shape[0]
        gt = _dot_nt(wif_ref[...], hrows) + bias_ref[...]
        pos = jnp.bitwise_and(lax.broadcasted_iota(jnp.int32, (8, n), 1), tt - 1)
        bcum = _seg_scan(_log_sigmoid(gt[8:16, :]), pos, tt, jnp.add, 0.0)
        g = gt[0:8, :] - bcum
        mrun = jnp.maximum(mprev, _seg_scan(g, pos, tt, jnp.maximum, NEG))
        mlast = _seg_bcast_last(mrun, pos, tt)
        winter = jnp.exp(mprev - mrun)
        efloor = jnp.exp(-(bcum + mrun))
        wrow = jnp.exp(g - mlast)
        decay = jnp.exp(mprev - mlast)
        mnew = _seg_bcast_last(bcum, pos, tt) + mlast
        zt = jnp.concatenate([mrun, winter, efloor, wrow], axis=0).T
        return g, decay, zt, mnew

    if not single_tile:
        pl.when(step == 0)(init_state)
        c_prev, n_prev, m_start = c_ref, n_ref, (lambda: m_ref[...])
    elif has_state:
        c_prev, n_prev, m_start = c0_ref, n0_ref, (lambda: m0_ref[...])
    else:
        c_ref[...] = jnp.zeros_like(c_ref)
        n_ref[...] = jnp.zeros_like(n_ref)
        c_prev, n_prev, m_start = c_ref, n_ref, (lambda: jnp.zeros(m_ref.shape, F32))

    if nsub == 1:
        base = 0
        h = normalise()
        hp = h if rp == r else jnp.concatenate([h, jnp.zeros((rp - r, D), BF16)], axis=0)
        g, decay, zt, mnew = gate_chain(hp, m_start())
        m_ref[...] = mnew
        project(h, range(3) if deferred else range(5))
    else:
        base = pl.multiple_of(step * r, r)

        @pl.when(step == 0)
        def _block_prologue():
            hh = normalise()
            ga, da, za, mnew_all = gate_chain(hh, m_start())
            project(hh)
            m_ref[...] = mnew_all
            zt_scr[:, 0:32] = za
            for s in range(nsub):
                gd_scr[16 * s:16 * s + 8, 0:r] = ga[:, s * r:(s + 1) * r]
                gd_scr[16 * s + 8:16 * s + 16, 0:r] = da[:, s * r:(s + 1) * r]
            for piece in range(GATE_PIECES):
                merge_gate(piece)

        zt = zt_scr[pl.ds(base, r), 0:32]
        gd = gd_scr[pl.ds(pl.multiple_of(step * 16, 16), 16), :]
        g = gd[0:8, 0:r]
        decay = gd[8:16, 0:r]

    if tt <= SHORT_SEQ:
        nst = NH * r
        prow = slice(0, r) if nsub == 1 else pl.ds(base, r)
        stack = lambda off: jnp.concatenate(
            [p_scr[prow, off + hd * DH:off + (hd + 1) * DH] for hd in range(NH)], axis=0)
        qs = stack(0)
        ks = stack(D) * K_SCALE
        vs = stack(2 * D)
        col = lambda c0: jnp.concatenate([zt[0:r, c0 + hd:c0 + hd + 1] for hd in range(NH)], axis=0)
        mcol, wi, ef, wc = col(0), col(8), col(16), col(24)
        grow = jnp.concatenate([g[hd:hd + 1, 0:r] for hd in range(NH)], axis=1)
        row_i = lax.broadcasted_iota(jnp.int32, (nst, nst), 0)
        col_i = lax.broadcasted_iota(jnp.int32, (nst, nst), 1)
        shift = tt.bit_length() - 1
        same = lax.shift_right_logical(row_i, shift) == lax.shift_right_logical(col_i, shift)
        arg = jnp.where(col_i <= row_i, grow - mcol, NEG)
        dm = jnp.exp(jnp.where(same, arg, NEG))
        sc = _dot_nt(qs.astype(BF16), ks.astype(BF16)) * dm
        intra = _dot(sc.astype(BF16), vs.astype(BF16))
        pairs = [(hd, b) for hd in range(NH) for b in range(bt)]
        blk = lambda hd, b: slice(hd * r + b * tt, hd * r + (b + 1) * tt)
        inter = jnp.concatenate([_dot(qs[blk(hd, b)].astype(BF16), c_prev[b, hd].astype(BF16))
                                 for hd, b in pairs], axis=0)
        nfull = jnp.concatenate([jnp.broadcast_to(n_prev[b, hd:hd + 1, :], (tt, DH)) for hd, b in pairs], axis=0)
        num = wi * inter + intra
        den = wi * jnp.sum(qs * nfull, axis=-1, keepdims=True) + jnp.sum(sc, axis=-1, keepdims=True)
        hh = num * (1.0 / jnp.maximum(jnp.abs(den), ef))
        for hd in range(NH):
            hm_scr[prow, hd * DH:(hd + 1) * DH] = hh[hd * r:(hd + 1) * r]
        wk = ks * wc
        for hd, b in pairs:
            dec = decay[hd:hd + 1, b * tt:b * tt + 1]
            upd = _dot_tn(wk[blk(hd, b)].astype(BF16), vs[blk(hd, b)].astype(BF16))
            c_ref[b, hd] = dec * c_prev[b, hd] + upd
            n_ref[b, hd:hd + 1, :] = (dec * n_prev[b, hd:hd + 1, :]
                                      + jnp.sum(wk[blk(hd, b)], axis=0, keepdims=True))
        if nsub == 1:
            for piece in range(GATE_PIECES):
                merge_gate(piece)

    row_i = lax.broadcasted_iota(jnp.int32, (tt, tt), 0)
    col_i = lax.broadcasted_iota(jnp.int32, (tt, tt), 1)
    causal = row_i >= col_i

    for b in range(bt if tt > SHORT_SEQ else 0):
        rows = slice(b * tt, (b + 1) * tt)
        prow = rows if nsub == 1 else pl.ds(base + b * tt, tt)
        for hd in range(NH):
            cs = slice(hd * DH, (hd + 1) * DH)
            qb = qkv_scr[prow, cs]
            kb = qkv_scr[prow, D + hd * DH:D + (hd + 1) * DH]
            vb = qkv_scr[prow, 2 * D + hd * DH:2 * D + (hd + 1) * DH]
            qf = qb.astype(F32)
            kf = kb.astype(F32)
            mcol = zt[rows, hd:hd + 1]
            wi = zt[rows, 8 + hd:9 + hd]
            ef = zt[rows, 16 + hd:17 + hd]
            wc = zt[rows, 24 + hd:25 + hd]
            grow = g[hd:hd + 1, b * tt:(b + 1) * tt]
            dm = jnp.exp(jnp.where(causal, grow - mcol, NEG))
            s_raw = _dot_nt(qb, kb)
            cf = c_prev[b, hd]
            nrow = n_prev[b, hd:hd + 1, :]
            inter = _dot(qb, cf.astype(BF16))
            dec = decay[hd:hd + 1, b * tt:b * tt + 1]
            wk = kf * wc
            c_ref[b, hd] = dec * cf + _dot_tn(wk.astype(BF16), vb)
            n_ref[b, hd:hd + 1, :] = dec * nrow + jnp.sum(wk, axis=0, keepdims=True)
            sc = s_raw * dm
            num = wi * inter + _dot(sc.astype(BF16), vb)
            den = wi * jnp.sum(qf * nrow, axis=-1, keepdims=True) + jnp.sum(sc, axis=-1, keepdims=True)
            hm_scr[prow, cs] = num * (1.0 / jnp.maximum(jnp.abs(den), ef))
            done = b * NH + hd + 1
            for f in fillers[(done - 1) * len(fillers) // (bt * NH):done * len(fillers) // (bt * NH)]:
                f()

    def epilogue():
        rc = min(MLSTM_ROW_CHUNK, ro)
        chunks = range(ro // rc)
        tms = [epilogue_head(ci, rc) for ci in chunks]
        merged = [epilogue_mid(ci, rc, tms[ci]) for ci in chunks]
        for ci in chunks:
            epilogue_tail(ci, rc, merged[ci])

    if nsub == 1:
        epilogue()
    else:
        pl.when(step == nsub - 1)(epilogue)


def _mlstm_call(layer, depth, x, msc, weights, state, c_stack, bt, tt):
    b, t, _ = x.shape
    r = bt * tt
    rp = max(r, 128)
    nbt = b // bt
    has_state = state is not None
    has_alias = c_stack is not None
    final = layer == depth - 1
    nsub = max(1, min(nbt, PROJ_ROWS // r)) if t == tt else 1
    assert nbt % nsub == 0
    ro = r * nsub
    rpo = max(ro, 128)
    if nsub == 1:
        grid = (nbt, t // tt)
        tile_idx = lambda bi, si: (bi, si, 0)
        sub_idx = lambda bi, si: bi
    else:
        grid = (nbt // nsub, nsub)
        tile_idx = lambda bi, si: (bi, 0, 0)
        sub_idx = lambda bi, si: bi * nsub + si
    wspec = lambda shape: pl.BlockSpec((None,) + shape, lambda bi, si: (layer,) + (0,) * len(shape),
                                       pipeline_mode=pl.Buffered(1))
    tile = pl.BlockSpec((bt * nsub, tt, D), tile_idx)
    wcol = lambda k: pl.BlockSpec((None, D, D), lambda bi, si: (layer, 0, k), pipeline_mode=pl.Buffered(1))
    nw, w_main, w_if, bias, w_gates, hnw, wpm, wout, fnw = weights
    in_specs = ([tile, tile, wspec((1, D))] + [wcol(OFF_M // D + k) for k in range(5)]
                + [wspec((16, D)), wspec((16, 1)), wcol(2), wspec((1, D)), wspec((D, D)), wspec((D, D)),
                   pl.BlockSpec((1, D), lambda bi, si: (0, 0), pipeline_mode=pl.Buffered(1))])
    args = [x, msc, nw] + [w_main] * 5 + [w_if, bias, w_gates, hnw, wpm, wout, fnw]
    if has_state:
        in_specs += [pl.BlockSpec((None, bt, NH, DH, DH), lambda bi, si: (layer, sub_idx(bi, si), 0, 0, 0)),
                     pl.BlockSpec((None, bt, NH, DH), lambda bi, si: (layer, sub_idx(bi, si), 0, 0)),
                     pl.BlockSpec((None, None, 8, rpo), lambda bi, si: (layer, bi, 0, 0))]
        c0, n0, m0 = state
        args += [c0, n0, _expand_m(m0, bt * nsub, tt)]
    aliases = {}
    if has_alias:
        aliases = {len(args): 1}
        in_specs.append(pl.BlockSpec(memory_space=pl.ANY))
        args.append(c_stack)
    out_specs = [tile,
                 pl.BlockSpec((None, bt, NH, DH, DH), lambda bi, si: (layer, sub_idx(bi, si), 0, 0, 0)),
                 pl.BlockSpec((bt, NH, DH), lambda bi, si: (sub_idx(bi, si), 0, 0)),
                 pl.BlockSpec((None, 8, rpo), lambda bi, si: (bi, 0, 0))]
    out_shape = [jax.ShapeDtypeStruct((b, t, D), F32), jax.ShapeDtypeStruct((depth, b, NH, DH, DH), F32),
                 jax.ShapeDtypeStruct((b, NH, DH), F32), jax.ShapeDtypeStruct((nbt // nsub, 8, rpo), F32)]
    scratch = [pltpu.VMEM((ro, D), BF16),
               pltpu.VMEM((ro, 5 * D), F32),
               pltpu.VMEM((ro, D), F32),
               pltpu.VMEM((ro, D), F32),
               pltpu.VMEM((ro if tt > SHORT_SEQ else 16, 3 * D), BF16)]
    if nsub > 1:
        scratch += [pltpu.VMEM((ro, 128), F32),
                    pltpu.VMEM((16 * nsub, 128), F32)]
    y, c_new, n_new, m_rows = pl.pallas_call(
        functools.partial(_mlstm_kernel, has_state, has_alias, final, t == tt, bt, tt, nsub),
        grid=grid,
        in_specs=in_specs, out_specs=out_specs, out_shape=out_shape, scratch_shapes=scratch,
        input_output_aliases=aliases,
        compiler_params=pltpu.CompilerParams(dimension_semantics=("arbitrary", "arbitrary"),
                                             vmem_limit_bytes=VMEM_LIMIT),
        name="mlstm_state" if has_state else "mlstm",
    )(*args)
    return y, c_new, n_new, _collapse_m(m_rows, bt * nsub, tt)


def _expand_m(m, bt, tt):
    depth, b, _ = m.shape
    r = bt * tt
    rp = max(r, 128)
    v = jnp.repeat(m.reshape(depth, b // bt, bt, NH).transpose(0, 1, 3, 2), tt, axis=-1)
    return jnp.pad(v, ((0, 0), (0, 0), (0, 8 - NH), (0, rp - r)))


def _collapse_m(mrows, bt, tt):
    nbt = mrows.shape[0]
    return mrows[:, :NH, 0:bt * tt:tt].transpose(0, 2, 1).reshape(nbt * bt, NH)


def kernel(x_prompt, x_sample, state_ssm_re, state_ssm_im, state_conv, state_mlstm_c, state_mlstm_n, state_mlstm_m, norm_w, w_in, i_bias, f_bias, lam_re, lam_im, log_dt, b_re, b_im, c_re, c_im, d_skip, w_glu, w_proj_s, conv_w, w_proj_c, mlstm_norm_w, w_proj_m, w_out, final_norm_w):
    depth = norm_w.shape[0]
    bp, tp, _ = x_prompt.shape
    bs, ts, _ = x_sample.shape
    assert tp % PROMPT_CHUNK == 0 and ts % 8 == 0 and ts & (ts - 1) == 0 and bp % 8 == 0 and bs % 64 == 0

    a_re, a_im, bb_re, bb_im, cp_re, cp_im = _s5_discretise(lam_re, lam_im, log_dt, b_re, b_im, c_re, c_im)
    row = lambda a: a.reshape(depth, 1, a.shape[-1])
    nw = row(norm_w)
    w_t = jnp.swapaxes(w_in, 1, 2)
    w_main = _cast_columns(w_t, 0, OFF_IF // D)
    w_gates = _cast_columns(w_t, OFF_G, 3)
    gate_rows = lambda a: jnp.pad(a.reshape(depth, 2, NH, -1), ((0, 0), (0, 0), (0, 8 - NH), (0, 0))).reshape(depth, 16, -1)
    w_if = gate_rows(w_t[:, OFF_IF:OFF_G, :]).astype(BF16)
    bias = gate_rows(jnp.concatenate([i_bias, f_bias], axis=-1)[:, :, None])
    wa = [nw, w_main, w_gates, a_re, a_im, bb_re, bb_im, cp_re, cp_im, row(d_skip),
          w_glu.astype(BF16), w_proj_s.astype(BF16), conv_w, w_proj_c.astype(BF16)]
    wb = [nw, w_main, w_if, bias, w_gates, row(mlstm_norm_w), w_proj_m.astype(BF16), w_out.astype(BF16),
          final_norm_w.reshape(1, D)]

    sb_a, sb_b = 64, 4
    pb_a, pb_b = 8, 2
    st_a = (state_ssm_re.reshape(depth, bs, S5N), state_ssm_im.reshape(depth, bs, S5N), state_conv)
    st_b = (state_mlstm_c, state_mlstm_n, state_mlstm_m)

    yp, ys = x_prompt, x_sample
    cp = cs = None
    outs_p = [[] for _ in range(5)]
    outs_s = [[] for _ in range(5)]
    for l in range(depth):
        msc, sre, sim, cv = _s5conv_call(l, yp, wa, None, pb_a, S5_TILE_T)
        yp, cp, nn, mm = _mlstm_call(l, depth, yp, msc, wb, None, cp, pb_b, PROMPT_CHUNK)
        for i, v in enumerate((sre.reshape(bp, S5G, S5P), sim.reshape(bp, S5G, S5P), cv, nn, mm)):
            outs_p[i].append(v)
        msc, sre, sim, cv = _s5conv_call(l, ys, wa, st_a, sb_a, ts)
        ys, cs, nn, mm = _mlstm_call(l, depth, ys, msc, wb, st_b, cs, sb_b, ts)
        for i, v in enumerate((sre.reshape(bs, S5G, S5P), sim.reshape(bs, S5G, S5P), cv, nn, mm)):
            outs_s[i].append(v)
    sp = [jnp.stack(o) for o in outs_p]
    ss = [jnp.stack(o) for o in outs_s]
    return (yp, ys, sp[0], ss[0], sp[1], ss[1], sp[2], ss[2], cp, cs, sp[3], ss[3], sp[4], ss[4])
```

```python
import functools
import math

import jax
import jax.numpy as jnp
from jax import lax
from jax.experimental import pallas as pl
from jax.experimental.pallas import tpu as pltpu

D = 1024
S5W = 512
S5G = 32
S5P = 64
S5C = 16
S5N = S5G * S5P
CW = 512
CK = 3
NH = 4
DH = 256
PROMPT_CHUNK = 256
S5_TILE_T = 64
SHORT_SEQ = 32
EPS = 1e-6
NEG = -1e30
K_SCALE = DH ** -0.5

OFF_S = 0
OFF_M = 3072
OFF_IF = 8192
OFF_G = 8200
IN_W = 11272

F32 = jnp.float32
BF16 = jnp.bfloat16
VMEM_LIMIT = 58 * 1024 * 1024
ROW_CHUNK = 256
GATE_PIECES = 4
MLSTM_ROW_CHUNK = 512
PROJ_ROWS = 256


def _sigmoid(x):
    return 1.0 / (1.0 + jnp.exp(-x))


def _silu(x):
    return x * _sigmoid(x)


def _gelu_tanh(x):
    return 0.5 * x * (1.0 + jnp.tanh(math.sqrt(2.0 / math.pi) * (x + 0.044715 * (x * x * x))))


def _log_sigmoid(x):
    return jnp.minimum(x, 0.0) - jnp.log1p(jnp.exp(-jnp.abs(x)))


def _rmsnorm(x, w):
    return x * lax.rsqrt(jnp.mean(x * x, axis=-1, keepdims=True) + EPS) * w


def _dot(a, b):
    return jnp.dot(a, b, preferred_element_type=F32)


def _dot_nt(a, b):
    return lax.dot_general(a, b, (((1,), (1,)), ((), ())), preferred_element_type=F32)


def _dot_tn(a, b):
    return lax.dot_general(a, b, (((0,), (0,)), ((), ())), preferred_element_type=F32)


def _pitch(n):
    p = n + 8
    return p if (p // 8) % 2 == 1 else p + 8


def _s5_disc_kernel(lre_ref, lim_ref, ldt_ref, bre_ref, bim_ref, cre_ref, cim_ref,
                    are_ref, aim_ref, bbre_ref, bbim_ref, cpre_ref, cpim_ref):
    lr = lre_ref[...]
    li = lim_ref[...]
    dt = jnp.exp(ldt_ref[...])
    ea = jnp.exp(lr * dt)
    ar = ea * jnp.cos(li * dt)
    ai = ea * jnp.sin(li * dt)
    are_ref[...] = ar
    aim_ref[...] = ai
    nr = ar - 1.0
    inv = 1.0 / (lr * lr + li * li)
    cr = (nr * lr + ai * li) * inv
    ci = (ai * lr - nr * li) * inv
    half = S5N // 2
    shr = lax.shift_right_logical
    bdiag = (shr(lax.broadcasted_iota(jnp.int32, (256, half), 0), 4)
             == shr(lax.broadcasted_iota(jnp.int32, (256, half), 1), 6))
    cdiag = (shr(lax.broadcasted_iota(jnp.int32, (half, 256), 0), 6)
             == shr(lax.broadcasted_iota(jnp.int32, (half, 256), 1), 4))
    rep = lambda x: jnp.concatenate([x] * 16, axis=0)
    for h in range(2):
        crh = cr[:, h * half:(h + 1) * half]
        cih = ci[:, h * half:(h + 1) * half]
        br = bre_ref[h]
        bi = bim_ref[h]
        bbre_ref[h] = jnp.where(bdiag, rep(crh * br - cih * bi), 0.0).astype(BF16)
        bbim_ref[h] = jnp.where(bdiag, rep(crh * bi + cih * br), 0.0).astype(BF16)
        cpre_ref[h] = jnp.where(cdiag, rep(cre_ref[h]), 0.0).astype(BF16)
        cpim_ref[h] = jnp.where(cdiag, rep(cim_ref[h]), 0.0).astype(BF16)


def _s5_discretise(lam_re, lam_im, log_dt, b_re, b_im, c_re, c_im):
    depth = lam_re.shape[0]

    def rows_b(b):
        return jnp.swapaxes(b.reshape(depth, 2, 16 * S5P, S5C), 2, 3)

    def rows_c(c):
        return jnp.swapaxes(c.reshape(depth, 2, 16 * S5C, S5P), 2, 3)

    row = lambda a: a.reshape(depth, 1, S5N)
    ldt = jnp.broadcast_to(log_dt[:, :, None], (depth, S5G, S5P))
    vec = pl.BlockSpec((None, 1, S5N), lambda l: (l, 0, 0))
    bmat = pl.BlockSpec((None, 2, 256, 1024), lambda l: (l, 0, 0, 0))
    cmat = pl.BlockSpec((None, 2, 1024, 256), lambda l: (l, 0, 0, 0))
    brow = pl.BlockSpec((None, 2, S5C, 1024), lambda l: (l, 0, 0, 0))
    crow = pl.BlockSpec((None, 2, S5P, 256), lambda l: (l, 0, 0, 0))
    return pl.pallas_call(
        _s5_disc_kernel,
        grid=(depth,),
        in_specs=[vec, vec, vec, brow, brow, crow, crow],
        out_specs=[vec, vec, bmat, bmat, cmat, cmat],
        out_shape=[jax.ShapeDtypeStruct((depth, 1, S5N), F32), jax.ShapeDtypeStruct((depth, 1, S5N), F32),
                   jax.ShapeDtypeStruct((depth, 2, 256, 1024), BF16),
                   jax.ShapeDtypeStruct((depth, 2, 256, 1024), BF16),
                   jax.ShapeDtypeStruct((depth, 2, 1024, 256), BF16),
                   jax.ShapeDtypeStruct((depth, 2, 1024, 256), BF16)],
        name="s5_discretise",
    )(row(lam_re), row(lam_im), row(ldt), rows_b(b_re), rows_b(b_im), rows_c(c_re), rows_c(c_im))


def _cast_kernel(shift, *refs):
    if shift:
        wa_ref, wb_ref, o_ref = refs
        w = jnp.concatenate([wa_ref[shift:, :], wb_ref[...]], axis=0)
    else:
        w_ref, o_ref = refs
        w = w_ref[...]
    o_ref[...] = w.T.astype(BF16)


def _cast_columns(w_t, first_col, n_blocks):
    depth = w_t.shape[0]
    k0, shift = divmod(first_col, D)
    assert shift % 8 == 0
    in_specs = [pl.BlockSpec((None, D, D), lambda l, k: (l, k0 + k, 0))]
    args = [w_t]
    if shift:
        assert D % shift == 0
        in_specs.append(pl.BlockSpec((None, shift, D), lambda l, k: (l, (k0 + k + 1) * (D // shift), 0)))
        args.append(w_t)
    return pl.pallas_call(
        functools.partial(_cast_kernel, shift),
        grid=(depth, n_blocks),
        in_specs=in_specs,
        out_specs=pl.BlockSpec((None, D, D), lambda l, k: (l, 0, k)),
        out_shape=jax.ShapeDtypeStruct((depth, D, n_blocks * D), BF16),
        name="cast_weights",
    )(*args)


def _s5conv_kernel(has_state, bt, tt, *refs):
    refs = list(refs)
    (x_ref, nw_ref, ws_ref, wg_ref, are_ref, aim_ref, bbre_ref, bbim_ref, cre_ref, cim_ref,
     dsk_ref, wglu_ref, wps_ref, cw_ref, wpc_ref) = refs[:15]
    refs = refs[15:]
    if has_state:
        s0re_ref, s0im_ref, cv0_ref = refs[:3]
        refs = refs[3:]
    msc_ref, sre_ref, sim_ref, cvo_ref = refs[:4]
    h_scr, ps_scr, g_scr, uslab, utb, bu_scr, yslab, y_scr, vhalo = refs[4:]

    r = bt * tt
    pt = _pitch(tt)
    pb = _pitch(bt)
    rc = min(ROW_CHUNK, r)
    nchunk = r // rc
    ti = pl.program_id(1)

    def merge_gates(ci):
        rows = slice(ci * rc, (ci + 1) * rc)
        g_scr[rows, :] = _sigmoid(_dot(h_scr[rows, :], wg_ref[...]))

    @pl.when(ti == 0)
    def _init():
        if has_state:
            sre_ref[...] = s0re_ref[...]
            sim_ref[...] = s0im_ref[...]
            vhalo[:, 6:8, :] = cv0_ref[...]
        else:
            sre_ref[...] = jnp.zeros_like(sre_ref)
            sim_ref[...] = jnp.zeros_like(sim_ref)
            vhalo[:, 6:8, :] = jnp.zeros((bt, 2, CW), F32)

    x = x_ref[...].reshape(r, D)
    h = _rmsnorm(x, nw_ref[...]).astype(BF16)
    h_scr[...] = h
    ps_scr[...] = _dot(h, ws_ref[...])


    for b in range(bt):
        for j in range(4):
            uslab[j, b * pt:b * pt + tt, :] = ps_scr[b * tt:(b + 1) * tt, j * 128:(j + 1) * 128]
    for t in range(tt):
        for bs in range(bt // 8):
            dst = t * bt + bs * 8
            for j in range(4):
                utb[dst:dst + 8, j * 128:(j + 1) * 128] = uslab[j, pl.ds(bs * 8 * pt + t, 8, stride=pt), :]
    for hf in range(2):
        uh = utb[:, hf * 256:(hf + 1) * 256].astype(BF16)
        bu_scr[:, hf * 1024:(hf + 1) * 1024] = _dot(uh, bbre_ref[hf])
        bu_scr[:, S5N + hf * 1024:S5N + (hf + 1) * 1024] = _dot(uh, bbim_ref[hf])
        for ci in range(hf * nchunk // 2, (hf + 1) * nchunk // 2):
            merge_gates(ci)

    v = ps_scr[:, 1536:2048] * ps_scr[:, 2048:2560]
    vhalo[:, 8:8 + tt, :] = v.reshape(bt, tt, CW)
    cw = cw_ref[...]
    yc = (cw[0:1, :] * vhalo[:, 6:6 + tt, :] + cw[1:2, :] * vhalo[:, 7:7 + tt, :]
          + cw[2:3, :] * vhalo[:, 8:8 + tt, :])
    ps_scr[:, 2048:2560] = yc.reshape(r, CW)
    new_halo = vhalo[:, tt + 6:tt + 8, :]
    vhalo[:, 6:8, :] = new_halo
    cvo_ref[...] = new_halo

    def conv_project(ci):
        rows = slice(ci * rc, (ci + 1) * rc)
        tc_ = ps_scr[rows, 1024:1536] * ps_scr[rows, 2048:2560] * _silu(ps_scr[rows, 2560:3072])
        ps_scr[rows, 1024:2048] = g_scr[rows, D:] * _dot(tc_.astype(BF16), wpc_ref[...])

    for ci in range(nchunk // 2):
        conv_project(ci)

    def scan_quarter(q):
        c_re = slice(q * 512, (q + 1) * 512)
        c_im = slice(S5N + q * 512, S5N + (q + 1) * 512)
        ar = jnp.broadcast_to(are_ref[:, c_re], (8, 512))
        ai = jnp.broadcast_to(aim_ref[:, c_re], (8, 512))
        for bs in range(bt // 8):
            r0 = bs * 8
            sr = sre_ref[r0:r0 + 8, c_re]
            si = sim_ref[r0:r0 + 8, c_re]
            for t in range(tt):
                row = t * bt + r0
                nr = ar * sr - ai * si + bu_scr[row:row + 8, c_re]
                ni = ar * si + ai * sr + bu_scr[row:row + 8, c_im]
                bu_scr[row:row + 8, c_re] = nr
                bu_scr[row:row + 8, c_im] = ni
                sr, si = nr, ni
            sre_ref[r0:r0 + 8, c_re] = sr
            sim_ref[r0:r0 + 8, c_re] = si

    for m in range(2):
        scan_quarter(2 * m)
        scan_quarter(2 * m + 1)
        sre = bu_scr[:, m * 1024:(m + 1) * 1024].astype(BF16)
        sim = bu_scr[:, S5N + m * 1024:S5N + (m + 1) * 1024].astype(BF16)
        ytb = _dot(sre, cre_ref[m]) - _dot(sim, cim_ref[m])
        for t in range(tt):
            for jj in range(2):
                yslab[2 * m + jj, t * pb:t * pb + bt, :] = ytb[t * bt:(t + 1) * bt, jj * 128:(jj + 1) * 128]
    for ci in range(nchunk // 2, nchunk):
        conv_project(ci)
    for b in range(bt):
        for ts in range(tt // 8):
            dst = b * tt + ts * 8
            for j in range(4):
                y_scr[dst:dst + 8, j * 128:(j + 1) * 128] = yslab[j, pl.ds(ts * 8 * pb + b, 8, stride=pb), :]

    nb = rc // tt if rc >= tt else 0
    dsk = dsk_ref[...]
    gls, gates = [], []
    for ci in range(nchunk):
        rows = slice(ci * rc, (ci + 1) * rc)
        gl = _gelu_tanh(y_scr[rows, :] + dsk * ps_scr[rows, 0:512])
        gls.append(gl)
        gates.append(_dot(gl.astype(BF16), wglu_ref[...]))
    for ci in range(nchunk):
        rows = slice(ci * rc, (ci + 1) * rc)
        glu = gls[ci] * _sigmoid(gates[ci])
        ts_ = glu * _silu(ps_scr[rows, 512:1024])
        ys = _dot(ts_.astype(BF16), wps_ref[...])
        out = g_scr[rows, :D] * ys + ps_scr[rows, 1024:2048]
        if nb:
            msc_ref[ci * nb:(ci + 1) * nb] = out.reshape(nb, tt, D)
        else:
            per = tt // rc
            msc_ref[ci // per, (ci % per) * rc:(ci % per + 1) * rc, :] = out


def _s5conv_call(layer, x, weights, state, bt, tt):
    b, t, _ = x.shape
    r = bt * tt
    has_state = state is not None
    wspec = lambda shape: pl.BlockSpec((None,) + shape, lambda bi, ti: (layer,) + (0,) * len(shape),
                                       pipeline_mode=pl.Buffered(1))
    tile = pl.BlockSpec((bt, tt, D), lambda bi, ti: (bi, ti, 0))
    in_specs = [tile, wspec((1, D)), wspec((D, 3072)), wspec((D, 2048)), wspec((1, S5N)), wspec((1, S5N)),
                wspec((2, 256, 1024)), wspec((2, 256, 1024)), wspec((2, 1024, 256)), wspec((2, 1024, 256)),
                wspec((1, S5W)), wspec((S5W, S5W)), wspec((S5W, D)), wspec((CK, CW)), wspec((CW, D))]
    args = [x] + list(weights)
    if has_state:
        in_specs += [pl.BlockSpec((None, bt, S5N), lambda bi, ti: (layer, bi, 0)),
                     pl.BlockSpec((None, bt, S5N), lambda bi, ti: (layer, bi, 0)),
                     pl.BlockSpec((None, bt, CK - 1, CW), lambda bi, ti: (layer, bi, 0, 0))]
        args += list(state)
    out_specs = [tile,
                 pl.BlockSpec((bt, S5N), lambda bi, ti: (bi, 0)),
                 pl.BlockSpec((bt, S5N), lambda bi, ti: (bi, 0)),
                 pl.BlockSpec((bt, CK - 1, CW), lambda bi, ti: (bi, 0, 0))]
    out_shape = [jax.ShapeDtypeStruct((b, t, D), F32), jax.ShapeDtypeStruct((b, S5N), F32),
                 jax.ShapeDtypeStruct((b, S5N), F32), jax.ShapeDtypeStruct((b, CK - 1, CW), F32)]
    scratch = [pltpu.VMEM((r, D), BF16),
               pltpu.VMEM((r, 3072), F32),
               pltpu.VMEM((r, 2 * D), F32),
               pltpu.VMEM((4, bt * _pitch(tt), 128), F32),
               pltpu.VMEM((r, S5W), F32),
               pltpu.VMEM((r, 2 * S5N), F32),
               pltpu.VMEM((4, tt * _pitch(bt), 128), F32),
               pltpu.VMEM((r, S5W), F32),
               pltpu.VMEM((bt, tt + 8, CW), F32)]
    return pl.pallas_call(
        functools.partial(_s5conv_kernel, has_state, bt, tt),
        grid=(b // bt, t // tt),
        in_specs=in_specs, out_specs=out_specs, out_shape=out_shape, scratch_shapes=scratch,
        compiler_params=pltpu.CompilerParams(dimension_semantics=("arbitrary", "arbitrary"),
                                             vmem_limit_bytes=VMEM_LIMIT),
        name="s5conv_state" if has_state else "s5conv",
    )(*args)


def _seg_scan(x, pos, seg, op, fill):
    s = 1
    while s < seg:
        x = op(x, jnp.where(pos >= s, pltpu.roll(x, s, 1), fill))
        s *= 2
    return x


def _seg_bcast_last(x, pos, seg):
    n = x.shape[1]
    s = 1
    while s < seg:
        x = jnp.where(pos + s <= seg - 1, pltpu.roll(x, n - s, 1), x)
        s *= 2
    return x


def _mlstm_kernel(has_state, has_alias, final, single_tile, bt, tt, nsub, layer, *refs):
    refs = list(refs)
    x_ref, msc_ref, nw_ref = refs[:3]
    wm_refs = refs[3:8]
    wif_ref, bias_ref, wgm_ref, hnw_ref, wpm_ref, wout_ref, fnw_ref = refs[8:15]
    refs = refs[15:]
    if has_state:
        c0_ref, n0_ref, m0_ref = refs[:3]
        refs = refs[3:]
    if has_alias:
        refs = refs[1:]
    y_ref, c_ref, n_ref, m_ref = refs[:4]
    h_scr, p_scr, hm_scr, sg_scr, qkv_scr = refs[4:9]
    if nsub > 1:
        zt_scr, gd_scr = refs[9:11]
    ring = single_tile and has_state and nsub > 1
    if ring:
        cring, csem = refs[11:13]

    r = bt * tt
    ro = r * nsub
    rp = max(r, 128)
    step = pl.program_id(1)

    def init_state():
        if has_state:
            c_ref[...] = c0_ref[...]
            n_ref[...] = n0_ref[...]
            m_ref[...] = m0_ref[...]
        else:
            c_ref[...] = jnp.zeros_like(c_ref)
            n_ref[...] = jnp.zeros_like(n_ref)
            m_ref[...] = jnp.zeros_like(m_ref)

    def normalise():
        hh = _rmsnorm(x_ref[...].reshape(ro, D), nw_ref[...]).astype(BF16)
        h_scr[...] = hh
        return hh

    def project(hh, blocks=range(5)):
        for k in blocks:
            res = _dot(hh, wm_refs[k][...])
            if tt > SHORT_SEQ and k < 3:
                qkv_scr[:, k * D:(k + 1) * D] = (res * K_SCALE if k == 1 else res).astype(BF16)
            else:
                p_scr[:, k * D:(k + 1) * D] = res

    def merge_gate(piece):
        cols = slice(piece * (D // GATE_PIECES), (piece + 1) * (D // GATE_PIECES))
        sg_scr[:, cols] = _sigmoid(_dot(h_scr[...], wgm_ref[:, cols]))

    def oz_piece(piece):
        k, j = 3 + piece // 2, piece % 2
        w = D // 2
        p_scr[:, k * D + j * w:k * D + (j + 1) * w] = _dot(h_scr[...], wm_refs[k][:, j * w:(j + 1) * w])

    def epilogue_head(ci, rc):
        rows = slice(ci * rc, (ci + 1) * rc)
        hnw = hnw_ref[...]
        hmv = hm_scr[rows, :] * _sigmoid(p_scr[rows, 3 * D:4 * D])
        parts = []
        for hd in range(NH):
            hh = hmv[:, hd * DH:(hd + 1) * DH]
            mu = jnp.mean(hh, axis=-1, keepdims=True)
            dv = hh - mu
            var = jnp.mean(dv * dv, axis=-1, keepdims=True)
            parts.append(dv * lax.rsqrt(var + EPS) * hnw[:, hd * DH:(hd + 1) * DH])
        hn = jnp.concatenate(parts, axis=1)
        return (hn * _silu(p_scr[rows, 4 * D:5 * D])).astype(BF16)

    def epilogue_mid(ci, rc, tm):
        rows = slice(ci * rc, (ci + 1) * rc)
        ym = _dot(tm, wpm_ref[...])
        nb = rc // tt if rc >= tt else 0
        if nb:
            msc = msc_ref[ci * nb:(ci + 1) * nb].reshape(rc, D)
        else:
            per = tt // rc
            msc = msc_ref[ci // per, (ci % per) * rc:(ci % per + 1) * rc, :]
        return (msc + sg_scr[rows, :] * ym).astype(BF16)

    def epilogue_tail(ci, rc, merged):
        nb = rc // tt if rc >= tt else 0
        if nb:
            xr = x_ref[ci * nb:(ci + 1) * nb].reshape(rc, D)
        else:
            per = tt // rc
            xr = x_ref[ci // per, (ci % per) * rc:(ci % per + 1) * rc, :]
        y = xr + _dot(merged, wout_ref[...])
        if final:
            y = _rmsnorm(y, fnw_ref[...])
        if nb:
            y_ref[ci * nb:(ci + 1) * nb] = y.reshape(nb, tt, D)
        else:
            y_ref[ci // per, (ci % per) * rc:(ci % per + 1) * rc, :] = y

    deferred = nsub == 1 and tt > SHORT_SEQ
    fillers = [f for p in range(4) for f in (functools.partial(oz_piece, p), functools.partial(merge_gate, p))
               ] if deferred else []

    def gate_chain(hrows, mprev):
        n = hrows.shape[0]
        gt = _dot_nt(wif_ref[...], hrows) + bias_ref[...]
        pos = jnp.bitwise_and(lax.broadcasted_iota(jnp.int32, (8, n), 1), tt - 1)
        bcum = _seg_scan(_log_sigmoid(gt[8:16, :]), pos, tt, jnp.add, 0.0)
        g = gt[0:8, :] - bcum
        mrun = jnp.maximum(mprev, _seg_scan(g, pos, tt, jnp.maximum, NEG))
        mlast = _seg_bcast_last(mrun, pos, tt)
        winter = jnp.exp(mprev - mrun)
        efloor = jnp.exp(-(bcum + mrun))
        wrow = jnp.exp(g - mlast)
        decay = jnp.exp(mprev - mlast)
        mnew = _seg_bcast_last(bcum, pos, tt) + mlast
        zt = jnp.concatenate([mrun, winter, efloor, wrow], axis=0).T
        return g, decay, zt, mnew

    if not single_tile:
        pl.when(step == 0)(init_state)
        c_prev, n_prev, m_start = c_ref, n_ref, (lambda: m_ref[...])
    elif has_state:
        c_prev, n_prev, m_start = c0_ref, n0_ref, (lambda: m0_ref[...])
        if ring:
            lin = pl.program_id(0) * nsub + step
            total = pl.num_programs(0) * nsub

            def c_copy(l):
                slot = lax.rem(l, 3)
                return pltpu.make_async_copy(c0_ref.at[layer, pl.ds(l * bt, bt)], cring.at[slot], csem.at[slot])

            @pl.when(lin == 0)
            def _prime():
                c_copy(lin).start()
                c_copy(lin + 1).start()

            @pl.when(lin + 2 < total)
            def _prefetch():
                c_copy(lin + 2).start()

            c_copy(lin).wait()
            c_prev = cring.at[lax.rem(lin, 3)]
    else:
        c_ref[...] = jnp.zeros_like(c_ref)
        n_ref[...] = jnp.zeros_like(n_ref)
        c_prev, n_prev, m_start = c_ref, n_ref, (lambda: jnp.zeros(m_ref.shape, F32))

    if nsub == 1:
        base = 0
        h = normalise()
        hp = h if rp == r else jnp.concatenate([h, jnp.zeros((rp - r, D), BF16)], axis=0)
        g, decay, zt, mnew = gate_chain(hp, m_start())
        m_ref[...] = mnew
        project(h, range(3) if deferred else range(5))
    else:
        base = pl.multiple_of(step * r, r)

        @pl.when(step == 0)
        def _block_prologue():
            hh = normalise()
            ga, da, za, mnew_all = gate_chain(hh, m_start())
            project(hh)
            m_ref[...] = mnew_all
            zt_scr[:, 0:32] = za
            for s in range(nsub):
                gd_scr[16 * s:16 * s + 8, 0:r] = ga[:, s * r:(s + 1) * r]
                gd_scr[16 * s + 8:16 * s + 16, 0:r] = da[:, s * r:(s + 1) * r]
            for piece in range(GATE_PIECES):
                merge_gate(piece)

        zt = zt_scr[pl.ds(base, r), 0:32]
        gd = gd_scr[pl.ds(pl.multiple_of(step * 16, 16), 16), :]
        g = gd[0:8, 0:r]
        decay = gd[8:16, 0:r]

    if tt <= SHORT_SEQ:
        nst = NH * r
        prow = slice(0, r) if nsub == 1 else pl.ds(base, r)
        stack = lambda off: jnp.concatenate(
            [p_scr[prow, off + hd * DH:off + (hd + 1) * DH] for hd in range(NH)], axis=0)
        qs = stack(0)
        ks = stack(D) * K_SCALE
        vs = stack(2 * D)
        col = lambda c0: jnp.concatenate([zt[0:r, c0 + hd:c0 + hd + 1] for hd in range(NH)], axis=0)
        mcol, wi, ef, wc = col(0), col(8), col(16), col(24)
        grow = jnp.concatenate([g[hd:hd + 1, 0:r] for hd in range(NH)], axis=1)
        row_i = lax.broadcasted_iota(jnp.int32, (nst, nst), 0)
        col_i = lax.broadcasted_iota(jnp.int32, (nst, nst), 1)
        shift = tt.bit_length() - 1
        same = lax.shift_right_logical(row_i, shift) == lax.shift_right_logical(col_i, shift)
        arg = jnp.where(col_i <= row_i, grow - mcol, NEG)
        dm = jnp.exp(jnp.where(same, arg, NEG))
        sc = _dot_nt(qs.astype(BF16), ks.astype(BF16)) * dm
        intra = _dot(sc.astype(BF16), vs.astype(BF16))
        pairs = [(hd, b) for hd in range(NH) for b in range(bt)]
        blk = lambda hd, b: slice(hd * r + b * tt, hd * r + (b + 1) * tt)
        inter = jnp.concatenate([_dot(qs[blk(hd, b)].astype(BF16), c_prev[b, hd].astype(BF16))
                                 for hd, b in pairs], axis=0)
        nfull = jnp.concatenate([jnp.broadcast_to(n_prev[b, hd:hd + 1, :], (tt, DH)) for hd, b in pairs], axis=0)
        num = wi * inter + intra
        den = wi * jnp.sum(qs * nfull, axis=-1, keepdims=True) + jnp.sum(sc, axis=-1, keepdims=True)
        hh = num * (1.0 / jnp.maximum(jnp.abs(den), ef))
        for hd in range(NH):
            hm_scr[prow, hd * DH:(hd + 1) * DH] = hh[hd * r:(hd + 1) * r]
        wk = ks * wc
        for hd, b in pairs:
            dec = decay[hd:hd + 1, b * tt:b * tt + 1]
            upd = _dot_tn(wk[blk(hd, b)].astype(BF16), vs[blk(hd, b)].astype(BF16))
            c_ref[b, hd] = dec * c_prev[b, hd] + upd
            n_ref[b, hd:hd + 1, :] = (dec * n_prev[b, hd:hd + 1, :]
                                      + jnp.sum(wk[blk(hd, b)], axis=0, keepdims=True))
        if nsub == 1:
            for piece in range(GATE_PIECES):
                merge_gate(piece)

    row_i = lax.broadcasted_iota(jnp.int32, (tt, tt), 0)
    col_i = lax.broadcasted_iota(jnp.int32, (tt, tt), 1)
    causal = row_i >= col_i

    for b in range(bt if tt > SHORT_SEQ else 0):
        rows = slice(b * tt, (b + 1) * tt)
        prow = rows if nsub == 1 else pl.ds(base + b * tt, tt)
        for hd in range(NH):
            cs = slice(hd * DH, (hd + 1) * DH)
            qb = qkv_scr[prow, cs]
            kb = qkv_scr[prow, D + hd * DH:D + (hd + 1) * DH]
            vb = qkv_scr[prow, 2 * D + hd * DH:2 * D + (hd + 1) * DH]
            qf = qb.astype(F32)
            kf = kb.astype(F32)
            mcol = zt[rows, hd:hd + 1]
            wi = zt[rows, 8 + hd:9 + hd]
            ef = zt[rows, 16 + hd:17 + hd]
            wc = zt[rows, 24 + hd:25 + hd]
            grow = g[hd:hd + 1, b * tt:(b + 1) * tt]
            dm = jnp.exp(jnp.where(causal, grow - mcol, NEG))
            s_raw = _dot_nt(qb, kb)
            cf = c_prev[b, hd]
            nrow = n_prev[b, hd:hd + 1, :]
            inter = _dot(qb, cf.astype(BF16))
            dec = decay[hd:hd + 1, b * tt:b * tt + 1]
            wk = kf * wc
            c_ref[b, hd] = dec * cf + _dot_tn(wk.astype(BF16), vb)
            n_ref[b, hd:hd + 1, :] = dec * nrow + jnp.sum(wk, axis=0, keepdims=True)
            sc = s_raw * dm
            num = wi * inter + _dot(sc.astype(BF16), vb)
            den = wi * jnp.sum(qf * nrow, axis=-1, keepdims=True) + jnp.sum(sc, axis=-1, keepdims=True)
            hm_scr[prow, cs] = num * (1.0 / jnp.maximum(jnp.abs(den), ef))
            done = b * NH + hd + 1
            for f in fillers[(done - 1) * len(fillers) // (bt * NH):done * len(fillers) // (bt * NH)]:
                f()

    def epilogue():
        rc = min(MLSTM_ROW_CHUNK, ro)
        chunks = range(ro // rc)
        tms = [epilogue_head(ci, rc) for ci in chunks]
        merged = [epilogue_mid(ci, rc, tms[ci]) for ci in chunks]
        for ci in chunks:
            epilogue_tail(ci, rc, merged[ci])

    if nsub == 1:
        epilogue()
    else:
        pl.when(step == nsub - 1)(epilogue)


def _mlstm_call(layer, depth, x, msc, weights, state, c_stack, bt, tt):
    b, t, _ = x.shape
    r = bt * tt
    rp = max(r, 128)
    nbt = b // bt
    has_state = state is not None
    has_alias = c_stack is not None
    final = layer == depth - 1
    nsub = max(1, min(nbt, PROJ_ROWS // r)) if t == tt else 1
    assert nbt % nsub == 0
    ro = r * nsub
    rpo = max(ro, 128)
    if nsub == 1:
        grid = (nbt, t // tt)
        tile_idx = lambda bi, si: (bi, si, 0)
        sub_idx = lambda bi, si: bi
    else:
        grid = (nbt // nsub, nsub)
        tile_idx = lambda bi, si: (bi, 0, 0)
        sub_idx = lambda bi, si: bi * nsub + si
    wspec = lambda shape: pl.BlockSpec((None,) + shape, lambda bi, si: (layer,) + (0,) * len(shape),
                                       pipeline_mode=pl.Buffered(1))
    tile = pl.BlockSpec((bt * nsub, tt, D), tile_idx)
    wcol = lambda k: pl.BlockSpec((None, D, D), lambda bi, si: (layer, 0, k), pipeline_mode=pl.Buffered(1))
    nw, w_main, w_if, bias, w_gates, hnw, wpm, wout, fnw = weights
    in_specs = ([tile, tile, wspec((1, D))] + [wcol(OFF_M // D + k) for k in range(5)]
                + [wspec((16, D)), wspec((16, 1)), wcol(2), wspec((1, D)), wspec((D, D)), wspec((D, D)),
                   pl.BlockSpec((1, D), lambda bi, si: (0, 0), pipeline_mode=pl.Buffered(1))])
    args = [x, msc, nw] + [w_main] * 5 + [w_if, bias, w_gates, hnw, wpm, wout, fnw]
    ring = has_state and t == tt and nsub > 1
    if has_state:
        c_spec = (pl.BlockSpec(memory_space=pl.ANY) if ring else
                  pl.BlockSpec((None, bt, NH, DH, DH), lambda bi, si: (layer, sub_idx(bi, si), 0, 0, 0)))
        in_specs += [c_spec,
                     pl.BlockSpec((None, bt, NH, DH), lambda bi, si: (layer, sub_idx(bi, si), 0, 0)),
                     pl.BlockSpec((None, None, 8, rpo), lambda bi, si: (layer, bi, 0, 0))]
        c0, n0, m0 = state
        args += [c0, n0, _expand_m(m0, bt * nsub, tt)]
    aliases = {}
    if has_alias:
        aliases = {len(args): 1}
        in_specs.append(pl.BlockSpec(memory_space=pl.ANY))
        args.append(c_stack)
    out_specs = [tile,
                 pl.BlockSpec((None, bt, NH, DH, DH), lambda bi, si: (layer, sub_idx(bi, si), 0, 0, 0)),
                 pl.BlockSpec((bt, NH, DH), lambda bi, si: (sub_idx(bi, si), 0, 0)),
                 pl.BlockSpec((None, 8, rpo), lambda bi, si: (bi, 0, 0))]
    out_shape = [jax.ShapeDtypeStruct((b, t, D), F32), jax.ShapeDtypeStruct((depth, b, NH, DH, DH), F32),
                 jax.ShapeDtypeStruct((b, NH, DH), F32), jax.ShapeDtypeStruct((nbt // nsub, 8, rpo), F32)]
    scratch = [pltpu.VMEM((ro, D), BF16),
               pltpu.VMEM((ro, 5 * D), F32),
               pltpu.VMEM((ro, D), F32),
               pltpu.VMEM((ro, D), F32),
               pltpu.VMEM((ro if tt > SHORT_SEQ else 16, 3 * D), BF16)]
    if nsub > 1:
        scratch += [pltpu.VMEM((ro, 128), F32),
                    pltpu.VMEM((16 * nsub, 128), F32)]
    if ring:
        scratch += [pltpu.VMEM((3, bt, NH, DH, DH), F32), pltpu.SemaphoreType.DMA((3,))]
    y, c_new, n_new, m_rows = pl.pallas_call(
        functools.partial(_mlstm_kernel, has_state, has_alias, final, t == tt, bt, tt, nsub, layer),
        grid=grid,
        in_specs=in_specs, out_specs=out_specs, out_shape=out_shape, scratch_shapes=scratch,
        input_output_aliases=aliases,
        compiler_params=pltpu.CompilerParams(dimension_semantics=("arbitrary", "arbitrary"),
                                             vmem_limit_bytes=VMEM_LIMIT),
        name="mlstm_state" if has_state else "mlstm",
    )(*args)
    return y, c_new, n_new, _collapse_m(m_rows, bt * nsub, tt)


def _expand_m(m, bt, tt):
    depth, b, _ = m.shape
    r = bt * tt
    rp = max(r, 128)
    v = jnp.repeat(m.reshape(depth, b // bt, bt, NH).transpose(0, 1, 3, 2), tt, axis=-1)
    return jnp.pad(v, ((0, 0), (0, 0), (0, 8 - NH), (0, rp - r)))


def _collapse_m(mrows, bt, tt):
    nbt = mrows.shape[0]
    return mrows[:, :NH, 0:bt * tt:tt].transpose(0, 2, 1).reshape(nbt * bt, NH)


def kernel(x_prompt, x_sample, state_ssm_re, state_ssm_im, state_conv, state_mlstm_c, state_mlstm_n, state_mlstm_m, norm_w, w_in, i_bias, f_bias, lam_re, lam_im, log_dt, b_re, b_im, c_re, c_im, d_skip, w_glu, w_proj_s, conv_w, w_proj_c, mlstm_norm_w, w_proj_m, w_out, final_norm_w):
    depth = norm_w.shape[0]
    bp, tp, _ = x_prompt.shape
    bs, ts, _ = x_sample.shape
    assert tp % PROMPT_CHUNK == 0 and ts % 8 == 0 and ts & (ts - 1) == 0 and bp % 8 == 0 and bs % 64 == 0

    a_re, a_im, bb_re, bb_im, cp_re, cp_im = _s5_discretise(lam_re, lam_im, log_dt, b_re, b_im, c_re, c_im)
    row = lambda a: a.reshape(depth, 1, a.shape[-1])
    nw = row(norm_w)
    w_t = jnp.swapaxes(w_in, 1, 2)
    w_main = _cast_columns(w_t, 0, OFF_IF // D)
    w_gates = _cast_columns(w_t, OFF_G, 3)
    gate_rows = lambda a: jnp.pad(a.reshape(depth, 2, NH, -1), ((0, 0), (0, 0), (0, 8 - NH), (0, 0))).reshape(depth, 16, -1)
    w_if = gate_rows(w_t[:, OFF_IF:OFF_G, :]).astype(BF16)
    bias = gate_rows(jnp.concatenate([i_bias, f_bias], axis=-1)[:, :, None])
    wa = [nw, w_main, w_gates, a_re, a_im, bb_re, bb_im, cp_re, cp_im, row(d_skip),
          w_glu.astype(BF16), w_proj_s.astype(BF16), conv_w, w_proj_c.astype(BF16)]
    wb = [nw, w_main, w_if, bias, w_gates, row(mlstm_norm_w), w_proj_m.astype(BF16), w_out.astype(BF16),
          final_norm_w.reshape(1, D)]

    sb_a, sb_b = 64, 4
    pb_a, pb_b = 8, 2
    st_a = (state_ssm_re.reshape(depth, bs, S5N), state_ssm_im.reshape(depth, bs, S5N), state_conv)
    st_b = (state_mlstm_c, state_mlstm_n, state_mlstm_m)

    yp, ys = x_prompt, x_sample
    cp = cs = None
    outs_p = [[] for _ in range(5)]
    outs_s = [[] for _ in range(5)]
    for l in range(depth):
        msc, sre, sim, cv = _s5conv_call(l, yp, wa, None, pb_a, S5_TILE_T)
        yp, cp, nn, mm = _mlstm_call(l, depth, yp, msc, wb, None, cp, pb_b, PROMPT_CHUNK)
        for i, v in enumerate((sre.reshape(bp, S5G, S5P), sim.reshape(bp, S5G, S5P), cv, nn, mm)):
            outs_p[i].append(v)
        msc, sre, sim, cv = _s5conv_call(l, ys, wa, st_a, sb_a, ts)
        ys, cs, nn, mm = _mlstm_call(l, depth, ys, msc, wb, st_b, cs, sb_b, ts)
        for i, v in enumerate((sre.reshape(bs, S5G, S5P), sim.reshape(bs, S5G, S5P), cv, nn, mm)):
            outs_s[i].append(v)
    sp = [jnp.stack(o) for o in outs_p]
    ss = [jnp.stack(o) for o in outs_s]
    return (yp, ys, sp[0], ss[0], sp[1], ss[1], sp[2], ss[2], cp, cs, sp[3], ss[3], sp[4], ss[4])
```
